```python
import jax, jax.numpy as jnp
from jax import lax
import numpy as np

D_MODEL = 1024
BATCH = 4
SEQ = 8192
DEPTH = 2
DEC_BATCH = 8
DEC_SEQ = 32
PAST_LEN = 1024

CHUNK = 64
ROPE_THETA = 500000.0
EPS = 1e-6
Q_BLOCK = 128
MLA_HEADS = 8
MLA_NOPE = 64
MLA_ROPE = 32
MLA_QK = MLA_NOPE + MLA_ROPE
MLA_V = 64
MLA_Q_LORA = 256
MLA_KV_LORA = 128
DSA_HEADS = 8
DSA_HD = 64
DSA_ROT = DSA_HD // 4
IDX_HEADS = 4
IDX_HD = 64
IDX_ROT = IDX_HD // 4
IDX_W_SCALE = (IDX_HD * IDX_HEADS) ** -0.5
TOPK_MAX = 256
MLA_WIDTH = MLA_HEADS * MLA_V
DSA_WIDTH = DSA_HEADS * DSA_HD
MIX_WIDTH = MLA_WIDTH + DSA_WIDTH
D_FF = 2816
IN_SIZES = (MLA_Q_LORA, MLA_KV_LORA, MLA_ROPE, DSA_WIDTH, DSA_WIDTH, DSA_WIDTH, IDX_HEADS * IDX_HD, IDX_HD, IDX_HEADS)
IN_COLS = MLA_Q_LORA + MLA_KV_LORA + MLA_ROPE + 3 * DSA_WIDTH + IDX_HEADS * IDX_HD + IDX_HD + IDX_HEADS

kernel_name = "hymba_mla_dsa_macaron_stream_step"


def rms_norm(x, g):
    xf = x.astype(jnp.float32)
    y = xf * lax.rsqrt(jnp.mean(xf * xf, axis=-1, keepdims=True) + EPS)
    return (y * g.astype(jnp.float32)).astype(x.dtype)


def rope_tables(pos, rot_dim):
    inv = 1.0 / (ROPE_THETA ** (jnp.arange(0, rot_dim, 2, dtype=jnp.float32) / rot_dim))
    ang = pos.astype(jnp.float32)[:, None] * inv[None, :]
    return jnp.cos(ang), jnp.sin(ang)


def apply_rope(x, cos, sin):
    xf = x.astype(jnp.float32)
    half = xf.shape[-1] // 2
    x1, x2 = xf[..., :half], xf[..., half:]
    c, s = cos[None, :, None, :], sin[None, :, None, :]
    return jnp.concatenate([x1 * c - x2 * s, x2 * c + x1 * s], axis=-1).astype(x.dtype)


def partial_rope(x, cos, sin, rot):
    return jnp.concatenate([apply_rope(x[..., :rot], cos, sin), x[..., rot:]], axis=-1)


def swiglu(x, wg, wu, wd):
    return (jax.nn.silu(x @ wg) * (x @ wu)) @ wd


def split_cols(z):
    outs, off = [], 0
    for n in IN_SIZES:
        outs.append(z[..., off:off + n])
        off += n
    return outs


def block_size(t):
    return t if t <= Q_BLOCK else Q_BLOCK


def to_blocks(a, qb):
    b, t = a.shape[0], a.shape[1]
    return jnp.moveaxis(a.reshape((b, t // qb, qb) + a.shape[2:]), 1, 0)


def from_blocks(a):
    a = jnp.moveaxis(a, 0, 1)
    return a.reshape((a.shape[0], a.shape[1] * a.shape[2]) + a.shape[3:])


def dense_chunk_attention(q, k, v, q_pos, k_pos):
    scale = q.shape[-1] ** -0.5
    qb = block_size(q.shape[1])
    k_chunk = k_pos // CHUNK

    def one_block(args):
        qq, pp = args
        s = jnp.einsum('bqhd,bshd->bhqs', qq, k).astype(jnp.float32) * scale
        mask = k_chunk[None, :] <= (pp // CHUNK)[:, None]
        s = jnp.where(mask[None, None], s, -jnp.inf)
        p = jax.nn.softmax(s, axis=-1).astype(v.dtype)
        return jnp.einsum('bhqs,bshd->bqhd', p, v)

    out = lax.map(one_block, (to_blocks(q, qb), q_pos.reshape(-1, qb)))
    return from_blocks(out)


def dsa_sparse_attention(q, k, v, iq, ik, iw, q_pos, k_pos):
    n_keys = k.shape[1]
    topk = min(TOPK_MAX, n_keys // 4)
    scale = q.shape[-1] ** -0.5
    qb = block_size(q.shape[1])
    k_chunk = k_pos // CHUNK

    def one_block(args):
        qq, iqq, iww, pp = args
        mask = k_chunk[None, :] <= (pp // CHUNK)[:, None]
        idx_logit = jnp.einsum('bqhd,bsd->bqhs', iqq, ik).astype(jnp.float32)
        score = jnp.einsum('bqh,bqhs->bqs', iww.astype(jnp.float32) * IDX_W_SCALE, jax.nn.relu(idx_logit))
        score = jnp.where(mask[None], score, -jnp.inf)
        top_val, top_idx = lax.top_k(score, topk)
        valid = jnp.isfinite(top_val)
        kg = jax.vmap(lambda kk, ii: kk[ii])(k, top_idx)
        vg = jax.vmap(lambda vv, ii: vv[ii])(v, top_idx)
        s = jnp.einsum('bqhd,bqkhd->bhqk', qq, kg).astype(jnp.float32) * scale
        s = jnp.where(valid[:, None], s, -jnp.inf)
        p = jax.nn.softmax(s, axis=-1).astype(vg.dtype)
        return jnp.einsum('bhqk,bqkhd->bqhd', p, vg)

    out = lax.map(one_block, (to_blocks(q, qb), to_blocks(iq, qb), to_blocks(iw, qb), q_pos.reshape(-1, qb)))
    return from_blocks(out)


def trunk_layer(x, pos_q, past, p):
    b, t = x.shape[0], x.shape[1]
    h = x + 0.5 * swiglu(rms_norm(x, p['ffn1_norm']), p['ffn1_w_gate'], p['ffn1_w_up'], p['ffn1_w_down'])
    u = rms_norm(h, p['mix_norm'])
    c_q, c_kv, k_r, q_b, k_b, v_b, q_i, k_i, w_i = split_cols(u @ p['w_in'])
    cos_a, sin_a = rope_tables(pos_q, MLA_ROPE)
    cos_b, sin_b = rope_tables(pos_q, DSA_ROT)

    q_a = (rms_norm(c_q, p['mla_q_norm']) @ p['mla_w_uq']).reshape(b, t, MLA_HEADS, MLA_QK)
    q_a = jnp.concatenate([q_a[..., :MLA_NOPE], apply_rope(q_a[..., MLA_NOPE:], cos_a, sin_a)], axis=-1)
    q_a = rms_norm(q_a, p['mla_q_gain'])
    c_kv = rms_norm(c_kv, p['mla_kv_norm'])
    k_r = apply_rope(k_r[:, :, None, :], cos_a, sin_a)[:, :, 0, :]

    q_b = partial_rope(rms_norm(q_b.reshape(b, t, DSA_HEADS, DSA_HD), p['dsa_q_gain']), cos_b, sin_b, DSA_ROT)
    k_b = partial_rope(rms_norm(k_b.reshape(b, t, DSA_HEADS, DSA_HD), p['dsa_k_gain']), cos_b, sin_b, DSA_ROT)
    v_b = v_b.reshape(b, t, DSA_HEADS, DSA_HD)
    q_i = partial_rope(q_i.reshape(b, t, IDX_HEADS, IDX_HD), cos_b, sin_b, IDX_ROT)
    k_i = partial_rope(k_i[:, :, None, :], cos_b, sin_b, IDX_ROT)[:, :, 0, :]

    if past is None:
        ckv_all, kr_all, kb_all, vb_all, ki_all = c_kv, k_r, k_b, v_b, k_i
    else:
        ckv_all = jnp.concatenate([past[0], c_kv], axis=1)
        kr_all = jnp.concatenate([past[1], k_r], axis=1)
        kb_all = jnp.concatenate([past[2], k_b], axis=1)
        vb_all = jnp.concatenate([past[3], v_b], axis=1)
        ki_all = jnp.concatenate([past[4], k_i], axis=1)
    n_keys = ckv_all.shape[1]
    pos_k = jnp.arange(n_keys, dtype=jnp.int32)

    kv = (ckv_all @ p['mla_w_ukv']).reshape(b, n_keys, MLA_HEADS, MLA_NOPE + MLA_V)
    k_a = jnp.concatenate([kv[..., :MLA_NOPE], jnp.broadcast_to(kr_all[:, :, None, :], (b, n_keys, MLA_HEADS, MLA_ROPE))], axis=-1)
    k_a = rms_norm(k_a, p['mla_k_gain'])
    o_a = dense_chunk_attention(q_a, k_a, kv[..., MLA_NOPE:], pos_q, pos_k)

    o_b = dsa_sparse_attention(q_b, kb_all, vb_all, q_i, ki_all, w_i, pos_q, pos_k)

    o = jnp.concatenate([o_a.reshape(b, t, MLA_WIDTH), o_b.reshape(b, t, DSA_WIDTH)], axis=-1)
    h = h + o @ p['w_out']
    y = h + 0.5 * swiglu(rms_norm(h, p['ffn2_norm']), p['ffn2_w_gate'], p['ffn2_w_up'], p['ffn2_w_down'])
    return y, (c_kv, k_r, k_b, v_b, k_i)


def setup_inputs(seed: int = 0) -> dict:
    key = jax.random.key(seed)
    ks = jax.random.split(key, 32)
    f32 = jnp.float32

    def nrm(k, shape, scale=1.0):
        return jax.random.normal(k, shape, f32) * scale

    def gain(k, n):
        return 1.0 + 0.02 * jax.random.normal(k, (DEPTH, n), f32)

    return {
        'x_prompt': nrm(ks[0], (BATCH, SEQ, D_MODEL)),
        'x_sample': nrm(ks[1], (DEC_BATCH, DEC_SEQ, D_MODEL)),
        'cache_mla_ckv': nrm(ks[2], (DEPTH, DEC_BATCH, PAST_LEN, MLA_KV_LORA)),
        'cache_mla_krope': nrm(ks[3], (DEPTH, DEC_BATCH, PAST_LEN, MLA_ROPE)),
        'cache_dsa_k': nrm(ks[4], (DEPTH, DEC_BATCH, PAST_LEN, DSA_HEADS, DSA_HD)),
        'cache_dsa_v': nrm(ks[5], (DEPTH, DEC_BATCH, PAST_LEN, DSA_HEADS, DSA_HD)),
        'cache_idx_k': nrm(ks[6], (DEPTH, DEC_BATCH, PAST_LEN, IDX_HD)),
        'ffn1_norm': gain(ks[7], D_MODEL),
        'ffn1_w_gate': nrm(ks[8], (DEPTH, D_MODEL, D_FF), D_MODEL ** -0.5),
        'ffn1_w_up': nrm(ks[9], (DEPTH, D_MODEL, D_FF), D_MODEL ** -0.5),
        'ffn1_w_down': nrm(ks[10], (DEPTH, D_FF, D_MODEL), D_FF ** -0.5),
        'mix_norm': gain(ks[11], D_MODEL),
        'w_in': nrm(ks[12], (DEPTH, D_MODEL, IN_COLS), D_MODEL ** -0.5),
        'mla_q_norm': gain(ks[13], MLA_Q_LORA),
        'mla_w_uq': nrm(ks[14], (DEPTH, MLA_Q_LORA, MLA_HEADS * MLA_QK), MLA_Q_LORA ** -0.5),
        'mla_kv_norm': gain(ks[15], MLA_KV_LORA),
        'mla_w_ukv': nrm(ks[16], (DEPTH, MLA_KV_LORA, MLA_HEADS * (MLA_NOPE + MLA_V)), MLA_KV_LORA ** -0.5),
        'mla_q_gain': gain(ks[17], MLA_QK),
        'mla_k_gain': gain(ks[18], MLA_QK),
        'dsa_q_gain': gain(ks[19], DSA_HD),
        'dsa_k_gain': gain(ks[20], DSA_HD),
        'w_out': nrm(ks[21], (DEPTH, MIX_WIDTH, D_MODEL), MIX_WIDTH ** -0.5),
        'ffn2_norm': gain(ks[22], D_MODEL),
        'ffn2_w_gate': nrm(ks[23], (DEPTH, D_MODEL, D_FF), D_MODEL ** -0.5),
        'ffn2_w_up': nrm(ks[24], (DEPTH, D_MODEL, D_FF), D_MODEL ** -0.5),
        'ffn2_w_down': nrm(ks[25], (DEPTH, D_FF, D_MODEL), D_FF ** -0.5),
    }


def reference(x_prompt, x_sample, cache_mla_ckv, cache_mla_krope, cache_dsa_k, cache_dsa_v, cache_idx_k,
              ffn1_norm, ffn1_w_gate, ffn1_w_up, ffn1_w_down, mix_norm, w_in,
              mla_q_norm, mla_w_uq, mla_kv_norm, mla_w_ukv, mla_q_gain, mla_k_gain,
              dsa_q_gain, dsa_k_gain, w_out, ffn2_norm, ffn2_w_gate, ffn2_w_up, ffn2_w_down):
    def layer_params(l):
        return {
            'ffn1_norm': ffn1_norm[l], 'ffn1_w_gate': ffn1_w_gate[l], 'ffn1_w_up': ffn1_w_up[l], 'ffn1_w_down': ffn1_w_down[l],
            'mix_norm': mix_norm[l], 'w_in': w_in[l],
            'mla_q_norm': mla_q_norm[l], 'mla_w_uq': mla_w_uq[l], 'mla_kv_norm': mla_kv_norm[l], 'mla_w_ukv': mla_w_ukv[l],
            'mla_q_gain': mla_q_gain[l], 'mla_k_gain': mla_k_gain[l],
            'dsa_q_gain': dsa_q_gain[l], 'dsa_k_gain': dsa_k_gain[l], 'w_out': w_out[l],
            'ffn2_norm': ffn2_norm[l], 'ffn2_w_gate': ffn2_w_gate[l], 'ffn2_w_up': ffn2_w_up[l], 'ffn2_w_down': ffn2_w_down[l],
        }

    t_p = x_prompt.shape[1]
    pos_p = jnp.arange(t_p, dtype=jnp.int32)
    h_p = x_prompt
    p_rows = [[], [], [], [], []]
    for l in range(DEPTH):
        h_p, rows = trunk_layer(h_p, pos_p, None, layer_params(l))
        for i in range(5):
            p_rows[i].append(rows[i])

    past_len = cache_mla_ckv.shape[2]
    t_s = x_sample.shape[1]
    pos_s = past_len + jnp.arange(t_s, dtype=jnp.int32)
    h_s = x_sample
    s_rows = [[], [], [], [], []]
    for l in range(DEPTH):
        past = (cache_mla_ckv[l], cache_mla_krope[l], cache_dsa_k[l], cache_dsa_v[l], cache_idx_k[l])
        h_s, rows = trunk_layer(h_s, pos_s, past, layer_params(l))
        for i in range(5):
            s_rows[i].append(rows[i])

    return (h_p, h_s,
            jnp.stack(p_rows[0]), jnp.stack(p_rows[1]), jnp.stack(p_rows[2]), jnp.stack(p_rows[3]), jnp.stack(p_rows[4]),
            jnp.stack(s_rows[0]), jnp.stack(s_rows[1]), jnp.stack(s_rows[2]), jnp.stack(s_rows[3]), jnp.stack(s_rows[4]))
```

```python
import functools

import jax
import jax.numpy as jnp
from jax import lax
from jax.experimental import pallas as pl
from jax.experimental.pallas import tpu as pltpu

F32 = jnp.float32
BF16 = jnp.bfloat16
I32 = jnp.int32

CHUNK_SHIFT = 6
ROPE_THETA = 500000.0
EPS = 1e-6
MLA_HEADS = 8
MLA_NOPE = 64
MLA_ROPE = 32
MLA_QK = MLA_NOPE + MLA_ROPE
MLA_V = 64
MLA_Q_LORA = 256
MLA_KV_LORA = 128
DSA_HEADS = 8
DSA_HD = 64
DSA_ROT = 16
IDX_HEADS = 4
IDX_HD = 64
IDX_W_SCALE = (IDX_HD * IDX_HEADS) ** -0.5
TOPK_MAX = 256
DSA_WIDTH = DSA_HEADS * DSA_HD
MLA_WIDTH = MLA_HEADS * MLA_V

LANE = 128
VMEM_LIMIT = 56 * 1024 * 1024

NEG = -1e30
INT_MIN = -(2 ** 31)
INT_MAX = 2 ** 31 - 1
KEY_NEG_INF = INT_MIN + 0x7FFFFF


def _nt_dot(a, b):
    return lax.dot_general(a, b, (((1,), (1,)), ((), ())), preferred_element_type=F32)


def _dot(a, b):
    return jnp.dot(a, b, preferred_element_type=F32)


def _rms(x, g):
    return x * lax.rsqrt(jnp.mean(x * x, axis=-1, keepdims=True) + EPS) * g


def _lane_iota(shape):
    return lax.broadcasted_iota(I32, shape, len(shape) - 1)


FFN_CHUNK = 256


def _ffn_body(x, g_ref, wg_ref, wu_ref, wd_ref, o_ref):
    xb = _rms(x, g_ref[...]).astype(BF16)
    d_ff = wg_ref.shape[1]
    acc = jnp.zeros(x.shape, F32)
    for c in range(d_ff // FFN_CHUNK):
        sl = slice(c * FFN_CHUNK, (c + 1) * FFN_CHUNK)
        gate = _dot(xb, wg_ref[:, sl])
        up = _dot(xb, wu_ref[:, sl])
        act = (gate * jax.nn.sigmoid(gate) * up).astype(BF16)
        acc = acc + _dot(act, wd_ref[sl, :])
    o_ref[...] = x + 0.5 * acc


def _ffn_kernel(x_ref, g_ref, wg_ref, wu_ref, wd_ref, o_ref):
    _ffn_body(x_ref[...], g_ref, wg_ref, wu_ref, wd_ref, o_ref)


def _out_ffn_kernel(h_ref, oa_ref, ob_ref, woa_ref, wob_ref, g_ref, wg_ref, wu_ref, wd_ref, o_ref):
    x = h_ref[...] + _dot(oa_ref[...], woa_ref[...]) + _dot(ob_ref[...], wob_ref[...])
    _ffn_body(x, g_ref, wg_ref, wu_ref, wd_ref, o_ref)


def _resident(shape):
    nd = len(shape)
    return pl.BlockSpec(shape, lambda *_: (0,) * nd, pipeline_mode=pl.Buffered(1))


def _row_tile(n, pref):
    for t in range(min(n, pref), 0, -16):
        if n % t == 0:
            return t
    raise ValueError(f"no row tile for {n} rows")


def _ffn(x, g, wg, wu, wd, attn=None):
    n, d = x.shape
    tm = _row_tile(n, 512)
    row = lambda w: pl.BlockSpec((tm, w), lambda i: (i, 0))
    w_specs = [_resident(g.shape), _resident(wg.shape), _resident(wu.shape), _resident(wd.shape)]
    params = pltpu.CompilerParams(dimension_semantics=("parallel",), vmem_limit_bytes=VMEM_LIMIT)
    out_shape = jax.ShapeDtypeStruct((n, d), F32)
    if attn is None:
        return pl.pallas_call(_ffn_kernel, grid=(n // tm,), in_specs=[row(d)] + w_specs, out_specs=row(d),
                              out_shape=out_shape, compiler_params=params, name="ffn")(x, g, wg, wu, wd)
    oa, ob, woa, wob = attn
    return pl.pallas_call(
        _out_ffn_kernel, grid=(n // tm,),
        in_specs=[row(d), row(oa.shape[1]), row(ob.shape[1]), _resident(woa.shape), _resident(wob.shape)] + w_specs,
        out_specs=row(d), out_shape=out_shape, compiler_params=params, name="out_ffn")(x, oa, ob, woa, wob, g, wg, wu, wd)


C_CQ = 0
C_CKV = C_CQ + MLA_Q_LORA
C_KR = C_CKV + MLA_KV_LORA
C_QB = C_KR + LANE
C_KB = C_QB + DSA_WIDTH
C_VB = C_KB + DSA_WIDTH
C_QI = C_VB + DSA_WIDTH
C_KI = C_QI + IDX_HEADS * IDX_HD
C_WI = C_KI + LANE
C_END = C_WI + LANE


def _rope_a(x, c, s):
    lane = _lane_iota(x.shape)
    partner = jnp.where(lane < MLA_NOPE + MLA_ROPE // 2, pltpu.roll(x, LANE - MLA_ROPE // 2, 1),
                        pltpu.roll(x, MLA_ROPE // 2, 1))
    return x * c + partner * s


def _rope_b(x, c, s):
    lane = _lane_iota(x.shape)
    half = DSA_ROT // 2
    partner = jnp.where((lane & (DSA_HD - 1)) < half, pltpu.roll(x, LANE - half, 1), pltpu.roll(x, half, 1))
    return x * c + partner * s


def _head96_norm(x, g):
    ms = jnp.sum(x * x, axis=-1, keepdims=True) * (1.0 / MLA_QK)
    return x * lax.rsqrt(ms + EPS) * g


def _head64_norm(x, g2):
    lane = _lane_iota(x.shape)
    lo = lane < DSA_HD
    sq = x * x
    s_lo = jnp.sum(jnp.where(lo, sq, 0.0), axis=-1, keepdims=True)
    s_hi = jnp.sum(jnp.where(lo, 0.0, sq), axis=-1, keepdims=True)
    ms = jnp.where(lo, s_lo, s_hi) * (1.0 / DSA_HD)
    return x * lax.rsqrt(ms + EPS) * g2


def _proj_kernel(h_ref, gmix_ref, win_ref, gq_ref, wuq_ref, gkv_ref, gqa_ref, gqb_ref, gkb_ref,
                 ca_ref, sa_ref, cb_ref, sb_ref,
                 ckv_ref, krope_ref, kslab_ref, kb_ref, vb_ref, ki_ref,
                 qa_ref, qb_ref, kb16_ref, vb16_ref, qi_ref, ik2_ref, wi_ref):
    u = _rms(h_ref[...], gmix_ref[...]).astype(BF16)
    ca, sa, cb, sb = ca_ref[...], sa_ref[...], cb_ref[...], sb_ref[...]

    def cols(start, width):
        return _dot(u, win_ref[:, start:start + width])

    cq = _rms(cols(C_CQ, MLA_Q_LORA), gq_ref[...]).astype(BF16)
    qa = _dot(cq, wuq_ref[...])
    qa_scale = MLA_QK ** -0.5
    for h in range(MLA_HEADS):
        sl = slice(h * LANE, (h + 1) * LANE)
        qa_ref[:, sl] = (_head96_norm(_rope_a(qa[:, sl], ca, sa), gqa_ref[...]) * qa_scale).astype(BF16)

    ckv_ref[...] = _rms(cols(C_CKV, MLA_KV_LORA), gkv_ref[...])
    kslab = _rope_a(cols(C_KR, LANE), ca, sa)
    kslab_ref[...] = kslab
    krope_ref[...] = kslab[:, MLA_NOPE:MLA_NOPE + MLA_ROPE]

    qb = cols(C_QB, DSA_WIDTH)
    kb = cols(C_KB, DSA_WIDTH)
    qb_scale = DSA_HD ** -0.5
    for p in range(DSA_WIDTH // LANE):
        sl = slice(p * LANE, (p + 1) * LANE)
        qb_ref[:, sl] = (_rope_b(_head64_norm(qb[:, sl], gqb_ref[...]), cb, sb) * qb_scale).astype(BF16)
        kp = _rope_b(_head64_norm(kb[:, sl], gkb_ref[...]), cb, sb)
        kb_ref[:, sl] = kp
        kb16_ref[:, sl] = kp.astype(BF16)
    vb = cols(C_VB, DSA_WIDTH)
    vb_ref[...] = vb
    vb16_ref[...] = vb.astype(BF16)

    qi = cols(C_QI, IDX_HEADS * IDX_HD)
    for p in range(IDX_HEADS * IDX_HD // LANE):
        sl = slice(p * LANE, (p + 1) * LANE)
        qi_ref[:, sl] = _rope_b(qi[:, sl], cb, sb).astype(BF16)
    ik2 = _rope_b(cols(C_KI, LANE), cb, sb)
    ki_ref[...] = ik2[:, :IDX_HD]
    ik2_ref[...] = ik2.astype(BF16)
    wi_ref[...] = cols(C_WI, LANE) * IDX_W_SCALE


def _proj(h, lw, tables, t_seq):
    n, d = h.shape
    tm = _row_tile(n, 512)
    ca, sa, cb, sb = tables
    n_tab = ca.shape[0] // tm
    row = lambda w: pl.BlockSpec((tm, w), lambda i: (i, 0))
    tab = pl.BlockSpec((tm, LANE), lambda i: (i % n_tab, 0))
    consts = [lw["mix_norm"], lw["w_in"], lw["mla_q_norm"], lw["w_uq"], lw["mla_kv_norm"],
              lw["mla_q_gain"], lw["dsa_q_gain"], lw["dsa_k_gain"]]
    out_widths = [(MLA_KV_LORA, F32), (MLA_ROPE, F32), (LANE, F32), (DSA_WIDTH, F32), (DSA_WIDTH, F32), (IDX_HD, F32),
                  (MLA_HEADS * LANE, BF16), (DSA_WIDTH, BF16), (DSA_WIDTH, BF16), (DSA_WIDTH, BF16),
                  (IDX_HEADS * IDX_HD, BF16), (LANE, BF16), (LANE, F32)]
    return pl.pallas_call(
        _proj_kernel, grid=(n // tm,),
        in_specs=[row(d)] + [_resident(c.shape) for c in consts] + [tab] * 4,
        out_specs=[row(w) for w, _ in out_widths],
        out_shape=[jax.ShapeDtypeStruct((n, w), dt) for w, dt in out_widths],
        compiler_params=pltpu.CompilerParams(dimension_semantics=("parallel",), vmem_limit_bytes=VMEM_LIMIT),
        name="proj")(h, *consts, ca, sa, cb, sb)


def _mla_kv_kernel(ckv_ref, kslab_ref, wn_ref, wv_ref, gk_ref, ka_ref, va_ref):
    c = ckv_ref[...].astype(BF16)
    kn = _dot(c, wn_ref[...])
    kslab = kslab_ref[...]
    for h in range(MLA_HEADS):
        sl = slice(h * LANE, (h + 1) * LANE)
        ka_ref[:, sl] = _head96_norm(kn[:, sl] + kslab, gk_ref[...]).astype(BF16)
    va_ref[...] = _dot(c, wv_ref[...]).astype(BF16)


def _mla_kv(ckv, kslab, lw):
    m = ckv.shape[0]
    tm = _row_tile(m, 512)
    row = lambda w: pl.BlockSpec((tm, w), lambda i: (i, 0))
    consts = [lw["w_ukv_nope"], lw["w_ukv_v"], lw["mla_k_gain"]]
    return pl.pallas_call(
        _mla_kv_kernel, grid=(m // tm,),
        in_specs=[row(MLA_KV_LORA), row(LANE)] + [_resident(c.shape) for c in consts],
        out_specs=[row(MLA_HEADS * LANE), row(MLA_WIDTH)],
        out_shape=[jax.ShapeDtypeStruct((m, MLA_HEADS * LANE), BF16), jax.ShapeDtypeStruct((m, MLA_WIDTH), BF16)],
        compiler_params=pltpu.CompilerParams(dimension_semantics=("parallel",), vmem_limit_bytes=VMEM_LIMIT),
        name="mla_kv")(ckv, kslab, *consts)


def _visible(q0, k0, tq, tk, s_real):
    q_chunk = (q0 + lax.broadcasted_iota(I32, (tq, tk), 0)) >> CHUNK_SHIFT
    k_idx = k0 + lax.broadcasted_iota(I32, (tq, tk), 1)
    return ((k_idx >> CHUNK_SHIFT) <= q_chunk) & (k_idx < s_real)


def _softmax_step(h, s, v_pair, m_ref, l_ref, acc_ref):
    m_old = m_ref[h]
    m_new = jnp.maximum(m_old, jnp.max(s, axis=-1, keepdims=True))
    alpha = jnp.exp(m_old - m_new)
    p = jnp.exp(s - m_new[:, :1])
    l_ref[h] = alpha * l_ref[h] + jnp.sum(p, axis=-1, keepdims=True)
    acc_ref[h] = alpha * acc_ref[h] + _dot(p.astype(BF16), v_pair)
    m_ref[h] = m_new


def _init_softmax(m_ref, l_ref, acc_ref):
    m_ref[...] = jnp.full(m_ref.shape, NEG, F32)
    l_ref[...] = jnp.zeros(l_ref.shape, F32)
    acc_ref[...] = jnp.zeros(acc_ref.shape, F32)


def _write_pairs(o_ref, l_ref, acc_ref, n_heads):
    lane = _lane_iota(acc_ref.shape[1:])
    for p in range(n_heads // 2):
        even = acc_ref[2 * p] / l_ref[2 * p]
        odd = acc_ref[2 * p + 1] / l_ref[2 * p + 1]
        o_ref[0, :, p * LANE:(p + 1) * LANE] = jnp.where(lane < LANE // 2, even, odd).astype(o_ref.dtype)


def _mla_attn_kernel(q_ref, k_ref, v_ref, o_ref, m_ref, l_ref, acc_ref, *, tq, tk, q_off, s_real):
    qi, ki = pl.program_id(1), pl.program_id(2)
    q0 = q_off + qi * tq
    k0 = ki * tk

    @pl.when(ki == 0)
    def _():
        _init_softmax(m_ref, l_ref, acc_ref)

    @pl.when((k0 >> CHUNK_SHIFT) <= ((q0 + tq - 1) >> CHUNK_SHIFT))
    def _():
        vis = _visible(q0, k0, tq, tk, s_real)
        for h in range(MLA_HEADS):
            sl = slice(h * LANE, (h + 1) * LANE)
            s = jnp.where(vis, _nt_dot(q_ref[0, :, sl], k_ref[0, :, sl]), NEG)
            _softmax_step(h, s, v_ref[0, :, (h // 2) * LANE:(h // 2 + 1) * LANE], m_ref, l_ref, acc_ref)

    @pl.when(ki == pl.num_programs(2) - 1)
    def _():
        _write_pairs(o_ref, l_ref, acc_ref, MLA_HEADS)


def _mla_attn(qa, ka, va, *, q_off, s_real, tq, tk):
    b, t, _ = qa.shape
    s_pad = ka.shape[1]
    nq, nk = t // tq, s_pad // tk

    def kv_map(bi, qi, ki):
        last = ((((q_off + qi * tq + tq - 1) >> CHUNK_SHIFT) + 1) << CHUNK_SHIFT) - 1
        return (bi, jnp.minimum(ki, jnp.minimum(last // tk, nk - 1)), 0)

    kern = functools.partial(_mla_attn_kernel, tq=tq, tk=tk, q_off=q_off, s_real=s_real)
    return pl.pallas_call(
        kern, grid=(b, nq, nk),
        in_specs=[pl.BlockSpec((1, tq, MLA_HEADS * LANE), lambda bi, qi, ki: (bi, qi, 0)),
                  pl.BlockSpec((1, tk, MLA_HEADS * LANE), kv_map),
                  pl.BlockSpec((1, tk, MLA_WIDTH), kv_map)],
        out_specs=pl.BlockSpec((1, tq, MLA_WIDTH), lambda bi, qi, ki: (bi, qi, 0)),
        out_shape=jax.ShapeDtypeStruct((b, t, MLA_WIDTH), BF16),
        scratch_shapes=[pltpu.VMEM((MLA_HEADS, tq, LANE), F32), pltpu.VMEM((MLA_HEADS, tq, LANE), F32),
                        pltpu.VMEM((MLA_HEADS, tq, LANE), F32)],
        compiler_params=pltpu.CompilerParams(dimension_semantics=("parallel", "parallel", "arbitrary"),
                                             vmem_limit_bytes=VMEM_LIMIT),
        name="mla_attn")(qa, ka, va)


def _fold_lanes(c):
    part = c[:, :LANE]
    for j in range(1, c.shape[1] // LANE):
        part = part + c[:, j * LANE:(j + 1) * LANE]
    return part


def _dsa_kernel(qb_ref, qi_ref, wi_ref, kb_ref, vb_ref, ik2_ref, o_ref,
                key_ref, qm_ref, qim_ref, j_ref, m_ref, l_ref, acc_ref, *, tq, tk, q_off, s_real, topk):
    q0 = q_off + pl.program_id(1) * tq
    vis_end = jnp.minimum((((q0 + tq - 1) >> CHUNK_SHIFT) + 1) << CHUNK_SHIFT, s_real)
    n_vis = (vis_end + tk - 1) // tk
    lane = _lane_iota((tq, LANE))
    lo = lane < LANE // 2

    for h in range(DSA_HEADS):
        pair = qb_ref[0, :, (h // 2) * LANE:(h // 2 + 1) * LANE]
        qm_ref[h] = jnp.where(lo if h % 2 == 0 else ~lo, pair, jnp.zeros_like(pair))
    for h in range(IDX_HEADS):
        pair = qi_ref[0, :, (h // 2) * LANE:(h // 2 + 1) * LANE]
        qim_ref[h] = jnp.where(lo if h % 2 == 0 else ~lo, pair, jnp.zeros_like(pair))
    w_cols = [wi_ref[0, :, h:h + 1] for h in range(IDX_HEADS)]

    def score_block(kb, carry):
        k0 = pl.multiple_of(kb * tk, tk)
        ik = ik2_ref[0, pl.ds(k0, tk), :]
        score = jnp.zeros((tq, tk), F32)
        for h in range(IDX_HEADS):
            score = score + jnp.maximum(_nt_dot(qim_ref[h], ik), 0.0) * w_cols[h]
        score = jnp.where(score == 0.0, 0.0, score)
        score = jnp.where(_visible(q0, k0, tq, tk, s_real), score, -jnp.inf)
        bits = pltpu.bitcast(score, I32)
        key_ref[kb] = bits ^ ((bits >> 31) & INT_MAX)
        return carry

    lax.fori_loop(0, n_vis, score_block, 0)

    def count(pred):
        def body(kb, acc):
            return acc + _fold_lanes(jnp.where(pred(key_ref[kb], kb), 1.0, 0.0))
        return jnp.sum(lax.fori_loop(0, n_vis, body, jnp.zeros((tq, LANE), F32)), axis=-1, keepdims=True)

    def count_ge(t):
        return count(lambda blk, kb: blk >= t)

    kf = float(topk)
    t = jnp.where(count_ge(jnp.zeros((tq, 1), I32)) >= kf, 0, INT_MIN).astype(I32)

    def value_bit(i, t):
        t_try = t | (jnp.int32(1) << (30 - i))
        return jnp.where(count_ge(t_try) >= kf, t_try, t)

    t = lax.fori_loop(0, 31, value_bit, t)

    j_ref[...] = jnp.full(j_ref.shape, INT_MAX, I32)
    n_ge = count_ge(t)

    @pl.when(jnp.max(n_ge) > kf)
    def _():
        need = kf - count(lambda blk, kb: blk > t)

        def count_eq_before(j):
            def pred(blk, kb):
                idx = kb * tk + lax.broadcasted_iota(I32, (tq, tk), 1)
                return (blk == t) & (idx < j)
            return count(pred)

        def index_bit(i, j):
            j_try = j | (jnp.int32(1) << (30 - i))
            return jnp.where(count_eq_before(j_try) < need, j_try, j)

        j_sel = lax.fori_loop(0, 31, index_bit, jnp.zeros((tq, 1), I32))
        j_ref[...] = jnp.broadcast_to(j_sel, j_ref.shape)

    j_sel = j_ref[:, :1]

    _init_softmax(m_ref, l_ref, acc_ref)

    def attend_block(kb, carry):
        k0 = pl.multiple_of(kb * tk, tk)
        blk = key_ref[kb]
        idx = k0 + lax.broadcasted_iota(I32, (tq, tk), 1)
        sel = ((blk > t) | ((blk == t) & (idx <= j_sel))) & (blk > KEY_NEG_INF)
        for h in range(DSA_HEADS):
            sl = slice((h // 2) * LANE, (h // 2 + 1) * LANE)
            s = jnp.where(sel, _nt_dot(qm_ref[h], kb_ref[0, pl.ds(k0, tk), sl]), NEG)
            _softmax_step(h, s, vb_ref[0, pl.ds(k0, tk), sl], m_ref, l_ref, acc_ref)
        return carry

    lax.fori_loop(0, n_vis, attend_block, 0)
    _write_pairs(o_ref, l_ref, acc_ref, DSA_HEADS)


def _dsa_attn(qb, qi, wi, kb, vb, ik2, *, q_off, s_real, tq, tk):
    b, t, _ = qb.shape
    s_pad = kb.shape[1]
    topk = min(TOPK_MAX, s_real // 4)
    qspec = lambda w: pl.BlockSpec((1, tq, w), lambda bi, i: (bi, i, 0))
    kspec = lambda w: pl.BlockSpec((1, s_pad, w), lambda bi, i: (bi, 0, 0), pipeline_mode=pl.Buffered(1))
    kern = functools.partial(_dsa_kernel, tq=tq, tk=tk, q_off=q_off, s_real=s_real, topk=topk)
    return pl.pallas_call(
        kern, grid=(b, t // tq),
        in_specs=[qspec(DSA_WIDTH), qspec(IDX_HEADS * IDX_HD), qspec(LANE),
                  kspec(DSA_WIDTH), kspec(DSA_WIDTH), kspec(LANE)],
        out_specs=qspec(DSA_WIDTH),
        out_shape=jax.ShapeDtypeStruct((b, t, DSA_WIDTH), BF16),
        scratch_shapes=[pltpu.VMEM((s_pad // tk, tq, tk), I32),
                        pltpu.VMEM((DSA_HEADS, tq, LANE), BF16), pltpu.VMEM((IDX_HEADS, tq, LANE), BF16),
                        pltpu.VMEM((tq, LANE), I32),
                        pltpu.VMEM((DSA_HEADS, tq, LANE), F32), pltpu.VMEM((DSA_HEADS, tq, LANE), F32),
                        pltpu.VMEM((DSA_HEADS, tq, LANE), F32)],
        compiler_params=pltpu.CompilerParams(dimension_semantics=("parallel", "arbitrary"),
                                             vmem_limit_bytes=VMEM_LIMIT),
        name="dsa_attn")(qb, qi, wi, kb, vb, ik2)


def _pad_cols(w, width):
    return jnp.pad(w, ((0, 0), (0, width - w.shape[1])))


def _layer_weights(p, l):
    w_in = p["w_in"][l]
    off, pieces = 0, []
    for n in (MLA_Q_LORA, MLA_KV_LORA, MLA_ROPE, DSA_WIDTH, DSA_WIDTH, DSA_WIDTH, IDX_HEADS * IDX_HD, IDX_HD, IDX_HEADS):
        pieces.append(w_in[:, off:off + n])
        off += n
    c_q, c_kv, k_r, q_b, k_b, v_b, q_i, k_i, w_i = pieces
    k_r = jnp.pad(k_r, ((0, 0), (MLA_NOPE, LANE - MLA_QK)))
    w_in_p = jnp.concatenate([c_q, c_kv, k_r, q_b, k_b, v_b, q_i, k_i, k_i, _pad_cols(w_i, LANE)], axis=1)
    assert w_in_p.shape[1] == C_END

    d_lora = p["mla_w_uq"].shape[1]
    w_uq = p["mla_w_uq"][l].reshape(d_lora, MLA_HEADS, MLA_QK)
    w_uq = jnp.pad(w_uq, ((0, 0), (0, 0), (0, LANE - MLA_QK))).reshape(d_lora, MLA_HEADS * LANE)
    w_ukv = p["mla_w_ukv"][l].reshape(MLA_KV_LORA, MLA_HEADS, MLA_NOPE + MLA_V)
    w_nope = jnp.pad(w_ukv[:, :, :MLA_NOPE], ((0, 0), (0, 0), (0, LANE - MLA_NOPE))).reshape(MLA_KV_LORA, MLA_HEADS * LANE)
    w_v = w_ukv[:, :, MLA_NOPE:].reshape(MLA_KV_LORA, MLA_WIDTH)
    w_out = p["w_out"][l]

    row = lambda g: g[l][None, :].astype(F32)
    pad96 = lambda g: jnp.pad(g[l].astype(F32), (0, LANE - MLA_QK))[None, :]
    twice = lambda g: jnp.tile(g[l].astype(F32), 2)[None, :]
    lw = {
        "w_in": w_in_p.astype(BF16), "w_uq": w_uq.astype(BF16), "w_ukv_nope": w_nope.astype(BF16),
        "w_ukv_v": w_v.astype(BF16), "w_out_a": w_out[:MLA_WIDTH].astype(BF16), "w_out_b": w_out[MLA_WIDTH:].astype(BF16),
        "mix_norm": row(p["mix_norm"]), "mla_q_norm": row(p["mla_q_norm"]), "mla_kv_norm": row(p["mla_kv_norm"]),
        "mla_q_gain": pad96(p["mla_q_gain"]), "mla_k_gain": pad96(p["mla_k_gain"]),
        "dsa_q_gain": twice(p["dsa_q_gain"]), "dsa_k_gain": twice(p["dsa_k_gain"]),
    }
    for f in ("ffn1", "ffn2"):
        lw[f + "_norm"] = row(p[f + "_norm"])
        for w in ("w_gate", "w_up", "w_down"):
            lw[f + "_" + w] = p[f + "_" + w][l].astype(BF16)
    return lw


def _rope_tables(pos, rows):
    def cs(rot):
        inv = 1.0 / (ROPE_THETA ** (jnp.arange(0, rot, 2, dtype=F32) / rot))
        ang = pos.astype(F32)[:, None] * inv[None, :]
        return jnp.cos(ang), jnp.sin(ang)

    t = pos.shape[0]
    cos_a, sin_a = cs(MLA_ROPE)
    ones = lambda w: jnp.ones((t, w), F32)
    zeros = lambda w: jnp.zeros((t, w), F32)
    ca = jnp.concatenate([ones(MLA_NOPE), cos_a, cos_a, ones(LANE - MLA_QK)], axis=1)
    sa = jnp.concatenate([zeros(MLA_NOPE), -sin_a, sin_a, zeros(LANE - MLA_QK)], axis=1)
    cos_b, sin_b = cs(DSA_ROT)
    cb = jnp.tile(jnp.concatenate([cos_b, cos_b, ones(DSA_HD - DSA_ROT)], axis=1), (1, 2))
    sb = jnp.tile(jnp.concatenate([-sin_b, sin_b, zeros(DSA_HD - DSA_ROT)], axis=1), (1, 2))
    reps = max(1, rows // t)
    return tuple(jnp.tile(x, (reps, 1)) for x in (ca, sa, cb, sb))


def _pad_keys(x, s_pad):
    return jnp.pad(x, ((0, 0), (0, s_pad - x.shape[1]), (0, 0)))


def _trunk_layer(x, lw, tables, past, *, b, t, q_off, tq_mla, tk_mla, tq_dsa, tk_dsa):
    h = _ffn(x, lw["ffn1_norm"], lw["ffn1_w_gate"], lw["ffn1_w_up"], lw["ffn1_w_down"])
    (ckv, krope, kslab, kb, vb, ki, qa, qb, kb16, vb16, qi, ik2, wi) = _proj(h, lw, tables, t)
    per_batch = lambda a: a.reshape(b, t, a.shape[-1])
    if past is None:
        ckv_all, kslab_all = ckv, kslab
        kb_all, vb_all, ik2_all = per_batch(kb16), per_batch(vb16), per_batch(ik2)
        s_real = t
    else:
        p_ckv, p_krope, p_kb, p_vb, p_ki = past
        s_real = p_ckv.shape[1] + t
        cat = lambda old, new: jnp.concatenate([old, per_batch(new)], axis=1)
        ckv_all = cat(p_ckv, ckv).reshape(b * s_real, MLA_KV_LORA)
        p_kslab = jnp.pad(p_krope, ((0, 0), (0, 0), (MLA_NOPE, LANE - MLA_QK)))
        kslab_all = cat(p_kslab, kslab).reshape(b * s_real, LANE)
        kb_all = cat(p_kb.reshape(b, -1, DSA_WIDTH).astype(BF16), kb16)
        vb_all = cat(p_vb.reshape(b, -1, DSA_WIDTH).astype(BF16), vb16)
        ik2_all = cat(jnp.tile(p_ki, (1, 1, 2)).astype(BF16), ik2)
    ka, va = _mla_kv(ckv_all, kslab_all, lw)
    s_pad_a = pl.cdiv(s_real, tk_mla) * tk_mla
    ka = _pad_keys(ka.reshape(b, s_real, -1), s_pad_a)
    va = _pad_keys(va.reshape(b, s_real, -1), s_pad_a)
    oa = _mla_attn(per_batch(qa), ka, va, q_off=q_off, s_real=s_real, tq=tq_mla, tk=tk_mla)
    s_pad_b = pl.cdiv(s_real, tk_dsa) * tk_dsa
    ob = _dsa_attn(per_batch(qb), per_batch(qi), per_batch(wi), _pad_keys(kb_all, s_pad_b), _pad_keys(vb_all, s_pad_b),
                   _pad_keys(ik2_all, s_pad_b), q_off=q_off, s_real=s_real, tq=tq_dsa, tk=tk_dsa)
    y = _ffn(h, lw["ffn2_norm"], lw["ffn2_w_gate"], lw["ffn2_w_up"], lw["ffn2_w_down"],
             attn=(oa.reshape(b * t, MLA_WIDTH), ob.reshape(b * t, DSA_WIDTH), lw["w_out_a"], lw["w_out_b"]))
    rows = (per_batch(ckv), per_batch(krope), per_batch(kb).reshape(b, t, DSA_HEADS, DSA_HD),
            per_batch(vb).reshape(b, t, DSA_HEADS, DSA_HD), per_batch(ki))
    return y, rows


def kernel(x_prompt, x_sample, cache_mla_ckv, cache_mla_krope, cache_dsa_k, cache_dsa_v, cache_idx_k,
           ffn1_norm, ffn1_w_gate, ffn1_w_up, ffn1_w_down, mix_norm, w_in,
           mla_q_norm, mla_w_uq, mla_kv_norm, mla_w_ukv, mla_q_gain, mla_k_gain,
           dsa_q_gain, dsa_k_gain, w_out, ffn2_norm, ffn2_w_gate, ffn2_w_up, ffn2_w_down):
    params = dict(ffn1_norm=ffn1_norm, ffn1_w_gate=ffn1_w_gate, ffn1_w_up=ffn1_w_up, ffn1_w_down=ffn1_w_down,
                  mix_norm=mix_norm, w_in=w_in, mla_q_norm=mla_q_norm, mla_w_uq=mla_w_uq, mla_kv_norm=mla_kv_norm,
                  mla_w_ukv=mla_w_ukv, mla_q_gain=mla_q_gain, mla_k_gain=mla_k_gain, dsa_q_gain=dsa_q_gain,
                  dsa_k_gain=dsa_k_gain, w_out=w_out, ffn2_norm=ffn2_norm, ffn2_w_gate=ffn2_w_gate,
                  ffn2_w_up=ffn2_w_up, ffn2_w_down=ffn2_w_down)
    depth = w_in.shape[0]
    d_model = x_prompt.shape[-1]
    weights = [_layer_weights(params, l) for l in range(depth)]

    b_p, t_p = x_prompt.shape[:2]
    n_p = b_p * t_p
    tabs_p = _rope_tables(jnp.arange(t_p, dtype=I32), _row_tile(n_p, 512))
    tile_p = dict(tq_mla=min(t_p, 512), tk_mla=min(t_p, 512), tq_dsa=min(t_p, 256), tk_dsa=min(t_p, 512))
    h_p = x_prompt.reshape(n_p, d_model)
    p_rows = []
    for l in range(depth):
        h_p, rows = _trunk_layer(h_p, weights[l], tabs_p, None, b=b_p, t=t_p, q_off=0, **tile_p)
        p_rows.append(rows)

    b_s, t_s = x_sample.shape[:2]
    n_s = b_s * t_s
    past_len = cache_mla_ckv.shape[2]
    tabs_s = _rope_tables(past_len + jnp.arange(t_s, dtype=I32), _row_tile(n_s, 512))
    tile_s = dict(tq_mla=t_s, tk_mla=LANE, tq_dsa=t_s, tk_dsa=2 * LANE)
    h_s = x_sample.reshape(n_s, d_model)
    s_rows = []
    for l in range(depth):
        past = (cache_mla_ckv[l], cache_mla_krope[l], cache_dsa_k[l], cache_dsa_v[l], cache_idx_k[l])
        h_s, rows = _trunk_layer(h_s, weights[l], tabs_s, past, b=b_s, t=t_s, q_off=past_len, **tile_s)
        s_rows.append(rows)

    stack = lambda rows_by_layer, i: jnp.stack([r[i] for r in rows_by_layer])
    return (h_p.reshape(b_p, t_p, d_model), h_s.reshape(b_s, t_s, d_model),
            *[stack(p_rows, i) for i in range(5)], *[stack(s_rows, i) for i in range(5)])
```

```python
import functools

import jax
import jax.numpy as jnp
from jax import lax
from jax.experimental import pallas as pl
from jax.experimental.pallas import tpu as pltpu

F32 = jnp.float32
BF16 = jnp.bfloat16
I32 = jnp.int32

CHUNK_SHIFT = 6
ROPE_THETA = 500000.0
EPS = 1e-6
MLA_HEADS = 8
MLA_NOPE = 64
MLA_ROPE = 32
MLA_QK = MLA_NOPE + MLA_ROPE
MLA_V = 64
MLA_Q_LORA = 256
MLA_KV_LORA = 128
DSA_HEADS = 8
DSA_HD = 64
DSA_ROT = 16
IDX_HEADS = 4
IDX_HD = 64
IDX_W_SCALE = (IDX_HD * IDX_HEADS) ** -0.5
TOPK_MAX = 256
DSA_WIDTH = DSA_HEADS * DSA_HD
MLA_WIDTH = MLA_HEADS * MLA_V

LANE = 128
VMEM_LIMIT = 56 * 1024 * 1024

NEG = -1e30
INT_MIN = -(2 ** 31)
INT_MAX = 2 ** 31 - 1
KEY_NEG_INF = INT_MIN + 0x7FFFFF


def _nt_dot(a, b):
    return lax.dot_general(a, b, (((1,), (1,)), ((), ())), preferred_element_type=F32)


def _dot(a, b):
    return jnp.dot(a, b, preferred_element_type=F32)


def _rms(x, g):
    return x * lax.rsqrt(jnp.mean(x * x, axis=-1, keepdims=True) + EPS) * g


def _lane_iota(shape):
    return lax.broadcasted_iota(I32, shape, len(shape) - 1)


FFN_CHUNK = 256


def _ffn_body(x, g_ref, wg_ref, wu_ref, wd_ref, o_ref):
    xb = _rms(x, g_ref[...]).astype(BF16)
    d_ff = wg_ref.shape[1]
    acc = jnp.zeros(x.shape, F32)
    for c in range(d_ff // FFN_CHUNK):
        sl = slice(c * FFN_CHUNK, (c + 1) * FFN_CHUNK)
        gate = _dot(xb, wg_ref[:, sl])
        up = _dot(xb, wu_ref[:, sl])
        act = (gate * jax.nn.sigmoid(gate) * up).astype(BF16)
        acc = acc + _dot(act, wd_ref[sl, :])
    o_ref[...] = x + 0.5 * acc


def _ffn_kernel(x_ref, g_ref, wg_ref, wu_ref, wd_ref, o_ref):
    _ffn_body(x_ref[...], g_ref, wg_ref, wu_ref, wd_ref, o_ref)


def _out_ffn_kernel(h_ref, oa_ref, ob_ref, woa_ref, wob_ref, g_ref, wg_ref, wu_ref, wd_ref, o_ref):
    x = h_ref[...] + _dot(oa_ref[...], woa_ref[...]) + _dot(ob_ref[...], wob_ref[...])
    _ffn_body(x, g_ref, wg_ref, wu_ref, wd_ref, o_ref)


def _resident(shape):
    nd = len(shape)
    return pl.BlockSpec(shape, lambda *_: (0,) * nd, pipeline_mode=pl.Buffered(1))


def _row_tile(n, pref):
    for t in range(min(n, pref), 0, -16):
        if n % t == 0:
            return t
    raise ValueError(f"no row tile for {n} rows")


def _ffn(x, g, wg, wu, wd, attn=None):
    n, d = x.shape
    tm = _row_tile(n, 512)
    row = lambda w: pl.BlockSpec((tm, w), lambda i: (i, 0))
    w_specs = [_resident(g.shape), _resident(wg.shape), _resident(wu.shape), _resident(wd.shape)]
    params = pltpu.CompilerParams(dimension_semantics=("parallel",), vmem_limit_bytes=VMEM_LIMIT)
    out_shape = jax.ShapeDtypeStruct((n, d), F32)
    if attn is None:
        return pl.pallas_call(_ffn_kernel, grid=(n // tm,), in_specs=[row(d)] + w_specs, out_specs=row(d),
                              out_shape=out_shape, compiler_params=params, name="ffn")(x, g, wg, wu, wd)
    oa, ob, woa, wob = attn
    return pl.pallas_call(
        _out_ffn_kernel, grid=(n // tm,),
        in_specs=[row(d), row(oa.shape[1]), row(ob.shape[1]), _resident(woa.shape), _resident(wob.shape)] + w_specs,
        out_specs=row(d), out_shape=out_shape, compiler_params=params, name="out_ffn")(x, oa, ob, woa, wob, g, wg, wu, wd)


C_CQ = 0
C_CKV = C_CQ + MLA_Q_LORA
C_KR = C_CKV + MLA_KV_LORA
C_QB = C_KR + LANE
C_KB = C_QB + DSA_WIDTH
C_VB = C_KB + DSA_WIDTH
C_QI = C_VB + DSA_WIDTH
C_KI = C_QI + IDX_HEADS * IDX_HD
C_WI = C_KI + LANE
C_END = C_WI + LANE


def _rope_a(x, c, s):
    lane = _lane_iota(x.shape)
    partner = jnp.where(lane < MLA_NOPE + MLA_ROPE // 2, pltpu.roll(x, LANE - MLA_ROPE // 2, 1),
                        pltpu.roll(x, MLA_ROPE // 2, 1))
    return x * c + partner * s


def _rope_b(x, c, s):
    lane = _lane_iota(x.shape)
    half = DSA_ROT // 2
    partner = jnp.where((lane & (DSA_HD - 1)) < half, pltpu.roll(x, LANE - half, 1), pltpu.roll(x, half, 1))
    return x * c + partner * s


def _head96_norm(x, g):
    ms = jnp.sum(x * x, axis=-1, keepdims=True) * (1.0 / MLA_QK)
    return x * lax.rsqrt(ms + EPS) * g


def _head64_norm(x, g2):
    lane = _lane_iota(x.shape)
    lo = lane < DSA_HD
    sq = x * x
    s_lo = jnp.sum(jnp.where(lo, sq, 0.0), axis=-1, keepdims=True)
    s_hi = jnp.sum(jnp.where(lo, 0.0, sq), axis=-1, keepdims=True)
    ms = jnp.where(lo, s_lo, s_hi) * (1.0 / DSA_HD)
    return x * lax.rsqrt(ms + EPS) * g2


def _proj_kernel(h_ref, gmix_ref, win_ref, gq_ref, wuq_ref, gkv_ref, gqa_ref, gqb_ref, gkb_ref,
                 ca_ref, sa_ref, cb_ref, sb_ref,
                 ckv_ref, krope_ref, kslab_ref, kb_ref, vb_ref, ki_ref,
                 qa_ref, qb_ref, kb16_ref, vb16_ref, qi_ref, ik2_ref, wi_ref):
    u = _rms(h_ref[...], gmix_ref[...]).astype(BF16)
    ca, sa, cb, sb = ca_ref[...], sa_ref[...], cb_ref[...], sb_ref[...]

    def cols(start, width):
        return _dot(u, win_ref[:, start:start + width])

    cq = _rms(cols(C_CQ, MLA_Q_LORA), gq_ref[...]).astype(BF16)
    qa = _dot(cq, wuq_ref[...])
    qa_scale = MLA_QK ** -0.5
    for h in range(MLA_HEADS):
        sl = slice(h * LANE, (h + 1) * LANE)
        qa_ref[:, sl] = (_head96_norm(_rope_a(qa[:, sl], ca, sa), gqa_ref[...]) * qa_scale).astype(BF16)

    ckv_ref[...] = _rms(cols(C_CKV, MLA_KV_LORA), gkv_ref[...])
    kslab = _rope_a(cols(C_KR, LANE), ca, sa)
    kslab_ref[...] = kslab
    krope_ref[...] = kslab[:, MLA_NOPE:MLA_NOPE + MLA_ROPE]

    qb = cols(C_QB, DSA_WIDTH)
    kb = cols(C_KB, DSA_WIDTH)
    qb_scale = DSA_HD ** -0.5
    for p in range(DSA_WIDTH // LANE):
        sl = slice(p * LANE, (p + 1) * LANE)
        qb_ref[:, sl] = (_rope_b(_head64_norm(qb[:, sl], gqb_ref[...]), cb, sb) * qb_scale).astype(BF16)
        kp = _rope_b(_head64_norm(kb[:, sl], gkb_ref[...]), cb, sb)
        kb_ref[:, sl] = kp
        kb16_ref[:, sl] = kp.astype(BF16)
    vb = cols(C_VB, DSA_WIDTH)
    vb_ref[...] = vb
    vb16_ref[...] = vb.astype(BF16)

    qi = cols(C_QI, IDX_HEADS * IDX_HD)
    for p in range(IDX_HEADS * IDX_HD // LANE):
        sl = slice(p * LANE, (p + 1) * LANE)
        qi_ref[:, sl] = _rope_b(qi[:, sl], cb, sb).astype(BF16)
    ik2 = _rope_b(cols(C_KI, LANE), cb, sb)
    ki_ref[...] = ik2[:, :IDX_HD]
    ik2_ref[...] = ik2.astype(BF16)
    wi_ref[...] = cols(C_WI, LANE) * IDX_W_SCALE


def _proj(h, lw, tables, t_seq):
    n, d = h.shape
    tm = _row_tile(n, 512)
    ca, sa, cb, sb = tables
    n_tab = ca.shape[0] // tm
    row = lambda w: pl.BlockSpec((tm, w), lambda i: (i, 0))
    tab = pl.BlockSpec((tm, LANE), lambda i: (i % n_tab, 0))
    consts = [lw["mix_norm"], lw["w_in"], lw["mla_q_norm"], lw["w_uq"], lw["mla_kv_norm"],
              lw["mla_q_gain"], lw["dsa_q_gain"], lw["dsa_k_gain"]]
    out_widths = [(MLA_KV_LORA, F32), (MLA_ROPE, F32), (LANE, F32), (DSA_WIDTH, F32), (DSA_WIDTH, F32), (IDX_HD, F32),
                  (MLA_HEADS * LANE, BF16), (DSA_WIDTH, BF16), (DSA_WIDTH, BF16), (DSA_WIDTH, BF16),
                  (IDX_HEADS * IDX_HD, BF16), (LANE, BF16), (LANE, F32)]
    return pl.pallas_call(
        _proj_kernel, grid=(n // tm,),
        in_specs=[row(d)] + [_resident(c.shape) for c in consts] + [tab] * 4,
        out_specs=[row(w) for w, _ in out_widths],
        out_shape=[jax.ShapeDtypeStruct((n, w), dt) for w, dt in out_widths],
        compiler_params=pltpu.CompilerParams(dimension_semantics=("parallel",), vmem_limit_bytes=VMEM_LIMIT),
        name="proj")(h, *consts, ca, sa, cb, sb)


def _mla_kv_kernel(ckv_ref, kslab_ref, wn_ref, wv_ref, gk_ref, ka_ref, va_ref):
    c = ckv_ref[...].astype(BF16)
    kn = _dot(c, wn_ref[...])
    kslab = kslab_ref[...]
    for h in range(MLA_HEADS):
        sl = slice(h * LANE, (h + 1) * LANE)
        ka_ref[:, sl] = _head96_norm(kn[:, sl] + kslab, gk_ref[...]).astype(BF16)
    va_ref[...] = _dot(c, wv_ref[...]).astype(BF16)


def _mla_kv(ckv, kslab, lw):
    m = ckv.shape[0]
    tm = _row_tile(m, 512)
    row = lambda w: pl.BlockSpec((tm, w), lambda i: (i, 0))
    consts = [lw["w_ukv_nope"], lw["w_ukv_v"], lw["mla_k_gain"]]
    return pl.pallas_call(
        _mla_kv_kernel, grid=(m // tm,),
        in_specs=[row(MLA_KV_LORA), row(LANE)] + [_resident(c.shape) for c in consts],
        out_specs=[row(MLA_HEADS * LANE), row(MLA_WIDTH)],
        out_shape=[jax.ShapeDtypeStruct((m, MLA_HEADS * LANE), BF16), jax.ShapeDtypeStruct((m, MLA_WIDTH), BF16)],
        compiler_params=pltpu.CompilerParams(dimension_semantics=("parallel",), vmem_limit_bytes=VMEM_LIMIT),
        name="mla_kv")(ckv, kslab, *consts)


def _visible(q0, k0, tq, tk, s_real):
    q_chunk = (q0 + lax.broadcasted_iota(I32, (tq, tk), 0)) >> CHUNK_SHIFT
    k_idx = k0 + lax.broadcasted_iota(I32, (tq, tk), 1)
    return ((k_idx >> CHUNK_SHIFT) <= q_chunk) & (k_idx < s_real)


def _softmax_update(s, v_pair, m, l, acc):
    m_new = jnp.maximum(m, jnp.max(s, axis=-1, keepdims=True))
    alpha = jnp.exp(m - m_new)
    p = jnp.exp(s - m_new)
    l_new = alpha * l + jnp.sum(p, axis=-1, keepdims=True)
    return m_new, l_new, alpha * acc + _dot(p.astype(BF16), v_pair)


def _write_pairs(o_ref, ls, accs):
    lane = _lane_iota(accs[0].shape)
    for p in range(len(accs) // 2):
        even = accs[2 * p] / ls[2 * p]
        odd = accs[2 * p + 1] / ls[2 * p + 1]
        o_ref[0, :, p * LANE:(p + 1) * LANE] = jnp.where(lane < LANE // 2, even, odd).astype(o_ref.dtype)


def _mla_attn_kernel(q_ref, k_ref, v_ref, o_ref, m_ref, l_ref, acc_ref, *, tq, tk, q_off, s_real):
    qi, ki = pl.program_id(1), pl.program_id(2)
    q0 = q_off + qi * tq
    k0 = ki * tk

    heads = range(MLA_HEADS)

    @pl.when(ki == 0)
    def _():
        m_ref[...] = jnp.full(m_ref.shape, NEG, F32)
        l_ref[...] = jnp.zeros(l_ref.shape, F32)
        acc_ref[...] = jnp.zeros(acc_ref.shape, F32)

    @pl.when((k0 >> CHUNK_SHIFT) <= ((q0 + tq - 1) >> CHUNK_SHIFT))
    def _():
        vis = _visible(q0, k0, tq, tk, s_real)
        scores = [_nt_dot(q_ref[0, :, h * LANE:(h + 1) * LANE], k_ref[0, :, h * LANE:(h + 1) * LANE]) for h in heads]
        new = [_softmax_update(jnp.where(vis, scores[h], NEG), v_ref[0, :, (h // 2) * LANE:(h // 2 + 1) * LANE],
                               m_ref[h], l_ref[h], acc_ref[h]) for h in heads]
        for h in heads:
            m_ref[h], l_ref[h], acc_ref[h] = new[h]

    @pl.when(ki == pl.num_programs(2) - 1)
    def _():
        _write_pairs(o_ref, [l_ref[h] for h in heads], [acc_ref[h] for h in heads])


def _mla_attn(qa, ka, va, *, q_off, s_real, tq, tk):
    b, t, _ = qa.shape
    s_pad = ka.shape[1]
    nq, nk = t // tq, s_pad // tk

    def kv_map(bi, qi, ki):
        last = ((((q_off + qi * tq + tq - 1) >> CHUNK_SHIFT) + 1) << CHUNK_SHIFT) - 1
        return (bi, jnp.minimum(ki, jnp.minimum(last // tk, nk - 1)), 0)

    kern = functools.partial(_mla_attn_kernel, tq=tq, tk=tk, q_off=q_off, s_real=s_real)
    return pl.pallas_call(
        kern, grid=(b, nq, nk),
        in_specs=[pl.BlockSpec((1, tq, MLA_HEADS * LANE), lambda bi, qi, ki: (bi, qi, 0)),
                  pl.BlockSpec((1, tk, MLA_HEADS * LANE), kv_map),
                  pl.BlockSpec((1, tk, MLA_WIDTH), kv_map)],
        out_specs=pl.BlockSpec((1, tq, MLA_WIDTH), lambda bi, qi, ki: (bi, qi, 0)),
        out_shape=jax.ShapeDtypeStruct((b, t, MLA_WIDTH), BF16),
        scratch_shapes=[pltpu.VMEM((MLA_HEADS, tq, 1), F32), pltpu.VMEM((MLA_HEADS, tq, 1), F32),
                        pltpu.VMEM((MLA_HEADS, tq, LANE), F32)],
        compiler_params=pltpu.CompilerParams(dimension_semantics=("parallel", "parallel", "arbitrary"),
                                             vmem_limit_bytes=VMEM_LIMIT),
        name="mla_attn")(qa, ka, va)


def _fold_lanes(c):
    part = c[:, :LANE]
    for j in range(1, c.shape[1] // LANE):
        part = part + c[:, j * LANE:(j + 1) * LANE]
    return part


def _dsa_kernel(qb_ref, qi_ref, wi_ref, kb_ref, vb_ref, ik2_ref, tri_ref, o_ref,
                key_ref, qm_ref, qim_ref, *, tq, tk, q_off, s_real, topk):
    q0 = q_off + pl.program_id(1) * tq
    vis_end = jnp.minimum((((q0 + tq - 1) >> CHUNK_SHIFT) + 1) << CHUNK_SHIFT, s_real)
    n_vis = (vis_end + tk - 1) // tk
    lane = _lane_iota((tq, LANE))
    lo = lane < LANE // 2

    for h in range(DSA_HEADS):
        pair = qb_ref[0, :, (h // 2) * LANE:(h // 2 + 1) * LANE]
        qm_ref[h] = jnp.where(lo if h % 2 == 0 else ~lo, pair, jnp.zeros_like(pair))
    for h in range(IDX_HEADS):
        pair = qi_ref[0, :, (h // 2) * LANE:(h // 2 + 1) * LANE]
        qim_ref[h] = jnp.where(lo if h % 2 == 0 else ~lo, pair, jnp.zeros_like(pair))
    w_cols = [wi_ref[0, :, h:h + 1] for h in range(IDX_HEADS)]

    def score_block(kb, carry):
        k0 = pl.multiple_of(kb * tk, tk)
        ik = ik2_ref[0, pl.ds(k0, tk), :]
        score = jnp.zeros((tq, tk), F32)
        for h in range(IDX_HEADS):
            score = score + jnp.maximum(_nt_dot(qim_ref[h], ik), 0.0) * w_cols[h]
        score = jnp.where(score == 0.0, 0.0, score)
        score = jnp.where(_visible(q0, k0, tq, tk, s_real), score, -jnp.inf)
        bits = pltpu.bitcast(score, I32)
        key_ref[kb] = bits ^ ((bits >> 31) & INT_MAX)
        return carry

    lax.fori_loop(0, n_vis, score_block, 0)

    def count(pred):
        def body(kb, acc):
            return acc + _fold_lanes(jnp.where(pred(key_ref[kb], kb), 1.0, 0.0))
        return jnp.sum(lax.fori_loop(0, n_vis, body, jnp.zeros((tq, LANE), F32)), axis=-1, keepdims=True)

    def count_ge(t):
        return count(lambda blk, kb: blk >= t)

    kf = float(topk)
    t = jnp.where(count_ge(jnp.zeros((tq, 1), I32)) >= kf, 0, INT_MIN).astype(I32)

    def value_bit(i, t):
        t_try = t | (jnp.int32(1) << (30 - i))
        return jnp.where(count_ge(t_try) >= kf, t_try, t)

    t = lax.fori_loop(0, 31, value_bit, t)

    need = jnp.where(t == KEY_NEG_INF, 0.0, kf - count(lambda blk, kb: blk > t))

    heads = range(DSA_HEADS)

    def attend_block(kb, carry):
        tied_before, ms, ls, accs = carry
        k0 = pl.multiple_of(kb * tk, tk)
        blk = key_ref[kb]
        tied = jnp.where(blk == t, 1.0, 0.0)
        tied_rank = (tied_before + _dot(tied.astype(BF16), tri_ref[...])) * tied
        sel = (blk >= t) & (tied_rank <= need)
        k_pairs = [kb_ref[0, pl.ds(k0, tk), p * LANE:(p + 1) * LANE] for p in range(DSA_HEADS // 2)]
        v_pairs = [vb_ref[0, pl.ds(k0, tk), p * LANE:(p + 1) * LANE] for p in range(DSA_HEADS // 2)]
        scores = [_nt_dot(qm_ref[h], k_pairs[h // 2]) for h in heads]
        new = [_softmax_update(jnp.where(sel, scores[h], NEG), v_pairs[h // 2], ms[h], ls[h], accs[h]) for h in heads]
        return (tied_before + jnp.sum(tied, axis=-1, keepdims=True),
                tuple(n[0] for n in new), tuple(n[1] for n in new), tuple(n[2] for n in new))

    col = lambda v: tuple(jnp.full((tq, 1), v, F32) for _ in heads)
    init = (jnp.zeros((tq, 1), F32), col(NEG), col(0.0), tuple(jnp.zeros((tq, LANE), F32) for _ in heads))
    _, _, ls, accs = lax.fori_loop(0, n_vis, attend_block, init)
    _write_pairs(o_ref, ls, accs)


def _dsa_attn(qb, qi, wi, kb, vb, ik2, *, q_off, s_real, tq, tk):
    b, t, _ = qb.shape
    s_pad = kb.shape[1]
    topk = min(TOPK_MAX, s_real // 4)
    qspec = lambda w: pl.BlockSpec((1, tq, w), lambda bi, i: (bi, i, 0))
    kspec = lambda w: pl.BlockSpec((1, s_pad, w), lambda bi, i: (bi, 0, 0), pipeline_mode=pl.Buffered(1))
    kern = functools.partial(_dsa_kernel, tq=tq, tk=tk, q_off=q_off, s_real=s_real, topk=topk)
    tri = (lax.broadcasted_iota(I32, (tk, tk), 0) <= lax.broadcasted_iota(I32, (tk, tk), 1)).astype(BF16)
    return pl.pallas_call(
        kern, grid=(b, t // tq),
        in_specs=[qspec(DSA_WIDTH), qspec(IDX_HEADS * IDX_HD), qspec(LANE),
                  kspec(DSA_WIDTH), kspec(DSA_WIDTH), kspec(LANE), _resident((tk, tk))],
        out_specs=qspec(DSA_WIDTH),
        out_shape=jax.ShapeDtypeStruct((b, t, DSA_WIDTH), BF16),
        scratch_shapes=[pltpu.VMEM((s_pad // tk, tq, tk), I32),
                        pltpu.VMEM((DSA_HEADS, tq, LANE), BF16), pltpu.VMEM((IDX_HEADS, tq, LANE), BF16)],
        compiler_params=pltpu.CompilerParams(dimension_semantics=("parallel", "arbitrary"),
                                             vmem_limit_bytes=VMEM_LIMIT),
        name="dsa_attn")(qb, qi, wi, kb, vb, ik2, tri)


def _pad_cols(w, width):
    return jnp.pad(w, ((0, 0), (0, width - w.shape[1])))


def _layer_weights(p, l):
    w_in = p["w_in"][l]
    off, pieces = 0, []
    for n in (MLA_Q_LORA, MLA_KV_LORA, MLA_ROPE, DSA_WIDTH, DSA_WIDTH, DSA_WIDTH, IDX_HEADS * IDX_HD, IDX_HD, IDX_HEADS):
        pieces.append(w_in[:, off:off + n])
        off += n
    c_q, c_kv, k_r, q_b, k_b, v_b, q_i, k_i, w_i = pieces
    k_r = jnp.pad(k_r, ((0, 0), (MLA_NOPE, LANE - MLA_QK)))
    w_in_p = jnp.concatenate([c_q, c_kv, k_r, q_b, k_b, v_b, q_i, k_i, k_i, _pad_cols(w_i, LANE)], axis=1)
    assert w_in_p.shape[1] == C_END

    d_lora = p["mla_w_uq"].shape[1]
    w_uq = p["mla_w_uq"][l].reshape(d_lora, MLA_HEADS, MLA_QK)
    w_uq = jnp.pad(w_uq, ((0, 0), (0, 0), (0, LANE - MLA_QK))).reshape(d_lora, MLA_HEADS * LANE)
    w_ukv = p["mla_w_ukv"][l].reshape(MLA_KV_LORA, MLA_HEADS, MLA_NOPE + MLA_V)
    w_nope = jnp.pad(w_ukv[:, :, :MLA_NOPE], ((0, 0), (0, 0), (0, LANE - MLA_NOPE))).reshape(MLA_KV_LORA, MLA_HEADS * LANE)
    w_v = w_ukv[:, :, MLA_NOPE:].reshape(MLA_KV_LORA, MLA_WIDTH)
    w_out = p["w_out"][l]

    row = lambda g: g[l][None, :].astype(F32)
    pad96 = lambda g: jnp.pad(g[l].astype(F32), (0, LANE - MLA_QK))[None, :]
    twice = lambda g: jnp.tile(g[l].astype(F32), 2)[None, :]
    lw = {
        "w_in": w_in_p.astype(BF16), "w_uq": w_uq.astype(BF16), "w_ukv_nope": w_nope.astype(BF16),
        "w_ukv_v": w_v.astype(BF16), "w_out_a": w_out[:MLA_WIDTH].astype(BF16), "w_out_b": w_out[MLA_WIDTH:].astype(BF16),
        "mix_norm": row(p["mix_norm"]), "mla_q_norm": row(p["mla_q_norm"]), "mla_kv_norm": row(p["mla_kv_norm"]),
        "mla_q_gain": pad96(p["mla_q_gain"]), "mla_k_gain": pad96(p["mla_k_gain"]),
        "dsa_q_gain": twice(p["dsa_q_gain"]), "dsa_k_gain": twice(p["dsa_k_gain"]),
    }
    for f in ("ffn1", "ffn2"):
        lw[f + "_norm"] = row(p[f + "_norm"])
        for w in ("w_gate", "w_up", "w_down"):
            lw[f + "_" + w] = p[f + "_" + w][l].astype(BF16)
    return lw


def _rope_tables(pos, rows):
    def cs(rot):
        inv = 1.0 / (ROPE_THETA ** (jnp.arange(0, rot, 2, dtype=F32) / rot))
        ang = pos.astype(F32)[:, None] * inv[None, :]
        return jnp.cos(ang), jnp.sin(ang)

    t = pos.shape[0]
    cos_a, sin_a = cs(MLA_ROPE)
    ones = lambda w: jnp.ones((t, w), F32)
    zeros = lambda w: jnp.zeros((t, w), F32)
    ca = jnp.concatenate([ones(MLA_NOPE), cos_a, cos_a, ones(LANE - MLA_QK)], axis=1)
    sa = jnp.concatenate([zeros(MLA_NOPE), -sin_a, sin_a, zeros(LANE - MLA_QK)], axis=1)
    cos_b, sin_b = cs(DSA_ROT)
    cb = jnp.tile(jnp.concatenate([cos_b, cos_b, ones(DSA_HD - DSA_ROT)], axis=1), (1, 2))
    sb = jnp.tile(jnp.concatenate([-sin_b, sin_b, zeros(DSA_HD - DSA_ROT)], axis=1), (1, 2))
    reps = max(1, rows // t)
    return tuple(jnp.tile(x, (reps, 1)) for x in (ca, sa, cb, sb))


def _pad_keys(x, s_pad):
    return jnp.pad(x, ((0, 0), (0, s_pad - x.shape[1]), (0, 0)))


def _trunk_layer(x, lw, tables, past, *, b, t, q_off, tq_mla, tk_mla, tq_dsa, tk_dsa):
    h = _ffn(x, lw["ffn1_norm"], lw["ffn1_w_gate"], lw["ffn1_w_up"], lw["ffn1_w_down"])
    (ckv, krope, kslab, kb, vb, ki, qa, qb, kb16, vb16, qi, ik2, wi) = _proj(h, lw, tables, t)
    per_batch = lambda a: a.reshape(b, t, a.shape[-1])
    if past is None:
        ckv_all, kslab_all = ckv, kslab
        kb_all, vb_all, ik2_all = per_batch(kb16), per_batch(vb16), per_batch(ik2)
        s_real = t
    else:
        p_ckv, p_krope, p_kb, p_vb, p_ki = past
        s_real = p_ckv.shape[1] + t
        cat = lambda old, new: jnp.concatenate([old, per_batch(new)], axis=1)
        ckv_all = cat(p_ckv, ckv).reshape(b * s_real, MLA_KV_LORA)
        p_kslab = jnp.pad(p_krope, ((0, 0), (0, 0), (MLA_NOPE, LANE - MLA_QK)))
        kslab_all = cat(p_kslab, kslab).reshape(b * s_real, LANE)
        kb_all = cat(p_kb.reshape(b, -1, DSA_WIDTH).astype(BF16), kb16)
        vb_all = cat(p_vb.reshape(b, -1, DSA_WIDTH).astype(BF16), vb16)
        ik2_all = cat(jnp.tile(p_ki, (1, 1, 2)).astype(BF16), ik2)
    ka, va = _mla_kv(ckv_all, kslab_all, lw)
    s_pad_a = pl.cdiv(s_real, tk_mla) * tk_mla
    ka = _pad_keys(ka.reshape(b, s_real, -1), s_pad_a)
    va = _pad_keys(va.reshape(b, s_real, -1), s_pad_a)
    oa = _mla_attn(per_batch(qa), ka, va, q_off=q_off, s_real=s_real, tq=tq_mla, tk=tk_mla)
    s_pad_b = pl.cdiv(s_real, tk_dsa) * tk_dsa
    ob = _dsa_attn(per_batch(qb), per_batch(qi), per_batch(wi), _pad_keys(kb_all, s_pad_b), _pad_keys(vb_all, s_pad_b),
                   _pad_keys(ik2_all, s_pad_b), q_off=q_off, s_real=s_real, tq=tq_dsa, tk=tk_dsa)
    y = _ffn(h, lw["ffn2_norm"], lw["ffn2_w_gate"], lw["ffn2_w_up"], lw["ffn2_w_down"],
             attn=(oa.reshape(b * t, MLA_WIDTH), ob.reshape(b * t, DSA_WIDTH), lw["w_out_a"], lw["w_out_b"]))
    rows = (per_batch(ckv), per_batch(krope), per_batch(kb).reshape(b, t, DSA_HEADS, DSA_HD),
            per_batch(vb).reshape(b, t, DSA_HEADS, DSA_HD), per_batch(ki))
    return y, rows


def kernel(x_prompt, x_sample, cache_mla_ckv, cache_mla_krope, cache_dsa_k, cache_dsa_v, cache_idx_k,
           ffn1_norm, ffn1_w_gate, ffn1_w_up, ffn1_w_down, mix_norm, w_in,
           mla_q_norm, mla_w_uq, mla_kv_norm, mla_w_ukv, mla_q_gain, mla_k_gain,
           dsa_q_gain, dsa_k_gain, w_out, ffn2_norm, ffn2_w_gate, ffn2_w_up, ffn2_w_down):
    params = dict(ffn1_norm=ffn1_norm, ffn1_w_gate=ffn1_w_gate, ffn1_w_up=ffn1_w_up, ffn1_w_down=ffn1_w_down,
                  mix_norm=mix_norm, w_in=w_in, mla_q_norm=mla_q_norm, mla_w_uq=mla_w_uq, mla_kv_norm=mla_kv_norm,
                  mla_w_ukv=mla_w_ukv, mla_q_gain=mla_q_gain, mla_k_gain=mla_k_gain, dsa_q_gain=dsa_q_gain,
                  dsa_k_gain=dsa_k_gain, w_out=w_out, ffn2_norm=ffn2_norm, ffn2_w_gate=ffn2_w_gate,
                  ffn2_w_up=ffn2_w_up, ffn2_w_down=ffn2_w_down)
    depth = w_in.shape[0]
    d_model = x_prompt.shape[-1]
    weights = [_layer_weights(params, l) for l in range(depth)]

    b_p, t_p = x_prompt.shape[:2]
    n_p = b_p * t_p
    tabs_p = _rope_tables(jnp.arange(t_p, dtype=I32), _row_tile(n_p, 512))
    tile_p = dict(tq_mla=min(t_p, 512), tk_mla=min(t_p, 512), tq_dsa=min(t_p, 256), tk_dsa=min(t_p, 512))
    h_p = x_prompt.reshape(n_p, d_model)
    p_rows = []
    for l in range(depth):
        h_p, rows = _trunk_layer(h_p, weights[l], tabs_p, None, b=b_p, t=t_p, q_off=0, **tile_p)
        p_rows.append(rows)

    b_s, t_s = x_sample.shape[:2]
    n_s = b_s * t_s
    past_len = cache_mla_ckv.shape[2]
    tabs_s = _rope_tables(past_len + jnp.arange(t_s, dtype=I32), _row_tile(n_s, 512))
    tile_s = dict(tq_mla=t_s, tk_mla=LANE, tq_dsa=t_s, tk_dsa=2 * LANE)
    h_s = x_sample.reshape(n_s, d_model)
    s_rows = []
    for l in range(depth):
        past = (cache_mla_ckv[l], cache_mla_krope[l], cache_dsa_k[l], cache_dsa_v[l], cache_idx_k[l])
        h_s, rows = _trunk_layer(h_s, weights[l], tabs_s, past, b=b_s, t=t_s, q_off=past_len, **tile_s)
        s_rows.append(rows)

    stack = lambda rows_by_layer, i: jnp.stack([r[i] for r in rows_by_layer])
    return (h_p.reshape(b_p, t_p, d_model), h_s.reshape(b_s, t_s, d_model),
            *[stack(p_rows, i) for i in range(5)], *[stack(s_rows, i) for i in range(5)])
```

```python
import functools

import jax
import jax.numpy as jnp
from jax import lax
from jax.experimental import pallas as pl
from jax.experimental.pallas import tpu as pltpu

F32 = jnp.float32
BF16 = jnp.bfloat16
I32 = jnp.int32

CHUNK_SHIFT = 6
ROPE_THETA = 500000.0
EPS = 1e-6
MLA_HEADS = 8
MLA_NOPE = 64
MLA_ROPE = 32
MLA_QK = MLA_NOPE + MLA_ROPE
MLA_V = 64
MLA_Q_LORA = 256
MLA_KV_LORA = 128
DSA_HEADS = 8
DSA_HD = 64
DSA_ROT = 16
IDX_HEADS = 4
IDX_HD = 64
IDX_W_SCALE = (IDX_HD * IDX_HEADS) ** -0.5
TOPK_MAX = 256
DSA_WIDTH = DSA_HEADS * DSA_HD
MLA_WIDTH = MLA_HEADS * MLA_V

LANE = 128
VMEM_LIMIT = 56 * 1024 * 1024

NEG = -1e30
LOG2E = 1.4426950408889634
INT_MIN = -(2 ** 31)
INT_MAX = 2 ** 31 - 1
KEY_NEG_INF = INT_MIN + 0x7FFFFF


def _nt_dot(a, b):
    return lax.dot_general(a, b, (((1,), (1,)), ((), ())), preferred_element_type=F32)


def _dot(a, b):
    return jnp.dot(a, b, preferred_element_type=F32)


def _rms(x, g):
    return x * lax.rsqrt(jnp.mean(x * x, axis=-1, keepdims=True) + EPS) * g


def _lane_iota(shape):
    return lax.broadcasted_iota(I32, shape, len(shape) - 1)


FFN_CHUNK = 256


def _ffn_body(x, g_ref, wg_ref, wu_ref, wd_ref, o_ref):
    xb = _rms(x, g_ref[...]).astype(BF16)
    d_ff = wg_ref.shape[1]
    acc = jnp.zeros(x.shape, F32)
    for c in range(d_ff // FFN_CHUNK):
        sl = slice(c * FFN_CHUNK, (c + 1) * FFN_CHUNK)
        gate = _dot(xb, wg_ref[:, sl])
        up = _dot(xb, wu_ref[:, sl])
        act = (gate * jax.nn.sigmoid(gate) * up).astype(BF16)
        acc = acc + _dot(act, wd_ref[sl, :])
    o_ref[...] = x + 0.5 * acc


def _ffn_kernel(x_ref, g_ref, wg_ref, wu_ref, wd_ref, o_ref):
    _ffn_body(x_ref[...], g_ref, wg_ref, wu_ref, wd_ref, o_ref)


def _out_ffn_kernel(h_ref, oa_ref, ob_ref, woa_ref, wob_ref, g_ref, wg_ref, wu_ref, wd_ref, o_ref):
    x = h_ref[...] + _dot(oa_ref[...], woa_ref[...]) + _dot(ob_ref[...], wob_ref[...])
    _ffn_body(x, g_ref, wg_ref, wu_ref, wd_ref, o_ref)


def _resident(shape):
    nd = len(shape)
    return pl.BlockSpec(shape, lambda *_: (0,) * nd, pipeline_mode=pl.Buffered(1))


def _row_tile(n, pref):
    for t in range(min(n, pref), 0, -16):
        if n % t == 0:
            return t
    raise ValueError(f"no row tile for {n} rows")


def _ffn(x, g, wg, wu, wd, attn=None):
    n, d = x.shape
    tm = _row_tile(n, 512)
    row = lambda w: pl.BlockSpec((tm, w), lambda i: (i, 0))
    w_specs = [_resident(g.shape), _resident(wg.shape), _resident(wu.shape), _resident(wd.shape)]
    params = pltpu.CompilerParams(dimension_semantics=("parallel",), vmem_limit_bytes=VMEM_LIMIT)
    out_shape = jax.ShapeDtypeStruct((n, d), F32)
    if attn is None:
        return pl.pallas_call(_ffn_kernel, grid=(n // tm,), in_specs=[row(d)] + w_specs, out_specs=row(d),
                              out_shape=out_shape, compiler_params=params, name="ffn")(x, g, wg, wu, wd)
    oa, ob, woa, wob = attn
    return pl.pallas_call(
        _out_ffn_kernel, grid=(n // tm,),
        in_specs=[row(d), row(oa.shape[1]), row(ob.shape[1]), _resident(woa.shape), _resident(wob.shape)] + w_specs,
        out_specs=row(d), out_shape=out_shape, compiler_params=params, name="out_ffn")(x, oa, ob, woa, wob, g, wg, wu, wd)


C_CQ = 0
C_CKV = C_CQ + MLA_Q_LORA
C_KR = C_CKV + MLA_KV_LORA
C_QB = C_KR + LANE
C_KB = C_QB + DSA_WIDTH
C_VB = C_KB + DSA_WIDTH
C_QI = C_VB + DSA_WIDTH
C_KI = C_QI + IDX_HEADS * IDX_HD
C_WI = C_KI + LANE
C_END = C_WI + LANE


def _rope_a(x, c, s):
    lane = _lane_iota(x.shape)
    partner = jnp.where(lane < MLA_NOPE + MLA_ROPE // 2, pltpu.roll(x, LANE - MLA_ROPE // 2, 1),
                        pltpu.roll(x, MLA_ROPE // 2, 1))
    return x * c + partner * s


def _rope_b(x, c, s):
    lane = _lane_iota(x.shape)
    half = DSA_ROT // 2
    partner = jnp.where((lane & (DSA_HD - 1)) < half, pltpu.roll(x, LANE - half, 1), pltpu.roll(x, half, 1))
    return x * c + partner * s


def _head96_norm(x, g):
    ms = jnp.sum(x * x, axis=-1, keepdims=True) * (1.0 / MLA_QK)
    return x * lax.rsqrt(ms + EPS) * g


def _head64_norm(x, g2):
    lane = _lane_iota(x.shape)
    lo = lane < DSA_HD
    sq = x * x
    s_lo = jnp.sum(jnp.where(lo, sq, 0.0), axis=-1, keepdims=True)
    s_hi = jnp.sum(jnp.where(lo, 0.0, sq), axis=-1, keepdims=True)
    ms = jnp.where(lo, s_lo, s_hi) * (1.0 / DSA_HD)
    return x * lax.rsqrt(ms + EPS) * g2


def _proj_kernel(h_ref, gmix_ref, win_ref, gq_ref, wuq_ref, gkv_ref, gqa_ref, gqb_ref, gkb_ref,
                 ca_ref, sa_ref, cb_ref, sb_ref,
                 ckv_ref, krope_ref, kslab_ref, kb_ref, vb_ref, ki_ref,
                 qa_ref, qb_ref, kb16_ref, vb16_ref, qi_ref, ik2_ref, wi_ref):
    u = _rms(h_ref[...], gmix_ref[...]).astype(BF16)
    ca, sa, cb, sb = ca_ref[...], sa_ref[...], cb_ref[...], sb_ref[...]

    def cols(start, width):
        return _dot(u, win_ref[:, start:start + width])

    cq = _rms(cols(C_CQ, MLA_Q_LORA), gq_ref[...]).astype(BF16)
    qa = _dot(cq, wuq_ref[...])
    qa_scale = MLA_QK ** -0.5 * LOG2E
    for h in range(MLA_HEADS):
        sl = slice(h * LANE, (h + 1) * LANE)
        qa_ref[:, sl] = (_head96_norm(_rope_a(qa[:, sl], ca, sa), gqa_ref[...]) * qa_scale).astype(BF16)

    ckv_ref[...] = _rms(cols(C_CKV, MLA_KV_LORA), gkv_ref[...])
    kslab = _rope_a(cols(C_KR, LANE), ca, sa)
    kslab_ref[...] = kslab
    krope_ref[...] = kslab[:, MLA_NOPE:MLA_NOPE + MLA_ROPE]

    qb = cols(C_QB, DSA_WIDTH)
    kb = cols(C_KB, DSA_WIDTH)
    qb_scale = DSA_HD ** -0.5 * LOG2E
    for p in range(DSA_WIDTH // LANE):
        sl = slice(p * LANE, (p + 1) * LANE)
        qb_ref[:, sl] = (_rope_b(_head64_norm(qb[:, sl], gqb_ref[...]), cb, sb) * qb_scale).astype(BF16)
        kp = _rope_b(_head64_norm(kb[:, sl], gkb_ref[...]), cb, sb)
        kb_ref[:, sl] = kp
        kb16_ref[:, sl] = kp.astype(BF16)
    vb = cols(C_VB, DSA_WIDTH)
    vb_ref[...] = vb
    vb16_ref[...] = vb.astype(BF16)

    qi = cols(C_QI, IDX_HEADS * IDX_HD)
    for p in range(IDX_HEADS * IDX_HD // LANE):
        sl = slice(p * LANE, (p + 1) * LANE)
        qi_ref[:, sl] = _rope_b(qi[:, sl], cb, sb).astype(BF16)
    ik2 = _rope_b(cols(C_KI, LANE), cb, sb)
    ki_ref[...] = ik2[:, :IDX_HD]
    ik2_ref[...] = ik2.astype(BF16)
    wi_ref[...] = cols(C_WI, LANE) * IDX_W_SCALE


def _proj(h, lw, tables, t_seq):
    n, d = h.shape
    tm = _row_tile(n, 512)
    ca, sa, cb, sb = tables
    n_tab = ca.shape[0] // tm
    row = lambda w: pl.BlockSpec((tm, w), lambda i: (i, 0))
    tab = pl.BlockSpec((tm, LANE), lambda i: (i % n_tab, 0))
    consts = [lw["mix_norm"], lw["w_in"], lw["mla_q_norm"], lw["w_uq"], lw["mla_kv_norm"],
              lw["mla_q_gain"], lw["dsa_q_gain"], lw["dsa_k_gain"]]
    out_widths = [(MLA_KV_LORA, F32), (MLA_ROPE, F32), (LANE, F32), (DSA_WIDTH, F32), (DSA_WIDTH, F32), (IDX_HD, F32),
                  (MLA_HEADS * LANE, BF16), (DSA_WIDTH, BF16), (DSA_WIDTH, BF16), (DSA_WIDTH, BF16),
                  (IDX_HEADS * IDX_HD, BF16), (LANE, BF16), (LANE, F32)]
    return pl.pallas_call(
        _proj_kernel, grid=(n // tm,),
        in_specs=[row(d)] + [_resident(c.shape) for c in consts] + [tab] * 4,
        out_specs=[row(w) for w, _ in out_widths],
        out_shape=[jax.ShapeDtypeStruct((n, w), dt) for w, dt in out_widths],
        compiler_params=pltpu.CompilerParams(dimension_semantics=("parallel",), vmem_limit_bytes=VMEM_LIMIT),
        name="proj")(h, *consts, ca, sa, cb, sb)


def _mla_kv_kernel(ckv_ref, kslab_ref, wn_ref, wv_ref, gk_ref, ka_ref, va_ref):
    c = ckv_ref[...].astype(BF16)
    kn = _dot(c, wn_ref[...])
    kslab = kslab_ref[...]
    for h in range(MLA_HEADS):
        sl = slice(h * LANE, (h + 1) * LANE)
        ka_ref[:, sl] = _head96_norm(kn[:, sl] + kslab, gk_ref[...]).astype(BF16)
    va_ref[...] = _dot(c, wv_ref[...]).astype(BF16)


def _mla_kv(ckv, kslab, lw):
    m = ckv.shape[0]
    tm = _row_tile(m, 512)
    row = lambda w: pl.BlockSpec((tm, w), lambda i: (i, 0))
    consts = [lw["w_ukv_nope"], lw["w_ukv_v"], lw["mla_k_gain"]]
    return pl.pallas_call(
        _mla_kv_kernel, grid=(m // tm,),
        in_specs=[row(MLA_KV_LORA), row(LANE)] + [_resident(c.shape) for c in consts],
        out_specs=[row(MLA_HEADS * LANE), row(MLA_WIDTH)],
        out_shape=[jax.ShapeDtypeStruct((m, MLA_HEADS * LANE), BF16), jax.ShapeDtypeStruct((m, MLA_WIDTH), BF16)],
        compiler_params=pltpu.CompilerParams(dimension_semantics=("parallel",), vmem_limit_bytes=VMEM_LIMIT),
        name="mla_kv")(ckv, kslab, *consts)


def _visible(q0, k0, tq, tk, s_real):
    q_chunk = (q0 + lax.broadcasted_iota(I32, (tq, tk), 0)) >> CHUNK_SHIFT
    k_idx = k0 + lax.broadcasted_iota(I32, (tq, tk), 1)
    return ((k_idx >> CHUNK_SHIFT) <= q_chunk) & (k_idx < s_real)


def _softmax_update(s, v_pair, m, l, acc):
    m_new = jnp.maximum(m, jnp.max(s, axis=-1, keepdims=True))
    alpha = jnp.exp2(m - m_new)
    p = jnp.exp2(s - m_new)
    l_new = alpha * l + jnp.sum(p, axis=-1, keepdims=True)
    return m_new, l_new, alpha * acc + _dot(p.astype(BF16), v_pair)


def _write_pairs(o_ref, ls, accs):
    lane = _lane_iota(accs[0].shape)
    for p in range(len(accs) // 2):
        even = accs[2 * p] / ls[2 * p]
        odd = accs[2 * p + 1] / ls[2 * p + 1]
        o_ref[0, :, p * LANE:(p + 1) * LANE] = jnp.where(lane < LANE // 2, even, odd).astype(o_ref.dtype)


def _mla_attn_kernel(q_ref, k_ref, v_ref, o_ref, m_ref, l_ref, acc_ref, *, tq, tk, q_off, s_real):
    qi, ki = pl.program_id(1), pl.program_id(2)
    q0 = q_off + qi * tq
    k0 = ki * tk

    heads = range(MLA_HEADS)

    @pl.when(ki == 0)
    def _():
        m_ref[...] = jnp.full(m_ref.shape, NEG, F32)
        l_ref[...] = jnp.zeros(l_ref.shape, F32)
        acc_ref[...] = jnp.zeros(acc_ref.shape, F32)

    def attend(masked):
        scores = [_nt_dot(q_ref[0, :, h * LANE:(h + 1) * LANE], k_ref[0, :, h * LANE:(h + 1) * LANE]) for h in heads]
        if masked:
            vis = _visible(q0, k0, tq, tk, s_real)
            scores = [jnp.where(vis, s, NEG) for s in scores]
        new = [_softmax_update(scores[h], v_ref[0, :, (h // 2) * LANE:(h // 2 + 1) * LANE],
                               m_ref[h], l_ref[h], acc_ref[h]) for h in heads]
        for h in heads:
            m_ref[h], l_ref[h], acc_ref[h] = new[h]

    needed = (k0 >> CHUNK_SHIFT) <= ((q0 + tq - 1) >> CHUNK_SHIFT)
    unmasked = (((k0 + tk - 1) >> CHUNK_SHIFT) <= (q0 >> CHUNK_SHIFT)) & (k0 + tk <= s_real)
    pl.when(needed & unmasked)(lambda: attend(False))
    pl.when(needed & ~unmasked)(lambda: attend(True))

    @pl.when(ki == pl.num_programs(2) - 1)
    def _():
        _write_pairs(o_ref, [l_ref[h] for h in heads], [acc_ref[h] for h in heads])


def _mla_attn(qa, ka, va, *, q_off, s_real, tq, tk):
    b, t, _ = qa.shape
    s_pad = ka.shape[1]
    nq, nk = t // tq, s_pad // tk

    def kv_map(bi, qi, ki):
        last = ((((q_off + qi * tq + tq - 1) >> CHUNK_SHIFT) + 1) << CHUNK_SHIFT) - 1
        return (bi, jnp.minimum(ki, jnp.minimum(last // tk, nk - 1)), 0)

    kern = functools.partial(_mla_attn_kernel, tq=tq, tk=tk, q_off=q_off, s_real=s_real)
    return pl.pallas_call(
        kern, grid=(b, nq, nk),
        in_specs=[pl.BlockSpec((1, tq, MLA_HEADS * LANE), lambda bi, qi, ki: (bi, qi, 0)),
                  pl.BlockSpec((1, tk, MLA_HEADS * LANE), kv_map),
                  pl.BlockSpec((1, tk, MLA_WIDTH), kv_map)],
        out_specs=pl.BlockSpec((1, tq, MLA_WIDTH), lambda bi, qi, ki: (bi, qi, 0)),
        out_shape=jax.ShapeDtypeStruct((b, t, MLA_WIDTH), BF16),
        scratch_shapes=[pltpu.VMEM((MLA_HEADS, tq, 1), F32), pltpu.VMEM((MLA_HEADS, tq, 1), F32),
                        pltpu.VMEM((MLA_HEADS, tq, LANE), F32)],
        compiler_params=pltpu.CompilerParams(dimension_semantics=("parallel", "parallel", "arbitrary"),
                                             vmem_limit_bytes=VMEM_LIMIT),
        name="mla_attn")(qa, ka, va)


FALSE_POSITION_PROBES = 24
MAX_PROBES = 64


def _fold_lanes(c, op):
    part = c[:, :LANE]
    for j in range(1, c.shape[1] // LANE):
        part = op(part, c[:, j * LANE:(j + 1) * LANE])
    return part


def _sortable(bits):
    return bits ^ ((bits >> 31) & INT_MAX)


def _dsa_kernel(qb_ref, qi_ref, wi_ref, kb_ref, vb_ref, ik2_ref, tri_ref, o_ref,
                key_ref, qm_ref, qim_ref, *, tq, tk, q_off, s_real, topk):
    q0 = q_off + pl.program_id(1) * tq
    vis_end = jnp.minimum((((q0 + tq - 1) >> CHUNK_SHIFT) + 1) << CHUNK_SHIFT, s_real)
    n_vis = (vis_end + tk - 1) // tk
    lane = _lane_iota((tq, LANE))
    lo = lane < LANE // 2

    for h in range(DSA_HEADS):
        pair = qb_ref[0, :, (h // 2) * LANE:(h // 2 + 1) * LANE]
        qm_ref[h] = jnp.where(lo if h % 2 == 0 else ~lo, pair, jnp.zeros_like(pair))
    for h in range(IDX_HEADS):
        pair = qi_ref[0, :, (h // 2) * LANE:(h // 2 + 1) * LANE]
        qim_ref[h] = jnp.where(lo if h % 2 == 0 else ~lo, pair, jnp.zeros_like(pair))
    w_cols = [wi_ref[0, :, h:h + 1] for h in range(IDX_HEADS)]

    def score_block(kb, carry):
        row_max, row_min = carry
        k0 = pl.multiple_of(kb * tk, tk)
        ik = ik2_ref[0, pl.ds(k0, tk), :]
        score = jnp.zeros((tq, tk), F32)
        for h in range(IDX_HEADS):
            score = score + jnp.maximum(_nt_dot(qim_ref[h], ik), 0.0) * w_cols[h]
        score = jnp.where(score == 0.0, 0.0, score)
        vis = _visible(q0, k0, tq, tk, s_real)
        masked = jnp.where(vis, score, -jnp.inf)
        key_ref[kb] = _sortable(pltpu.bitcast(masked, I32))
        return (jnp.maximum(row_max, _fold_lanes(masked, jnp.maximum)),
                jnp.minimum(row_min, _fold_lanes(jnp.where(vis, score, jnp.inf), jnp.minimum)))

    row_max, row_min = lax.fori_loop(0, n_vis, score_block,
                                     (jnp.full((tq, LANE), -jnp.inf, F32), jnp.full((tq, LANE), jnp.inf, F32)))
    row_max = jnp.max(row_max, axis=-1, keepdims=True)
    row_min = jnp.min(row_min, axis=-1, keepdims=True)

    def count_ge(t):
        def body(kb, acc):
            return acc + _fold_lanes(jnp.where(key_ref[kb] >= t, 1.0, 0.0), jnp.add)
        return jnp.sum(lax.fori_loop(0, n_vis, body, jnp.zeros((tq, LANE), F32)), axis=-1, keepdims=True)

    kf = float(topk)
    q_pos = q0 + lax.broadcasted_iota(I32, (tq, 1), 0)
    n_row = jnp.minimum(((q_pos >> CHUNK_SHIFT) + 1) << CHUNK_SHIFT, s_real).astype(F32)
    few = n_row < kf
    lo0 = _sortable(lax.bitcast_convert_type(row_min, I32))
    hi0 = _sortable(lax.bitcast_convert_type(row_max, I32)) + 1

    def finished(lo_k, hi_k, c_lo):
        return few | (c_lo == kf) | (hi_k == lo_k + 1)

    def probe_step(carry):
        it, _, lo_k, hi_k, c_lo, c_hi, g_lo, g_hi, last = carry
        v_lo = lax.bitcast_convert_type(_sortable(lo_k), F32)
        v_hi = lax.bitcast_convert_type(_sortable(hi_k), F32)
        a = jnp.log(c_lo * (1.0 / kf)) * g_lo
        b = jnp.log(kf / jnp.maximum(c_hi, 0.5)) * g_hi
        p = _sortable(lax.bitcast_convert_type(v_lo + (v_hi - v_lo) * (a / (a + b)), I32))
        p = jnp.where(it >= FALSE_POSITION_PROBES, (lo_k >> 1) + (hi_k >> 1) + (lo_k & hi_k & 1), p)
        p = jnp.where((it == 0) & (lo_k < 0) & (hi_k > 0), 0, p)
        p = jnp.where(lo_k == 0, 1, p)
        p = jnp.minimum(jnp.maximum(p, lo_k + 1), hi_k - 1)
        c = count_ge(p)
        open_ = ~finished(lo_k, hi_k, c_lo)
        up = open_ & (c >= kf)
        down = open_ & (c < kf)
        lo_k, c_lo = jnp.where(up, p, lo_k), jnp.where(up, c, c_lo)
        hi_k, c_hi = jnp.where(down, p, hi_k), jnp.where(down, c, c_hi)
        g_lo = jnp.where(down, jnp.where(last < 0.0, 0.5 * g_lo, 1.0), jnp.where(up, 1.0, g_lo))
        g_hi = jnp.where(up, jnp.where(last > 0.0, 0.5 * g_hi, 1.0), jnp.where(down, 1.0, g_hi))
        last = jnp.where(up, 1.0, jnp.where(down, -1.0, last))
        n_open = jnp.max(jnp.where(finished(lo_k, hi_k, c_lo), 0, 1))
        return it + 1, n_open, lo_k, hi_k, c_lo, c_hi, g_lo, g_hi, last

    ones = jnp.ones((tq, 1), F32)
    init = (jnp.int32(0), jnp.max(jnp.where(finished(lo0, hi0, n_row), 0, 1)), lo0, hi0, n_row, 0.0 * ones,
            ones, ones, 0.0 * ones)
    final = lax.while_loop(lambda c: (c[1] > 0) & (c[0] < MAX_PROBES), probe_step, init)
    t, c_lo, c_hi = final[2], final[4], final[5]

    t = jnp.where(few, KEY_NEG_INF, t)
    need = jnp.where(few, 0.0, jnp.where(c_lo == kf, kf, kf - c_hi))

    heads = range(DSA_HEADS)

    def attend_block(kb, carry):
        tied_before, ms, ls, accs = carry
        k0 = pl.multiple_of(kb * tk, tk)
        blk = key_ref[kb]
        tied = jnp.where(blk == t, 1.0, 0.0)
        tied_rank = (tied_before + _dot(tied.astype(BF16), tri_ref[...])) * tied
        sel = (blk >= t) & (tied_rank <= need)
        k_pairs = [kb_ref[0, pl.ds(k0, tk), p * LANE:(p + 1) * LANE] for p in range(DSA_HEADS // 2)]
        v_pairs = [vb_ref[0, pl.ds(k0, tk), p * LANE:(p + 1) * LANE] for p in range(DSA_HEADS // 2)]
        scores = [_nt_dot(qm_ref[h], k_pairs[h // 2]) for h in heads]
        new = [_softmax_update(jnp.where(sel, scores[h], NEG), v_pairs[h // 2], ms[h], ls[h], accs[h]) for h in heads]
        return (tied_before + jnp.sum(tied, axis=-1, keepdims=True),
                tuple(n[0] for n in new), tuple(n[1] for n in new), tuple(n[2] for n in new))

    col = lambda v: tuple(jnp.full((tq, 1), v, F32) for _ in heads)
    init = (jnp.zeros((tq, 1), F32), col(NEG), col(0.0), tuple(jnp.zeros((tq, LANE), F32) for _ in heads))
    _, _, ls, accs = lax.fori_loop(0, n_vis, attend_block, init)
    _write_pairs(o_ref, ls, accs)


def _dsa_attn(qb, qi, wi, kb, vb, ik2, *, q_off, s_real, tq, tk):
    b, t, _ = qb.shape
    s_pad = kb.shape[1]
    topk = min(TOPK_MAX, s_real // 4)
    qspec = lambda w: pl.BlockSpec((1, tq, w), lambda bi, i: (bi, i, 0))
    kspec = lambda w: pl.BlockSpec((1, s_pad, w), lambda bi, i: (bi, 0, 0), pipeline_mode=pl.Buffered(1))
    kern = functools.partial(_dsa_kernel, tq=tq, tk=tk, q_off=q_off, s_real=s_real, topk=topk)
    tri = (lax.broadcasted_iota(I32, (tk, tk), 0) <= lax.broadcasted_iota(I32, (tk, tk), 1)).astype(BF16)
    return pl.pallas_call(
        kern, grid=(b, t // tq),
        in_specs=[qspec(DSA_WIDTH), qspec(IDX_HEADS * IDX_HD), qspec(LANE),
                  kspec(DSA_WIDTH), kspec(DSA_WIDTH), kspec(LANE), _resident((tk, tk))],
        out_specs=qspec(DSA_WIDTH),
        out_shape=jax.ShapeDtypeStruct((b, t, DSA_WIDTH), BF16),
        scratch_shapes=[pltpu.VMEM((s_pad // tk, tq, tk), I32),
                        pltpu.VMEM((DSA_HEADS, tq, LANE), BF16), pltpu.VMEM((IDX_HEADS, tq, LANE), BF16)],
        compiler_params=pltpu.CompilerParams(dimension_semantics=("parallel", "arbitrary"),
                                             vmem_limit_bytes=VMEM_LIMIT),
        name="dsa_attn")(qb, qi, wi, kb, vb, ik2, tri)


def _pad_cols(w, width):
    return jnp.pad(w, ((0, 0), (0, width - w.shape[1])))


def _layer_weights(p, l):
    w_in = p["w_in"][l]
    off, pieces = 0, []
    for n in (MLA_Q_LORA, MLA_KV_LORA, MLA_ROPE, DSA_WIDTH, DSA_WIDTH, DSA_WIDTH, IDX_HEADS * IDX_HD, IDX_HD, IDX_HEADS):
        pieces.append(w_in[:, off:off + n])
        off += n
    c_q, c_kv, k_r, q_b, k_b, v_b, q_i, k_i, w_i = pieces
    k_r = jnp.pad(k_r, ((0, 0), (MLA_NOPE, LANE - MLA_QK)))
    w_in_p = jnp.concatenate([c_q, c_kv, k_r, q_b, k_b, v_b, q_i, k_i, k_i, _pad_cols(w_i, LANE)], axis=1)
    assert w_in_p.shape[1] == C_END

    d_lora = p["mla_w_uq"].shape[1]
    w_uq = p["mla_w_uq"][l].reshape(d_lora, MLA_HEADS, MLA_QK)
    w_uq = jnp.pad(w_uq, ((0, 0), (0, 0), (0, LANE - MLA_QK))).reshape(d_lora, MLA_HEADS * LANE)
    w_ukv = p["mla_w_ukv"][l].reshape(MLA_KV_LORA, MLA_HEADS, MLA_NOPE + MLA_V)
    w_nope = jnp.pad(w_ukv[:, :, :MLA_NOPE], ((0, 0), (0, 0), (0, LANE - MLA_NOPE))).reshape(MLA_KV_LORA, MLA_HEADS * LANE)
    w_v = w_ukv[:, :, MLA_NOPE:].reshape(MLA_KV_LORA, MLA_WIDTH)
    w_out = p["w_out"][l]

    row = lambda g: g[l][None, :].astype(F32)
    pad96 = lambda g: jnp.pad(g[l].astype(F32), (0, LANE - MLA_QK))[None, :]
    twice = lambda g: jnp.tile(g[l].astype(F32), 2)[None, :]
    lw = {
        "w_in": w_in_p.astype(BF16), "w_uq": w_uq.astype(BF16), "w_ukv_nope": w_nope.astype(BF16),
        "w_ukv_v": w_v.astype(BF16), "w_out_a": w_out[:MLA_WIDTH].astype(BF16), "w_out_b": w_out[MLA_WIDTH:].astype(BF16),
        "mix_norm": row(p["mix_norm"]), "mla_q_norm": row(p["mla_q_norm"]), "mla_kv_norm": row(p["mla_kv_norm"]),
        "mla_q_gain": pad96(p["mla_q_gain"]), "mla_k_gain": pad96(p["mla_k_gain"]),
        "dsa_q_gain": twice(p["dsa_q_gain"]), "dsa_k_gain": twice(p["dsa_k_gain"]),
    }
    for f in ("ffn1", "ffn2"):
        lw[f + "_norm"] = row(p[f + "_norm"])
        for w in ("w_gate", "w_up", "w_down"):
            lw[f + "_" + w] = p[f + "_" + w][l].astype(BF16)
    return lw


def _rope_tables(pos, rows):
    def cs(rot):
        inv = 1.0 / (ROPE_THETA ** (jnp.arange(0, rot, 2, dtype=F32) / rot))
        ang = pos.astype(F32)[:, None] * inv[None, :]
        return jnp.cos(ang), jnp.sin(ang)

    t = pos.shape[0]
    cos_a, sin_a = cs(MLA_ROPE)
    ones = lambda w: jnp.ones((t, w), F32)
    zeros = lambda w: jnp.zeros((t, w), F32)
    ca = jnp.concatenate([ones(MLA_NOPE), cos_a, cos_a, ones(LANE - MLA_QK)], axis=1)
    sa = jnp.concatenate([zeros(MLA_NOPE), -sin_a, sin_a, zeros(LANE - MLA_QK)], axis=1)
    cos_b, sin_b = cs(DSA_ROT)
    cb = jnp.tile(jnp.concatenate([cos_b, cos_b, ones(DSA_HD - DSA_ROT)], axis=1), (1, 2))
    sb = jnp.tile(jnp.concatenate([-sin_b, sin_b, zeros(DSA_HD - DSA_ROT)], axis=1), (1, 2))
    reps = max(1, rows // t)
    return tuple(jnp.tile(x, (reps, 1)) for x in (ca, sa, cb, sb))


def _pad_keys(x, s_pad):
    return jnp.pad(x, ((0, 0), (0, s_pad - x.shape[1]), (0, 0)))


def _trunk_layer(x, lw, tables, past, *, b, t, q_off, tq_mla, tk_mla, tq_dsa, tk_dsa):
    h = _ffn(x, lw["ffn1_norm"], lw["ffn1_w_gate"], lw["ffn1_w_up"], lw["ffn1_w_down"])
    (ckv, krope, kslab, kb, vb, ki, qa, qb, kb16, vb16, qi, ik2, wi) = _proj(h, lw, tables, t)
    per_batch = lambda a: a.reshape(b, t, a.shape[-1])
    if past is None:
        ckv_all, kslab_all = ckv, kslab
        kb_all, vb_all, ik2_all = per_batch(kb16), per_batch(vb16), per_batch(ik2)
        s_real = t
    else:
        p_ckv, p_krope, p_kb, p_vb, p_ki = past
        s_real = p_ckv.shape[1] + t
        cat = lambda old, new: jnp.concatenate([old, per_batch(new)], axis=1)
        ckv_all = cat(p_ckv, ckv).reshape(b * s_real, MLA_KV_LORA)
        p_kslab = jnp.pad(p_krope, ((0, 0), (0, 0), (MLA_NOPE, LANE - MLA_QK)))
        kslab_all = cat(p_kslab, kslab).reshape(b * s_real, LANE)
        kb_all = cat(p_kb.reshape(b, -1, DSA_WIDTH).astype(BF16), kb16)
        vb_all = cat(p_vb.reshape(b, -1, DSA_WIDTH).astype(BF16), vb16)
        ik2_all = cat(jnp.tile(p_ki, (1, 1, 2)).astype(BF16), ik2)
    ka, va = _mla_kv(ckv_all, kslab_all, lw)
    s_pad_a = pl.cdiv(s_real, tk_mla) * tk_mla
    ka = _pad_keys(ka.reshape(b, s_real, -1), s_pad_a)
    va = _pad_keys(va.reshape(b, s_real, -1), s_pad_a)
    oa = _mla_attn(per_batch(qa), ka, va, q_off=q_off, s_real=s_real, tq=tq_mla, tk=tk_mla)
    s_pad_b = pl.cdiv(s_real, tk_dsa) * tk_dsa
    ob = _dsa_attn(per_batch(qb), per_batch(qi), per_batch(wi), _pad_keys(kb_all, s_pad_b), _pad_keys(vb_all, s_pad_b),
                   _pad_keys(ik2_all, s_pad_b), q_off=q_off, s_real=s_real, tq=tq_dsa, tk=tk_dsa)
    y = _ffn(h, lw["ffn2_norm"], lw["ffn2_w_gate"], lw["ffn2_w_up"], lw["ffn2_w_down"],
             attn=(oa.reshape(b * t, MLA_WIDTH), ob.reshape(b * t, DSA_WIDTH), lw["w_out_a"], lw["w_out_b"]))
    rows = (per_batch(ckv), per_batch(krope), per_batch(kb).reshape(b, t, DSA_HEADS, DSA_HD),
            per_batch(vb).reshape(b, t, DSA_HEADS, DSA_HD), per_batch(ki))
    return y, rows


def kernel(x_prompt, x_sample, cache_mla_ckv, cache_mla_krope, cache_dsa_k, cache_dsa_v, cache_idx_k,
           ffn1_norm, ffn1_w_gate, ffn1_w_up, ffn1_w_down, mix_norm, w_in,
           mla_q_norm, mla_w_uq, mla_kv_norm, mla_w_ukv, mla_q_gain, mla_k_gain,
           dsa_q_gain, dsa_k_gain, w_out, ffn2_norm, ffn2_w_gate, ffn2_w_up, ffn2_w_down):
    params = dict(ffn1_norm=ffn1_norm, ffn1_w_gate=ffn1_w_gate, ffn1_w_up=ffn1_w_up, ffn1_w_down=ffn1_w_down,
                  mix_norm=mix_norm, w_in=w_in, mla_q_norm=mla_q_norm, mla_w_uq=mla_w_uq, mla_kv_norm=mla_kv_norm,
                  mla_w_ukv=mla_w_ukv, mla_q_gain=mla_q_gain, mla_k_gain=mla_k_gain, dsa_q_gain=dsa_q_gain,
                  dsa_k_gain=dsa_k_gain, w_out=w_out, ffn2_norm=ffn2_norm, ffn2_w_gate=ffn2_w_gate,
                  ffn2_w_up=ffn2_w_up, ffn2_w_down=ffn2_w_down)
    depth = w_in.shape[0]
    d_model = x_prompt.shape[-1]
    weights = [_layer_weights(params, l) for l in range(depth)]

    b_p, t_p = x_prompt.shape[:2]
    n_p = b_p * t_p
    tabs_p = _rope_tables(jnp.arange(t_p, dtype=I32), _row_tile(n_p, 512))
    tile_p = dict(tq_mla=min(t_p, 512), tk_mla=min(t_p, 512), tq_dsa=min(t_p, 256), tk_dsa=min(t_p, 512))
    h_p = x_prompt.reshape(n_p, d_model)
    p_rows = []
    for l in range(depth):
        h_p, rows = _trunk_layer(h_p, weights[l], tabs_p, None, b=b_p, t=t_p, q_off=0, **tile_p)
        p_rows.append(rows)

    b_s, t_s = x_sample.shape[:2]
    n_s = b_s * t_s
    past_len = cache_mla_ckv.shape[2]
    tabs_s = _rope_tables(past_len + jnp.arange(t_s, dtype=I32), _row_tile(n_s, 512))
    tile_s = dict(tq_mla=t_s, tk_mla=LANE, tq_dsa=t_s, tk_dsa=2 * LANE)
    h_s = x_sample.reshape(n_s, d_model)
    s_rows = []
    for l in range(depth):
        past = (cache_mla_ckv[l], cache_mla_krope[l], cache_dsa_k[l], cache_dsa_v[l], cache_idx_k[l])
        h_s, rows = _trunk_layer(h_s, weights[l], tabs_s, past, b=b_s, t=t_s, q_off=past_len, **tile_s)
        s_rows.append(rows)

    stack = lambda rows_by_layer, i: jnp.stack([r[i] for r in rows_by_layer])
    return (h_p.reshape(b_p, t_p, d_model), h_s.reshape(b_s, t_s, d_model),
            *[stack(p_rows, i) for i in range(5)], *[stack(s_rows, i) for i in range(5)])
```

```python
import functools

import jax
import jax.numpy as jnp
from jax import lax
from jax.experimental import pallas as pl
from jax.experimental.pallas import tpu as pltpu

F32 = jnp.float32
BF16 = jnp.bfloat16
I32 = jnp.int32

CHUNK_SHIFT = 6
ROPE_THETA = 500000.0
EPS = 1e-6
MLA_HEADS = 8
MLA_NOPE = 64
MLA_ROPE = 32
MLA_QK = MLA_NOPE + MLA_ROPE
MLA_V = 64
MLA_Q_LORA = 256
MLA_KV_LORA = 128
DSA_HEADS = 8
DSA_HD = 64
DSA_ROT = 16
IDX_HEADS = 4
IDX_HD = 64
IDX_W_SCALE = (IDX_HD * IDX_HEADS) ** -0.5
TOPK_MAX = 256
DSA_WIDTH = DSA_HEADS * DSA_HD
MLA_WIDTH = MLA_HEADS * MLA_V

LANE = 128
VMEM_LIMIT = 56 * 1024 * 1024

NEG = -1e30
LOG2E = 1.4426950408889634
INT_MIN = -(2 ** 31)
INT_MAX = 2 ** 31 - 1
KEY_NEG_INF = INT_MIN + 0x7FFFFF


def _nt_dot(a, b):
    return lax.dot_general(a, b, (((1,), (1,)), ((), ())), preferred_element_type=F32)


def _dot(a, b):
    return jnp.dot(a, b, preferred_element_type=F32)


def _rms(x, g):
    return x * lax.rsqrt(jnp.mean(x * x, axis=-1, keepdims=True) + EPS) * g


def _lane_iota(shape):
    return lax.broadcasted_iota(I32, shape, len(shape) - 1)


FFN_CHUNK = 256


def _ffn_body(x, g_ref, wg_ref, wu_ref, wd_ref, o_ref):
    xb = _rms(x, g_ref[...]).astype(BF16)
    d_ff = wg_ref.shape[1]
    acc = jnp.zeros(x.shape, F32)
    for c in range(d_ff // FFN_CHUNK):
        sl = slice(c * FFN_CHUNK, (c + 1) * FFN_CHUNK)
        gate = _dot(xb, wg_ref[:, sl])
        up = _dot(xb, wu_ref[:, sl])
        act = (gate * jax.nn.sigmoid(gate) * up).astype(BF16)
        acc = acc + _dot(act, wd_ref[sl, :])
    o_ref[...] = x + 0.5 * acc


def _ffn_kernel(x_ref, g_ref, wg_ref, wu_ref, wd_ref, o_ref):
    _ffn_body(x_ref[...], g_ref, wg_ref, wu_ref, wd_ref, o_ref)


def _out_ffn_kernel(h_ref, oa_ref, ob_ref, woa_ref, wob_ref, g_ref, wg_ref, wu_ref, wd_ref, o_ref):
    x = h_ref[...] + _dot(oa_ref[...], woa_ref[...]) + _dot(ob_ref[...], wob_ref[...])
    _ffn_body(x, g_ref, wg_ref, wu_ref, wd_ref, o_ref)


def _resident(shape):
    nd = len(shape)
    return pl.BlockSpec(shape, lambda *_: (0,) * nd, pipeline_mode=pl.Buffered(1))


def _row_tile(n, pref):
    for t in range(min(n, pref), 0, -16):
        if n % t == 0:
            return t
    raise ValueError(f"no row tile for {n} rows")


def _ffn(x, g, wg, wu, wd, attn=None):
    n, d = x.shape
    tm = _row_tile(n, 512)
    row = lambda w: pl.BlockSpec((tm, w), lambda i: (i, 0))
    w_specs = [_resident(g.shape), _resident(wg.shape), _resident(wu.shape), _resident(wd.shape)]
    params = pltpu.CompilerParams(dimension_semantics=("parallel",), vmem_limit_bytes=VMEM_LIMIT)
    out_shape = jax.ShapeDtypeStruct((n, d), F32)
    if attn is None:
        return pl.pallas_call(_ffn_kernel, grid=(n // tm,), in_specs=[row(d)] + w_specs, out_specs=row(d),
                              out_shape=out_shape, compiler_params=params, name="ffn")(x, g, wg, wu, wd)
    oa, ob, woa, wob = attn
    return pl.pallas_call(
        _out_ffn_kernel, grid=(n // tm,),
        in_specs=[row(d), row(oa.shape[1]), row(ob.shape[1]), _resident(woa.shape), _resident(wob.shape)] + w_specs,
        out_specs=row(d), out_shape=out_shape, compiler_params=params, name="out_ffn")(x, oa, ob, woa, wob, g, wg, wu, wd)


C_CQ = 0
C_CKV = C_CQ + MLA_Q_LORA
C_KR = C_CKV + MLA_KV_LORA
C_QB = C_KR + LANE
C_KB = C_QB + DSA_WIDTH
C_VB = C_KB + DSA_WIDTH
C_QI = C_VB + DSA_WIDTH
C_KI = C_QI + IDX_HEADS * IDX_HD
C_WI = C_KI + LANE
C_END = C_WI + LANE


def _rope_a(x, c, s):
    lane = _lane_iota(x.shape)
    partner = jnp.where(lane < MLA_NOPE + MLA_ROPE // 2, pltpu.roll(x, LANE - MLA_ROPE // 2, 1),
                        pltpu.roll(x, MLA_ROPE // 2, 1))
    return x * c + partner * s


def _rope_b(x, c, s):
    lane = _lane_iota(x.shape)
    half = DSA_ROT // 2
    partner = jnp.where((lane & (DSA_HD - 1)) < half, pltpu.roll(x, LANE - half, 1), pltpu.roll(x, half, 1))
    return x * c + partner * s


def _head96_norm(x, g):
    ms = jnp.sum(x * x, axis=-1, keepdims=True) * (1.0 / MLA_QK)
    return x * lax.rsqrt(ms + EPS) * g


def _head64_norm(x, g2):
    lane = _lane_iota(x.shape)
    lo = lane < DSA_HD
    sq = x * x
    s_lo = jnp.sum(jnp.where(lo, sq, 0.0), axis=-1, keepdims=True)
    s_hi = jnp.sum(jnp.where(lo, 0.0, sq), axis=-1, keepdims=True)
    ms = jnp.where(lo, s_lo, s_hi) * (1.0 / DSA_HD)
    return x * lax.rsqrt(ms + EPS) * g2


def _proj_kernel(h_ref, gmix_ref, win_ref, gq_ref, wuq_ref, gkv_ref, gqa_ref, gqb_ref, gkb_ref,
                 ca_ref, sa_ref, cb_ref, sb_ref,
                 ckv_ref, krope_ref, kslab_ref, kb_ref, vb_ref, ki_ref,
                 qa_ref, qb_ref, kb16_ref, vb16_ref, qi_ref, ik2_ref, wi_ref):
    u = _rms(h_ref[...], gmix_ref[...]).astype(BF16)
    ca, sa, cb, sb = ca_ref[...], sa_ref[...], cb_ref[...], sb_ref[...]

    def cols(start, width):
        return _dot(u, win_ref[:, start:start + width])

    cq = _rms(cols(C_CQ, MLA_Q_LORA), gq_ref[...]).astype(BF16)
    qa = _dot(cq, wuq_ref[...])
    qa_scale = MLA_QK ** -0.5 * LOG2E
    for h in range(MLA_HEADS):
        sl = slice(h * LANE, (h + 1) * LANE)
        qa_ref[:, sl] = (_head96_norm(_rope_a(qa[:, sl], ca, sa), gqa_ref[...]) * qa_scale).astype(BF16)

    ckv_ref[...] = _rms(cols(C_CKV, MLA_KV_LORA), gkv_ref[...])
    kslab = _rope_a(cols(C_KR, LANE), ca, sa)
    kslab_ref[...] = kslab
    krope_ref[...] = kslab[:, MLA_NOPE:MLA_NOPE + MLA_ROPE]

    qb = cols(C_QB, DSA_WIDTH)
    kb = cols(C_KB, DSA_WIDTH)
    qb_scale = DSA_HD ** -0.5 * LOG2E
    for p in range(DSA_WIDTH // LANE):
        sl = slice(p * LANE, (p + 1) * LANE)
        qb_ref[:, sl] = (_rope_b(_head64_norm(qb[:, sl], gqb_ref[...]), cb, sb) * qb_scale).astype(BF16)
        kp = _rope_b(_head64_norm(kb[:, sl], gkb_ref[...]), cb, sb)
        kb_ref[:, sl] = kp
        kb16_ref[:, sl] = kp.astype(BF16)
    vb = cols(C_VB, DSA_WIDTH)
    vb_ref[...] = vb
    vb16_ref[...] = vb.astype(BF16)

    qi = cols(C_QI, IDX_HEADS * IDX_HD)
    for p in range(IDX_HEADS * IDX_HD // LANE):
        sl = slice(p * LANE, (p + 1) * LANE)
        qi_ref[:, sl] = _rope_b(qi[:, sl], cb, sb).astype(BF16)
    ik2 = _rope_b(cols(C_KI, LANE), cb, sb)
    ki_ref[...] = ik2[:, :IDX_HD]
    ik2_ref[...] = ik2.astype(BF16)
    wi_ref[...] = cols(C_WI, LANE) * IDX_W_SCALE


def _proj(h, lw, tables, t_seq):
    n, d = h.shape
    tm = _row_tile(n, 512)
    ca, sa, cb, sb = tables
    n_tab = ca.shape[0] // tm
    row = lambda w: pl.BlockSpec((tm, w), lambda i: (i, 0))
    tab = pl.BlockSpec((tm, LANE), lambda i: (i % n_tab, 0))
    consts = [lw["mix_norm"], lw["w_in"], lw["mla_q_norm"], lw["w_uq"], lw["mla_kv_norm"],
              lw["mla_q_gain"], lw["dsa_q_gain"], lw["dsa_k_gain"]]
    out_widths = [(MLA_KV_LORA, F32), (MLA_ROPE, F32), (LANE, F32), (DSA_WIDTH, F32), (DSA_WIDTH, F32), (IDX_HD, F32),
                  (MLA_HEADS * LANE, BF16), (DSA_WIDTH, BF16), (DSA_WIDTH, BF16), (DSA_WIDTH, BF16),
                  (IDX_HEADS * IDX_HD, BF16), (LANE, BF16), (LANE, F32)]
    return pl.pallas_call(
        _proj_kernel, grid=(n // tm,),
        in_specs=[row(d)] + [_resident(c.shape) for c in consts] + [tab] * 4,
        out_specs=[row(w) for w, _ in out_widths],
        out_shape=[jax.ShapeDtypeStruct((n, w), dt) for w, dt in out_widths],
        compiler_params=pltpu.CompilerParams(dimension_semantics=("parallel",), vmem_limit_bytes=VMEM_LIMIT),
        name="proj")(h, *consts, ca, sa, cb, sb)


def _mla_kv_kernel(ckv_ref, kslab_ref, wn_ref, wv_ref, gk_ref, ka_ref, va_ref):
    c = ckv_ref[...].astype(BF16)
    kn = _dot(c, wn_ref[...])
    kslab = kslab_ref[...]
    for h in range(MLA_HEADS):
        sl = slice(h * LANE, (h + 1) * LANE)
        ka_ref[:, sl] = _head96_norm(kn[:, sl] + kslab, gk_ref[...]).astype(BF16)
    va_ref[...] = _dot(c, wv_ref[...]).astype(BF16)


def _mla_kv(ckv, kslab, lw):
    m = ckv.shape[0]
    tm = _row_tile(m, 512)
    row = lambda w: pl.BlockSpec((tm, w), lambda i: (i, 0))
    consts = [lw["w_ukv_nope"], lw["w_ukv_v"], lw["mla_k_gain"]]
    return pl.pallas_call(
        _mla_kv_kernel, grid=(m // tm,),
        in_specs=[row(MLA_KV_LORA), row(LANE)] + [_resident(c.shape) for c in consts],
        out_specs=[row(MLA_HEADS * LANE), row(MLA_WIDTH)],
        out_shape=[jax.ShapeDtypeStruct((m, MLA_HEADS * LANE), BF16), jax.ShapeDtypeStruct((m, MLA_WIDTH), BF16)],
        compiler_params=pltpu.CompilerParams(dimension_semantics=("parallel",), vmem_limit_bytes=VMEM_LIMIT),
        name="mla_kv")(ckv, kslab, *consts)


def _visible(q0, k0, tq, tk, s_real):
    q_chunk = (q0 + lax.broadcasted_iota(I32, (tq, tk), 0)) >> CHUNK_SHIFT
    k_idx = k0 + lax.broadcasted_iota(I32, (tq, tk), 1)
    return ((k_idx >> CHUNK_SHIFT) <= q_chunk) & (k_idx < s_real)


def _softmax_update(s, v_pair, m, l, acc):
    m_new = jnp.maximum(m, jnp.max(s, axis=-1, keepdims=True))
    alpha = jnp.exp2(m - m_new)
    p = jnp.exp2(s - m_new)
    l_new = alpha * l + jnp.sum(p, axis=-1, keepdims=True)
    return m_new, l_new, alpha * acc + _dot(p.astype(BF16), v_pair)


def _write_pairs(o_ref, ls, accs):
    lane = _lane_iota(accs[0].shape)
    for p in range(len(accs) // 2):
        even = accs[2 * p] / ls[2 * p]
        odd = accs[2 * p + 1] / ls[2 * p + 1]
        o_ref[0, :, p * LANE:(p + 1) * LANE] = jnp.where(lane < LANE // 2, even, odd).astype(o_ref.dtype)


def _mla_attn_kernel(q_ref, k_ref, v_ref, o_ref, m_ref, l_ref, acc_ref, *, tq, tk, q_off, s_real):
    qi, ki = pl.program_id(1), pl.program_id(2)
    q0 = q_off + qi * tq
    k0 = ki * tk

    heads = range(MLA_HEADS)

    @pl.when(ki == 0)
    def _():
        m_ref[...] = jnp.full(m_ref.shape, NEG, F32)
        l_ref[...] = jnp.zeros(l_ref.shape, F32)
        acc_ref[...] = jnp.zeros(acc_ref.shape, F32)

    def attend(masked):
        scores = [_nt_dot(q_ref[0, :, h * LANE:(h + 1) * LANE], k_ref[0, :, h * LANE:(h + 1) * LANE]) for h in heads]
        if masked:
            vis = _visible(q0, k0, tq, tk, s_real)
            scores = [jnp.where(vis, s, NEG) for s in scores]
        new = [_softmax_update(scores[h], v_ref[0, :, (h // 2) * LANE:(h // 2 + 1) * LANE],
                               m_ref[h], l_ref[h], acc_ref[h]) for h in heads]
        for h in heads:
            m_ref[h], l_ref[h], acc_ref[h] = new[h]

    needed = (k0 >> CHUNK_SHIFT) <= ((q0 + tq - 1) >> CHUNK_SHIFT)
    unmasked = (((k0 + tk - 1) >> CHUNK_SHIFT) <= (q0 >> CHUNK_SHIFT)) & (k0 + tk <= s_real)
    pl.when(needed & unmasked)(lambda: attend(False))
    pl.when(needed & ~unmasked)(lambda: attend(True))

    @pl.when(ki == pl.num_programs(2) - 1)
    def _():
        _write_pairs(o_ref, [l_ref[h] for h in heads], [acc_ref[h] for h in heads])


def _mla_attn(qa, ka, va, *, q_off, s_real, tq, tk):
    b, t, _ = qa.shape
    s_pad = ka.shape[1]
    nq, nk = t // tq, s_pad // tk

    def kv_map(bi, qi, ki):
        last = ((((q_off + qi * tq + tq - 1) >> CHUNK_SHIFT) + 1) << CHUNK_SHIFT) - 1
        return (bi, jnp.minimum(ki, jnp.minimum(last // tk, nk - 1)), 0)

    kern = functools.partial(_mla_attn_kernel, tq=tq, tk=tk, q_off=q_off, s_real=s_real)
    return pl.pallas_call(
        kern, grid=(b, nq, nk),
        in_specs=[pl.BlockSpec((1, tq, MLA_HEADS * LANE), lambda bi, qi, ki: (bi, qi, 0)),
                  pl.BlockSpec((1, tk, MLA_HEADS * LANE), kv_map),
                  pl.BlockSpec((1, tk, MLA_WIDTH), kv_map)],
        out_specs=pl.BlockSpec((1, tq, MLA_WIDTH), lambda bi, qi, ki: (bi, qi, 0)),
        out_shape=jax.ShapeDtypeStruct((b, t, MLA_WIDTH), BF16),
        scratch_shapes=[pltpu.VMEM((MLA_HEADS, tq, 1), F32), pltpu.VMEM((MLA_HEADS, tq, 1), F32),
                        pltpu.VMEM((MLA_HEADS, tq, LANE), F32)],
        compiler_params=pltpu.CompilerParams(dimension_semantics=("parallel", "parallel", "arbitrary"),
                                             vmem_limit_bytes=VMEM_LIMIT),
        name="mla_attn")(qa, ka, va)


FALSE_POSITION_PROBES = 24
MAX_PROBES = 64


SUBLANES = 8


FOLD_CHAINS = 4


def _fold_rows(x, reduce):
    groups = x.shape[0] // SUBLANES
    if groups % FOLD_CHAINS == 0 and groups > FOLD_CHAINS:
        x = reduce(x.reshape(groups // FOLD_CHAINS, FOLD_CHAINS * SUBLANES, x.shape[1]), axis=0)
        groups = FOLD_CHAINS
    return reduce(x.reshape(groups, SUBLANES, x.shape[1]), axis=0)


def _softmax_update_t(s, v_t, m, l, acc):
    m_new = jnp.maximum(m, jnp.max(_fold_rows(s, jnp.max), axis=0, keepdims=True))
    alpha = jnp.exp2(m - m_new)
    p = jnp.exp2(s - m_new)
    l_new = alpha * l + jnp.sum(_fold_rows(p, jnp.sum), axis=0, keepdims=True)
    return m_new, l_new, alpha * acc + _dot(v_t, p.astype(BF16))


def _sortable(bits):
    return bits ^ ((bits >> 31) & INT_MAX)


def _dsa_kernel(qb_ref, qi_ref, wi_ref, kb_ref, vt_ref, ik2_ref, tri_ref, o_ref,
                key_ref, qm_ref, qim_ref, *, tq, tk, q_off, s_real, topk):
    q0 = q_off + pl.program_id(1) * tq
    vis_end = jnp.minimum((((q0 + tq - 1) >> CHUNK_SHIFT) + 1) << CHUNK_SHIFT, s_real)
    n_vis = (vis_end + tk - 1) // tk
    lane = _lane_iota((tq, LANE))
    lo = lane < LANE // 2

    for h in range(DSA_HEADS):
        pair = qb_ref[0, :, (h // 2) * LANE:(h // 2 + 1) * LANE]
        qm_ref[h] = jnp.where(lo if h % 2 == 0 else ~lo, pair, jnp.zeros_like(pair))
    for h in range(IDX_HEADS):
        pair = qi_ref[0, :, (h // 2) * LANE:(h // 2 + 1) * LANE]
        qim_ref[h] = jnp.where(lo if h % 2 == 0 else ~lo, pair, jnp.zeros_like(pair))
    w_rows = [wi_ref[0, h:h + 1, :] for h in range(IDX_HEADS)]

    def visible_t(k0):
        q_chunk = (q0 + lax.broadcasted_iota(I32, (tk, tq), 1)) >> CHUNK_SHIFT
        k_idx = k0 + lax.broadcasted_iota(I32, (tk, tq), 0)
        return ((k_idx >> CHUNK_SHIFT) <= q_chunk) & (k_idx < s_real)

    def score_block(kb, carry, masked):
        q_max, q_min = carry
        k0 = pl.multiple_of(kb * tk, tk)
        ik = ik2_ref[0, pl.ds(k0, tk), :]
        score = jnp.zeros((tk, tq), F32)
        for h in range(IDX_HEADS):
            score = score + jnp.maximum(_nt_dot(ik, qim_ref[h]), 0.0) * w_rows[h]
        score = jnp.where(score == 0.0, 0.0, score)
        below = above = score
        if masked:
            vis = visible_t(k0)
            below, above = jnp.where(vis, score, -jnp.inf), jnp.where(vis, score, jnp.inf)
        key_ref[kb] = _sortable(pltpu.bitcast(below, I32))
        return (jnp.maximum(q_max, _fold_rows(below, jnp.max)), jnp.minimum(q_min, _fold_rows(above, jnp.min)))

    n_full = jnp.minimum(((q0 >> CHUNK_SHIFT) + 1) << CHUNK_SHIFT, s_real) // tk
    carry = (jnp.full((SUBLANES, tq), -jnp.inf, F32), jnp.full((SUBLANES, tq), jnp.inf, F32))
    carry = lax.fori_loop(0, n_full, functools.partial(score_block, masked=False), carry)
    q_max, q_min = lax.fori_loop(n_full, n_vis, functools.partial(score_block, masked=True), carry)
    row_max = jnp.max(q_max, axis=0, keepdims=True)
    row_min = jnp.min(q_min, axis=0, keepdims=True)

    def count_ge(t):
        def body(kb, acc):
            return acc + _fold_rows(jnp.where(key_ref[kb] >= t, 1.0, 0.0), jnp.sum)
        return jnp.sum(lax.fori_loop(0, n_vis, body, jnp.zeros((SUBLANES, tq), F32)), axis=0, keepdims=True)

    kf = float(topk)
    q_pos = q0 + lax.broadcasted_iota(I32, (1, tq), 1)
    n_row = jnp.minimum(((q_pos >> CHUNK_SHIFT) + 1) << CHUNK_SHIFT, s_real).astype(F32)
    few = n_row < kf
    lo0 = _sortable(lax.bitcast_convert_type(row_min, I32))
    hi0 = _sortable(lax.bitcast_convert_type(row_max, I32)) + 1

    def finished(lo_k, hi_k, c_lo):
        return few | (c_lo == kf) | (hi_k == lo_k + 1)

    def probe_step(carry):
        it, _, lo_k, hi_k, c_lo, c_hi, g_lo, g_hi, last = carry
        v_lo = lax.bitcast_convert_type(_sortable(lo_k), F32)
        v_hi = lax.bitcast_convert_type(_sortable(hi_k), F32)
        a = jnp.log(c_lo * (1.0 / kf)) * g_lo
        b = jnp.log(kf / jnp.maximum(c_hi, 0.5)) * g_hi
        p = _sortable(lax.bitcast_convert_type(v_lo + (v_hi - v_lo) * (a / (a + b)), I32))
        p = jnp.where(it >= FALSE_POSITION_PROBES, (lo_k >> 1) + (hi_k >> 1) + (lo_k & hi_k & 1), p)
        p = jnp.where((it == 0) & (lo_k < 0) & (hi_k > 0), 0, p)
        p = jnp.where(lo_k == 0, 1, p)
        p = jnp.minimum(jnp.maximum(p, lo_k + 1), hi_k - 1)
        c = count_ge(p)
        open_ = ~finished(lo_k, hi_k, c_lo)
        up = open_ & (c >= kf)
        down = open_ & (c < kf)
        lo_k, c_lo = jnp.where(up, p, lo_k), jnp.where(up, c, c_lo)
        hi_k, c_hi = jnp.where(down, p, hi_k), jnp.where(down, c, c_hi)
        g_lo = jnp.where(down, jnp.where(last < 0.0, 0.5 * g_lo, 1.0), jnp.where(up, 1.0, g_lo))
        g_hi = jnp.where(up, jnp.where(last > 0.0, 0.5 * g_hi, 1.0), jnp.where(down, 1.0, g_hi))
        last = jnp.where(up, 1.0, jnp.where(down, -1.0, last))
        n_open = jnp.max(jnp.where(finished(lo_k, hi_k, c_lo), 0, 1))
        return it + 1, n_open, lo_k, hi_k, c_lo, c_hi, g_lo, g_hi, last

    ones = jnp.ones((1, tq), F32)
    init = (jnp.int32(0), jnp.max(jnp.where(finished(lo0, hi0, n_row), 0, 1)), lo0, hi0, n_row, 0.0 * ones,
            ones, ones, 0.0 * ones)
    final = lax.while_loop(lambda c: (c[1] > 0) & (c[0] < MAX_PROBES), probe_step, init)
    t, c_lo, c_hi = final[2], final[4], final[5]

    t = jnp.where(few, KEY_NEG_INF, t)
    need = jnp.where(few, 0.0, jnp.where(c_lo == kf, kf, kf - c_hi))

    heads = range(DSA_HEADS)

    def attend_block(kb, carry):
        tied_before, ms, ls, accs = carry
        k0 = pl.multiple_of(kb * tk, tk)
        blk = key_ref[kb]
        tied = jnp.where(blk == t, 1.0, 0.0)
        tied_rank = (tied_before + _dot(tri_ref[...], tied.astype(BF16))) * tied
        sel = (blk >= t) & (tied_rank <= need)
        k_pairs = [kb_ref[0, pl.ds(k0, tk), p * LANE:(p + 1) * LANE] for p in range(DSA_HEADS // 2)]
        v_pairs = [vt_ref[0, kb, p * LANE:(p + 1) * LANE, :] for p in range(DSA_HEADS // 2)]
        scores = [_nt_dot(k_pairs[h // 2], qm_ref[h]) for h in heads]
        new = [_softmax_update_t(jnp.where(sel, scores[h], NEG), v_pairs[h // 2], ms[h], ls[h], accs[h]) for h in heads]
        return (tied_before + jnp.sum(_fold_rows(tied, jnp.sum), axis=0, keepdims=True),
                tuple(n[0] for n in new), tuple(n[1] for n in new), tuple(n[2] for n in new))

    row = lambda v: tuple(jnp.full((1, tq), v, F32) for _ in heads)
    init = (jnp.zeros((1, tq), F32), row(NEG), row(0.0), tuple(jnp.zeros((LANE, tq), F32) for _ in heads))
    _, _, ls, accs = lax.fori_loop(0, n_vis, attend_block, init)
    upper = lax.broadcasted_iota(I32, (LANE, tq), 0) < LANE // 2
    for p in range(DSA_HEADS // 2):
        o_ref[0, p * LANE:(p + 1) * LANE, :] = jnp.where(upper, accs[2 * p] / ls[2 * p],
                                                         accs[2 * p + 1] / ls[2 * p + 1]).astype(o_ref.dtype)


def _dsa_attn(qb, qi, wi, kb, vb, ik2, *, q_off, s_real, tq, tk):
    b, t, _ = qb.shape
    s_pad = kb.shape[1]
    n_kb = s_pad // tk
    topk = min(TOPK_MAX, s_real // 4)
    t_pad = pl.cdiv(t, tq) * tq
    if t_pad != t:
        qb, qi, wi = (jnp.pad(a, ((0, 0), (0, t_pad - t), (0, 0))) for a in (qb, qi, wi))
    wi_t = jnp.swapaxes(wi[:, :, :SUBLANES], 1, 2)
    v_t = jnp.swapaxes(vb.reshape(b, n_kb, tk, DSA_WIDTH), 2, 3)
    tri = (lax.broadcasted_iota(I32, (tk, tk), 0) >= lax.broadcasted_iota(I32, (tk, tk), 1)).astype(BF16)
    qspec = lambda w: pl.BlockSpec((1, tq, w), lambda bi, i: (bi, i, 0))
    kspec = lambda w: pl.BlockSpec((1, s_pad, w), lambda bi, i: (bi, 0, 0), pipeline_mode=pl.Buffered(1))
    kern = functools.partial(_dsa_kernel, tq=tq, tk=tk, q_off=q_off, s_real=s_real, topk=topk)
    o_t = pl.pallas_call(
        kern, grid=(b, t_pad // tq),
        in_specs=[qspec(DSA_WIDTH), qspec(IDX_HEADS * IDX_HD),
                  pl.BlockSpec((1, SUBLANES, tq), lambda bi, i: (bi, 0, i)),
                  kspec(DSA_WIDTH),
                  pl.BlockSpec((1, n_kb, DSA_WIDTH, tk), lambda bi, i: (bi, 0, 0, 0), pipeline_mode=pl.Buffered(1)),
                  kspec(LANE), _resident((tk, tk))],
        out_specs=pl.BlockSpec((1, DSA_WIDTH, tq), lambda bi, i: (bi, 0, i)),
        out_shape=jax.ShapeDtypeStruct((b, DSA_WIDTH, t_pad), BF16),
        scratch_shapes=[pltpu.VMEM((n_kb, tk, tq), I32),
                        pltpu.VMEM((DSA_HEADS, tq, LANE), BF16), pltpu.VMEM((IDX_HEADS, tq, LANE), BF16)],
        compiler_params=pltpu.CompilerParams(dimension_semantics=("parallel", "arbitrary"),
                                             vmem_limit_bytes=VMEM_LIMIT),
        name="dsa_attn")(qb, qi, wi_t, kb, v_t, ik2, tri)
    return jnp.swapaxes(o_t, 1, 2)[:, :t]


def _pad_cols(w, width):
    return jnp.pad(w, ((0, 0), (0, width - w.shape[1])))


def _layer_weights(p, l):
    w_in = p["w_in"][l]
    off, pieces = 0, []
    for n in (MLA_Q_LORA, MLA_KV_LORA, MLA_ROPE, DSA_WIDTH, DSA_WIDTH, DSA_WIDTH, IDX_HEADS * IDX_HD, IDX_HD, IDX_HEADS):
        pieces.append(w_in[:, off:off + n])
        off += n
    c_q, c_kv, k_r, q_b, k_b, v_b, q_i, k_i, w_i = pieces
    k_r = jnp.pad(k_r, ((0, 0), (MLA_NOPE, LANE - MLA_QK)))
    w_in_p = jnp.concatenate([c_q, c_kv, k_r, q_b, k_b, v_b, q_i, k_i, k_i, _pad_cols(w_i, LANE)], axis=1)
    assert w_in_p.shape[1] == C_END

    d_lora = p["mla_w_uq"].shape[1]
    w_uq = p["mla_w_uq"][l].reshape(d_lora, MLA_HEADS, MLA_QK)
    w_uq = jnp.pad(w_uq, ((0, 0), (0, 0), (0, LANE - MLA_QK))).reshape(d_lora, MLA_HEADS * LANE)
    w_ukv = p["mla_w_ukv"][l].reshape(MLA_KV_LORA, MLA_HEADS, MLA_NOPE + MLA_V)
    w_nope = jnp.pad(w_ukv[:, :, :MLA_NOPE], ((0, 0), (0, 0), (0, LANE - MLA_NOPE))).reshape(MLA_KV_LORA, MLA_HEADS * LANE)
    w_v = w_ukv[:, :, MLA_NOPE:].reshape(MLA_KV_LORA, MLA_WIDTH)
    w_out = p["w_out"][l]

    row = lambda g: g[l][None, :].astype(F32)
    pad96 = lambda g: jnp.pad(g[l].astype(F32), (0, LANE - MLA_QK))[None, :]
    twice = lambda g: jnp.tile(g[l].astype(F32), 2)[None, :]
    lw = {
        "w_in": w_in_p.astype(BF16), "w_uq": w_uq.astype(BF16), "w_ukv_nope": w_nope.astype(BF16),
        "w_ukv_v": w_v.astype(BF16), "w_out_a": w_out[:MLA_WIDTH].astype(BF16), "w_out_b": w_out[MLA_WIDTH:].astype(BF16),
        "mix_norm": row(p["mix_norm"]), "mla_q_norm": row(p["mla_q_norm"]), "mla_kv_norm": row(p["mla_kv_norm"]),
        "mla_q_gain": pad96(p["mla_q_gain"]), "mla_k_gain": pad96(p["mla_k_gain"]),
        "dsa_q_gain": twice(p["dsa_q_gain"]), "dsa_k_gain": twice(p["dsa_k_gain"]),
    }
    for f in ("ffn1", "ffn2"):
        lw[f + "_norm"] = row(p[f + "_norm"])
        for w in ("w_gate", "w_up", "w_down"):
            lw[f + "_" + w] = p[f + "_" + w][l].astype(BF16)
    return lw


def _rope_tables(pos, rows):
    def cs(rot):
        inv = 1.0 / (ROPE_THETA ** (jnp.arange(0, rot, 2, dtype=F32) / rot))
        ang = pos.astype(F32)[:, None] * inv[None, :]
        return jnp.cos(ang), jnp.sin(ang)

    t = pos.shape[0]
    cos_a, sin_a = cs(MLA_ROPE)
    ones = lambda w: jnp.ones((t, w), F32)
    zeros = lambda w: jnp.zeros((t, w), F32)
    ca = jnp.concatenate([ones(MLA_NOPE), cos_a, cos_a, ones(LANE - MLA_QK)], axis=1)
    sa = jnp.concatenate([zeros(MLA_NOPE), -sin_a, sin_a, zeros(LANE - MLA_QK)], axis=1)
    cos_b, sin_b = cs(DSA_ROT)
    cb = jnp.tile(jnp.concatenate([cos_b, cos_b, ones(DSA_HD - DSA_ROT)], axis=1), (1, 2))
    sb = jnp.tile(jnp.concatenate([-sin_b, sin_b, zeros(DSA_HD - DSA_ROT)], axis=1), (1, 2))
    reps = max(1, rows // t)
    return tuple(jnp.tile(x, (reps, 1)) for x in (ca, sa, cb, sb))


def _pad_keys(x, s_pad):
    return jnp.pad(x, ((0, 0), (0, s_pad - x.shape[1]), (0, 0)))


def _trunk_layer(x, lw, tables, past, *, b, t, q_off, tq_mla, tk_mla, tq_dsa, tk_dsa):
    h = _ffn(x, lw["ffn1_norm"], lw["ffn1_w_gate"], lw["ffn1_w_up"], lw["ffn1_w_down"])
    (ckv, krope, kslab, kb, vb, ki, qa, qb, kb16, vb16, qi, ik2, wi) = _proj(h, lw, tables, t)
    per_batch = lambda a: a.reshape(b, t, a.shape[-1])
    if past is None:
        ckv_all, kslab_all = ckv, kslab
        kb_all, vb_all, ik2_all = per_batch(kb16), per_batch(vb16), per_batch(ik2)
        s_real = t
    else:
        p_ckv, p_krope, p_kb, p_vb, p_ki = past
        s_real = p_ckv.shape[1] + t
        cat = lambda old, new: jnp.concatenate([old, per_batch(new)], axis=1)
        ckv_all = cat(p_ckv, ckv).reshape(b * s_real, MLA_KV_LORA)
        p_kslab = jnp.pad(p_krope, ((0, 0), (0, 0), (MLA_NOPE, LANE - MLA_QK)))
        kslab_all = cat(p_kslab, kslab).reshape(b * s_real, LANE)
        kb_all = cat(p_kb.reshape(b, -1, DSA_WIDTH).astype(BF16), kb16)
        vb_all = cat(p_vb.reshape(b, -1, DSA_WIDTH).astype(BF16), vb16)
        ik2_all = cat(jnp.tile(p_ki, (1, 1, 2)).astype(BF16), ik2)
    ka, va = _mla_kv(ckv_all, kslab_all, lw)
    s_pad_a = pl.cdiv(s_real, tk_mla) * tk_mla
    ka = _pad_keys(ka.reshape(b, s_real, -1), s_pad_a)
    va = _pad_keys(va.reshape(b, s_real, -1), s_pad_a)
    oa = _mla_attn(per_batch(qa), ka, va, q_off=q_off, s_real=s_real, tq=tq_mla, tk=tk_mla)
    s_pad_b = pl.cdiv(s_real, tk_dsa) * tk_dsa
    ob = _dsa_attn(per_batch(qb), per_batch(qi), per_batch(wi), _pad_keys(kb_all, s_pad_b), _pad_keys(vb_all, s_pad_b),
                   _pad_keys(ik2_all, s_pad_b), q_off=q_off, s_real=s_real, tq=tq_dsa, tk=tk_dsa)
    y = _ffn(h, lw["ffn2_norm"], lw["ffn2_w_gate"], lw["ffn2_w_up"], lw["ffn2_w_down"],
             attn=(oa.reshape(b * t, MLA_WIDTH), ob.reshape(b * t, DSA_WIDTH), lw["w_out_a"], lw["w_out_b"]))
    rows = (per_batch(ckv), per_batch(krope), per_batch(kb).reshape(b, t, DSA_HEADS, DSA_HD),
            per_batch(vb).reshape(b, t, DSA_HEADS, DSA_HD), per_batch(ki))
    return y, rows


def kernel(x_prompt, x_sample, cache_mla_ckv, cache_mla_krope, cache_dsa_k, cache_dsa_v, cache_idx_k,
           ffn1_norm, ffn1_w_gate, ffn1_w_up, ffn1_w_down, mix_norm, w_in,
           mla_q_norm, mla_w_uq, mla_kv_norm, mla_w_ukv, mla_q_gain, mla_k_gain,
           dsa_q_gain, dsa_k_gain, w_out, ffn2_norm, ffn2_w_gate, ffn2_w_up, ffn2_w_down):
    params = dict(ffn1_norm=ffn1_norm, ffn1_w_gate=ffn1_w_gate, ffn1_w_up=ffn1_w_up, ffn1_w_down=ffn1_w_down,
                  mix_norm=mix_norm, w_in=w_in, mla_q_norm=mla_q_norm, mla_w_uq=mla_w_uq, mla_kv_norm=mla_kv_norm,
                  mla_w_ukv=mla_w_ukv, mla_q_gain=mla_q_gain, mla_k_gain=mla_k_gain, dsa_q_gain=dsa_q_gain,
                  dsa_k_gain=dsa_k_gain, w_out=w_out, ffn2_norm=ffn2_norm, ffn2_w_gate=ffn2_w_gate,
                  ffn2_w_up=ffn2_w_up, ffn2_w_down=ffn2_w_down)
    depth = w_in.shape[0]
    d_model = x_prompt.shape[-1]
    weights = [_layer_weights(params, l) for l in range(depth)]

    b_p, t_p = x_prompt.shape[:2]
    n_p = b_p * t_p
    tabs_p = _rope_tables(jnp.arange(t_p, dtype=I32), _row_tile(n_p, 512))
    tile_p = dict(tq_mla=min(t_p, 512), tk_mla=min(t_p, 512), tq_dsa=min(t_p, 256), tk_dsa=min(t_p, 512))
    h_p = x_prompt.reshape(n_p, d_model)
    p_rows = []
    for l in range(depth):
        h_p, rows = _trunk_layer(h_p, weights[l], tabs_p, None, b=b_p, t=t_p, q_off=0, **tile_p)
        p_rows.append(rows)

    b_s, t_s = x_sample.shape[:2]
    n_s = b_s * t_s
    past_len = cache_mla_ckv.shape[2]
    tabs_s = _rope_tables(past_len + jnp.arange(t_s, dtype=I32), _row_tile(n_s, 512))
    tile_s = dict(tq_mla=t_s, tk_mla=LANE, tq_dsa=LANE, tk_dsa=2 * LANE)
    h_s = x_sample.reshape(n_s, d_model)
    s_rows = []
    for l in range(depth):
        past = (cache_mla_ckv[l], cache_mla_krope[l], cache_dsa_k[l], cache_dsa_v[l], cache_idx_k[l])
        h_s, rows = _trunk_layer(h_s, weights[l], tabs_s, past, b=b_s, t=t_s, q_off=past_len, **tile_s)
        s_rows.append(rows)

    stack = lambda rows_by_layer, i: jnp.stack([r[i] for r in rows_by_layer])
    return (h_p.reshape(b_p, t_p, d_model), h_s.reshape(b_s, t_s, d_model),
            *[stack(p_rows, i) for i in range(5)], *[stack(s_rows, i) for i in range(5)])
```

```python
import functools

import jax
import jax.numpy as jnp
from jax import lax
from jax.experimental import pallas as pl
from jax.experimental.pallas import tpu as pltpu

F32 = jnp.float32
BF16 = jnp.bfloat16
I32 = jnp.int32

CHUNK_SHIFT = 6
ROPE_THETA = 500000.0
EPS = 1e-6
MLA_HEADS = 8
MLA_NOPE = 64
MLA_ROPE = 32
MLA_QK = MLA_NOPE + MLA_ROPE
MLA_V = 64
MLA_Q_LORA = 256
MLA_KV_LORA = 128
DSA_HEADS = 8
DSA_HD = 64
DSA_ROT = 16
IDX_HEADS = 4
IDX_HD = 64
IDX_W_SCALE = (IDX_HD * IDX_HEADS) ** -0.5
TOPK_MAX = 256
DSA_WIDTH = DSA_HEADS * DSA_HD
MLA_WIDTH = MLA_HEADS * MLA_V

LANE = 128
VMEM_LIMIT = 56 * 1024 * 1024

NEG = -1e30
LOG2E = 1.4426950408889634
INT_MIN = -(2 ** 31)
INT_MAX = 2 ** 31 - 1
KEY_NEG_INF = INT_MIN + 0x7FFFFF


def _nt_dot(a, b):
    return lax.dot_general(a, b, (((1,), (1,)), ((), ())), preferred_element_type=F32)


def _dot(a, b):
    return jnp.dot(a, b, preferred_element_type=F32)


def _rms(x, g):
    return x * lax.rsqrt(jnp.mean(x * x, axis=-1, keepdims=True) + EPS) * g


def _lane_iota(shape):
    return lax.broadcasted_iota(I32, shape, len(shape) - 1)


FFN_CHUNK = 256


def _ffn_body(x, g_ref, wg_ref, wu_ref, wd_ref, o_ref):
    xb = _rms(x, g_ref[...]).astype(BF16)
    d_ff = wg_ref.shape[1]
    acc = jnp.zeros(x.shape, F32)
    for c in range(d_ff // FFN_CHUNK):
        sl = slice(c * FFN_CHUNK, (c + 1) * FFN_CHUNK)
        gate = _dot(xb, wg_ref[:, sl])
        up = _dot(xb, wu_ref[:, sl])
        act = (gate * jax.nn.sigmoid(gate) * up).astype(BF16)
        acc = acc + _dot(act, wd_ref[sl, :])
    o_ref[...] = x + 0.5 * acc


def _ffn_kernel(x_ref, g_ref, wg_ref, wu_ref, wd_ref, o_ref):
    _ffn_body(x_ref[...], g_ref, wg_ref, wu_ref, wd_ref, o_ref)


def _out_ffn_kernel(h_ref, oa_ref, ob_ref, woa_ref, wob_ref, g_ref, wg_ref, wu_ref, wd_ref, o_ref):
    x = h_ref[...] + _dot(oa_ref[...], woa_ref[...]) + _dot(ob_ref[...], wob_ref[...])
    _ffn_body(x, g_ref, wg_ref, wu_ref, wd_ref, o_ref)


def _resident(shape):
    nd = len(shape)
    return pl.BlockSpec(shape, lambda *_: (0,) * nd, pipeline_mode=pl.Buffered(1))


def _row_tile(n, pref):
    for t in range(min(n, pref), 0, -16):
        if n % t == 0:
            return t
    raise ValueError(f"no row tile for {n} rows")


def _ffn(x, g, wg, wu, wd, attn=None):
    n, d = x.shape
    tm = _row_tile(n, 512)
    row = lambda w: pl.BlockSpec((tm, w), lambda i: (i, 0))
    w_specs = [_resident(g.shape), _resident(wg.shape), _resident(wu.shape), _resident(wd.shape)]
    params = pltpu.CompilerParams(dimension_semantics=("parallel",), vmem_limit_bytes=VMEM_LIMIT)
    out_shape = jax.ShapeDtypeStruct((n, d), F32)
    if attn is None:
        return pl.pallas_call(_ffn_kernel, grid=(n // tm,), in_specs=[row(d)] + w_specs, out_specs=row(d),
                              out_shape=out_shape, compiler_params=params, name="ffn")(x, g, wg, wu, wd)
    oa, ob, woa, wob = attn
    return pl.pallas_call(
        _out_ffn_kernel, grid=(n // tm,),
        in_specs=[row(d), row(oa.shape[1]), row(ob.shape[1]), _resident(woa.shape), _resident(wob.shape)] + w_specs,
        out_specs=row(d), out_shape=out_shape, compiler_params=params, name="out_ffn")(x, oa, ob, woa, wob, g, wg, wu, wd)


C_CQ = 0
C_CKV = C_CQ + MLA_Q_LORA
C_KR = C_CKV + MLA_KV_LORA
C_QB = C_KR + LANE
C_KB = C_QB + DSA_WIDTH
C_VB = C_KB + DSA_WIDTH
C_QI = C_VB + DSA_WIDTH
C_KI = C_QI + IDX_HEADS * IDX_HD
C_WI = C_KI + LANE
C_END = C_WI + LANE


def _rope_a(x, c, s):
    lane = _lane_iota(x.shape)
    partner = jnp.where(lane < MLA_NOPE + MLA_ROPE // 2, pltpu.roll(x, LANE - MLA_ROPE // 2, 1),
                        pltpu.roll(x, MLA_ROPE // 2, 1))
    return x * c + partner * s


def _rope_b(x, c, s):
    lane = _lane_iota(x.shape)
    half = DSA_ROT // 2
    partner = jnp.where((lane & (DSA_HD - 1)) < half, pltpu.roll(x, LANE - half, 1), pltpu.roll(x, half, 1))
    return x * c + partner * s


def _head96_norm(x, g):
    ms = jnp.sum(x * x, axis=-1, keepdims=True) * (1.0 / MLA_QK)
    return x * lax.rsqrt(ms + EPS) * g


def _head64_norm(x, g2):
    lane = _lane_iota(x.shape)
    lo = lane < DSA_HD
    sq = x * x
    s_lo = jnp.sum(jnp.where(lo, sq, 0.0), axis=-1, keepdims=True)
    s_hi = jnp.sum(jnp.where(lo, 0.0, sq), axis=-1, keepdims=True)
    ms = jnp.where(lo, s_lo, s_hi) * (1.0 / DSA_HD)
    return x * lax.rsqrt(ms + EPS) * g2


def _proj_kernel(h_ref, gmix_ref, win_ref, gq_ref, wuq_ref, gkv_ref, gqa_ref, gqb_ref, gkb_ref,
                 ca_ref, sa_ref, cb_ref, sb_ref,
                 ckv_ref, krope_ref, kslab_ref, kb_ref, vb_ref, ki_ref,
                 qa_ref, qb_ref, kb16_ref, vb16_ref, qi_ref, ik2_ref, wi_ref):
    u = _rms(h_ref[...], gmix_ref[...]).astype(BF16)
    ca, sa, cb, sb = ca_ref[...], sa_ref[...], cb_ref[...], sb_ref[...]

    def cols(start, width):
        return _dot(u, win_ref[:, start:start + width])

    cq = _rms(cols(C_CQ, MLA_Q_LORA), gq_ref[...]).astype(BF16)
    qa = _dot(cq, wuq_ref[...])
    qa_scale = MLA_QK ** -0.5 * LOG2E
    for h in range(MLA_HEADS):
        sl = slice(h * LANE, (h + 1) * LANE)
        qa_ref[:, sl] = (_head96_norm(_rope_a(qa[:, sl], ca, sa), gqa_ref[...]) * qa_scale).astype(BF16)

    ckv_ref[...] = _rms(cols(C_CKV, MLA_KV_LORA), gkv_ref[...])
    kslab = _rope_a(cols(C_KR, LANE), ca, sa)
    kslab_ref[...] = kslab
    krope_ref[...] = kslab[:, MLA_NOPE:MLA_NOPE + MLA_ROPE]

    qb = cols(C_QB, DSA_WIDTH)
    kb = cols(C_KB, DSA_WIDTH)
    qb_scale = DSA_HD ** -0.5 * LOG2E
    for p in range(DSA_WIDTH // LANE):
        sl = slice(p * LANE, (p + 1) * LANE)
        qb_ref[:, sl] = (_rope_b(_head64_norm(qb[:, sl], gqb_ref[...]), cb, sb) * qb_scale).astype(BF16)
        kp = _rope_b(_head64_norm(kb[:, sl], gkb_ref[...]), cb, sb)
        kb_ref[:, sl] = kp
        kb16_ref[:, sl] = kp.astype(BF16)
    vb = cols(C_VB, DSA_WIDTH)
    vb_ref[...] = vb
    vb16_ref[...] = vb.astype(BF16)

    qi = cols(C_QI, IDX_HEADS * IDX_HD)
    for p in range(IDX_HEADS * IDX_HD // LANE):
        sl = slice(p * LANE, (p + 1) * LANE)
        qi_ref[:, sl] = _rope_b(qi[:, sl], cb, sb).astype(BF16)
    ik2 = _rope_b(cols(C_KI, LANE), cb, sb)
    ki_ref[...] = ik2[:, :IDX_HD]
    ik2_ref[...] = ik2.astype(BF16)
    wi_ref[...] = cols(C_WI, LANE) * IDX_W_SCALE


def _proj(h, lw, tables, t_seq):
    n, d = h.shape
    tm = _row_tile(n, 512)
    ca, sa, cb, sb = tables
    n_tab = ca.shape[0] // tm
    row = lambda w: pl.BlockSpec((tm, w), lambda i: (i, 0))
    tab = pl.BlockSpec((tm, LANE), lambda i: (i % n_tab, 0))
    consts = [lw["mix_norm"], lw["w_in"], lw["mla_q_norm"], lw["w_uq"], lw["mla_kv_norm"],
              lw["mla_q_gain"], lw["dsa_q_gain"], lw["dsa_k_gain"]]
    out_widths = [(MLA_KV_LORA, F32), (MLA_ROPE, F32), (LANE, F32), (DSA_WIDTH, F32), (DSA_WIDTH, F32), (IDX_HD, F32),
                  (MLA_HEADS * LANE, BF16), (DSA_WIDTH, BF16), (DSA_WIDTH, BF16), (DSA_WIDTH, BF16),
                  (IDX_HEADS * IDX_HD, BF16), (LANE, BF16), (LANE, F32)]
    return pl.pallas_call(
        _proj_kernel, grid=(n // tm,),
        in_specs=[row(d)] + [_resident(c.shape) for c in consts] + [tab] * 4,
        out_specs=[row(w) for w, _ in out_widths],
        out_shape=[jax.ShapeDtypeStruct((n, w), dt) for w, dt in out_widths],
        compiler_params=pltpu.CompilerParams(dimension_semantics=("parallel",), vmem_limit_bytes=VMEM_LIMIT),
        name="proj")(h, *consts, ca, sa, cb, sb)


def _mla_kv_kernel(ckv_ref, kslab_ref, wn_ref, wv_ref, gk_ref, ka_ref, va_ref):
    c = ckv_ref[...].astype(BF16)
    kn = _dot(c, wn_ref[...])
    kslab = kslab_ref[...]
    for h in range(MLA_HEADS):
        sl = slice(h * LANE, (h + 1) * LANE)
        ka_ref[:, sl] = _head96_norm(kn[:, sl] + kslab, gk_ref[...]).astype(BF16)
    va_ref[...] = _dot(c, wv_ref[...]).astype(BF16)


def _mla_kv(ckv, kslab, lw):
    m = ckv.shape[0]
    tm = _row_tile(m, 512)
    row = lambda w: pl.BlockSpec((tm, w), lambda i: (i, 0))
    consts = [lw["w_ukv_nope"], lw["w_ukv_v"], lw["mla_k_gain"]]
    return pl.pallas_call(
        _mla_kv_kernel, grid=(m // tm,),
        in_specs=[row(MLA_KV_LORA), row(LANE)] + [_resident(c.shape) for c in consts],
        out_specs=[row(MLA_HEADS * LANE), row(MLA_WIDTH)],
        out_shape=[jax.ShapeDtypeStruct((m, MLA_HEADS * LANE), BF16), jax.ShapeDtypeStruct((m, MLA_WIDTH), BF16)],
        compiler_params=pltpu.CompilerParams(dimension_semantics=("parallel",), vmem_limit_bytes=VMEM_LIMIT),
        name="mla_kv")(ckv, kslab, *consts)


SUBLANES = 8
FOLD_CHAINS = 4


def _visible_t(q0, k0, tk, tq, s_real):
    q_chunk = (q0 + lax.broadcasted_iota(I32, (tk, tq), 1)) >> CHUNK_SHIFT
    k_idx = k0 + lax.broadcasted_iota(I32, (tk, tq), 0)
    return ((k_idx >> CHUNK_SHIFT) <= q_chunk) & (k_idx < s_real)


def _fold_rows(x, reduce):
    groups = x.shape[0] // SUBLANES
    if groups % FOLD_CHAINS == 0 and groups > FOLD_CHAINS:
        x = reduce(x.reshape(groups // FOLD_CHAINS, FOLD_CHAINS * SUBLANES, x.shape[1]), axis=0)
        groups = FOLD_CHAINS
    return reduce(x.reshape(groups, SUBLANES, x.shape[1]), axis=0)


def _softmax_update_t(s, v_t, m, l, acc):
    m_new = jnp.maximum(m, jnp.max(_fold_rows(s, jnp.max), axis=0, keepdims=True))
    alpha = jnp.exp2(m - m_new)
    p = jnp.exp2(s - m_new)
    l_new = alpha * l + jnp.sum(_fold_rows(p, jnp.sum), axis=0, keepdims=True)
    return m_new, l_new, alpha * acc + _dot(v_t, p.astype(BF16))


def _write_pairs_t(o_ref, ls, accs):
    upper = lax.broadcasted_iota(I32, accs[0].shape, 0) < LANE // 2
    for p in range(len(accs) // 2):
        o_ref[0, p * LANE:(p + 1) * LANE, :] = jnp.where(upper, accs[2 * p] / ls[2 * p],
                                                         accs[2 * p + 1] / ls[2 * p + 1]).astype(o_ref.dtype)


def _pad_queries(arrays, tq):
    t = arrays[0].shape[1]
    t_pad = pl.cdiv(t, tq) * tq
    return [jnp.pad(a, ((0, 0), (0, t_pad - t), (0, 0))) for a in arrays] if t_pad != t else list(arrays)


def _mla_attn_kernel(q_ref, k_ref, vt_ref, o_ref, m_ref, l_ref, acc_ref, *, tq, tk, q_off, s_real):
    qi, ki = pl.program_id(1), pl.program_id(2)
    q0 = q_off + qi * tq
    k0 = ki * tk

    heads = range(MLA_HEADS)

    @pl.when(ki == 0)
    def _():
        m_ref[...] = jnp.full(m_ref.shape, NEG, F32)
        l_ref[...] = jnp.zeros(l_ref.shape, F32)
        acc_ref[...] = jnp.zeros(acc_ref.shape, F32)

    def attend(masked):
        scores = [_nt_dot(k_ref[0, :, h * LANE:(h + 1) * LANE], q_ref[0, :, h * LANE:(h + 1) * LANE]) for h in heads]
        if masked:
            vis = _visible_t(q0, k0, tk, tq, s_real)
            scores = [jnp.where(vis, s, NEG) for s in scores]
        new = [_softmax_update_t(scores[h], vt_ref[0, (h // 2) * LANE:(h // 2 + 1) * LANE, :],
                                 m_ref[h], l_ref[h], acc_ref[h]) for h in heads]
        for h in heads:
            m_ref[h], l_ref[h], acc_ref[h] = new[h]

    needed = (k0 >> CHUNK_SHIFT) <= ((q0 + tq - 1) >> CHUNK_SHIFT)
    unmasked = (((k0 + tk - 1) >> CHUNK_SHIFT) <= (q0 >> CHUNK_SHIFT)) & (k0 + tk <= s_real)
    pl.when(needed & unmasked)(lambda: attend(False))
    pl.when(needed & ~unmasked)(lambda: attend(True))

    @pl.when(ki == pl.num_programs(2) - 1)
    def _():
        _write_pairs_t(o_ref, [l_ref[h] for h in heads], [acc_ref[h] for h in heads])


def _mla_attn(qa, ka, va, *, q_off, s_real, tq, tk):
    b, t, _ = qa.shape
    s_pad = ka.shape[1]
    (qa,) = _pad_queries([qa], tq)
    t_pad = qa.shape[1]
    nq, nk = t_pad // tq, s_pad // tk
    va_t = jnp.swapaxes(va, 1, 2)

    def last_block(qi):
        last = ((((q_off + qi * tq + tq - 1) >> CHUNK_SHIFT) + 1) << CHUNK_SHIFT) - 1
        return jnp.minimum(last // tk, nk - 1)

    kern = functools.partial(_mla_attn_kernel, tq=tq, tk=tk, q_off=q_off, s_real=s_real)
    o_t = pl.pallas_call(
        kern, grid=(b, nq, nk),
        in_specs=[pl.BlockSpec((1, tq, MLA_HEADS * LANE), lambda bi, qi, ki: (bi, qi, 0)),
                  pl.BlockSpec((1, tk, MLA_HEADS * LANE), lambda bi, qi, ki: (bi, jnp.minimum(ki, last_block(qi)), 0)),
                  pl.BlockSpec((1, MLA_WIDTH, tk), lambda bi, qi, ki: (bi, 0, jnp.minimum(ki, last_block(qi))))],
        out_specs=pl.BlockSpec((1, MLA_WIDTH, tq), lambda bi, qi, ki: (bi, 0, qi)),
        out_shape=jax.ShapeDtypeStruct((b, MLA_WIDTH, t_pad), BF16),
        scratch_shapes=[pltpu.VMEM((MLA_HEADS, 1, tq), F32), pltpu.VMEM((MLA_HEADS, 1, tq), F32),
                        pltpu.VMEM((MLA_HEADS, LANE, tq), F32)],
        compiler_params=pltpu.CompilerParams(dimension_semantics=("parallel", "parallel", "arbitrary"),
                                             vmem_limit_bytes=VMEM_LIMIT),
        name="mla_attn")(qa, ka, va_t)
    return jnp.swapaxes(o_t, 1, 2)[:, :t]


FALSE_POSITION_PROBES = 24
MAX_PROBES = 64


def _sortable(bits):
    return bits ^ ((bits >> 31) & INT_MAX)


def _dsa_kernel(qb_ref, qi_ref, wi_ref, kb_ref, vt_ref, ik2_ref, tri_ref, o_ref,
                key_ref, qm_ref, qim_ref, *, tq, tk, q_off, s_real, topk):
    q0 = q_off + pl.program_id(1) * tq
    vis_end = jnp.minimum((((q0 + tq - 1) >> CHUNK_SHIFT) + 1) << CHUNK_SHIFT, s_real)
    n_vis = (vis_end + tk - 1) // tk
    lane = _lane_iota((tq, LANE))
    lo = lane < LANE // 2

    for h in range(DSA_HEADS):
        pair = qb_ref[0, :, (h // 2) * LANE:(h // 2 + 1) * LANE]
        qm_ref[h] = jnp.where(lo if h % 2 == 0 else ~lo, pair, jnp.zeros_like(pair))
    for h in range(IDX_HEADS):
        pair = qi_ref[0, :, (h // 2) * LANE:(h // 2 + 1) * LANE]
        qim_ref[h] = jnp.where(lo if h % 2 == 0 else ~lo, pair, jnp.zeros_like(pair))
    w_rows = [wi_ref[0, h:h + 1, :] for h in range(IDX_HEADS)]

    def score_block(kb, carry, masked):
        q_max, q_min = carry
        k0 = pl.multiple_of(kb * tk, tk)
        ik = ik2_ref[0, pl.ds(k0, tk), :]
        score = jnp.zeros((tk, tq), F32)
        for h in range(IDX_HEADS):
            score = score + jnp.maximum(_nt_dot(ik, qim_ref[h]), 0.0) * w_rows[h]
        score = jnp.where(score == 0.0, 0.0, score)
        below = above = score
        if masked:
            vis = _visible_t(q0, k0, tk, tq, s_real)
            below, above = jnp.where(vis, score, -jnp.inf), jnp.where(vis, score, jnp.inf)
        key_ref[kb] = _sortable(pltpu.bitcast(below, I32))
        return (jnp.maximum(q_max, _fold_rows(below, jnp.max)), jnp.minimum(q_min, _fold_rows(above, jnp.min)))

    n_full = jnp.minimum(((q0 >> CHUNK_SHIFT) + 1) << CHUNK_SHIFT, s_real) // tk
    carry = (jnp.full((SUBLANES, tq), -jnp.inf, F32), jnp.full((SUBLANES, tq), jnp.inf, F32))
    carry = lax.fori_loop(0, n_full, functools.partial(score_block, masked=False), carry)
    q_max, q_min = lax.fori_loop(n_full, n_vis, functools.partial(score_block, masked=True), carry)
    row_max = jnp.max(q_max, axis=0, keepdims=True)
    row_min = jnp.min(q_min, axis=0, keepdims=True)

    def count_ge(t):
        def body(kb, acc):
            return acc + _fold_rows(jnp.where(key_ref[kb] >= t, 1.0, 0.0), jnp.sum)
        return jnp.sum(lax.fori_loop(0, n_vis, body, jnp.zeros((SUBLANES, tq), F32)), axis=0, keepdims=True)

    kf = float(topk)
    q_pos = q0 + lax.broadcasted_iota(I32, (1, tq), 1)
    n_row = jnp.minimum(((q_pos >> CHUNK_SHIFT) + 1) << CHUNK_SHIFT, s_real).astype(F32)
    few = n_row < kf
    lo0 = _sortable(lax.bitcast_convert_type(row_min, I32))
    hi0 = _sortable(lax.bitcast_convert_type(row_max, I32)) + 1

    def finished(lo_k, hi_k, c_lo):
        return few | (c_lo == kf) | (hi_k == lo_k + 1)

    def probe_step(carry):
        it, _, lo_k, hi_k, c_lo, c_hi, g_lo, g_hi, last = carry
        v_lo = lax.bitcast_convert_type(_sortable(lo_k), F32)
        v_hi = lax.bitcast_convert_type(_sortable(hi_k), F32)
        a = jnp.log(c_lo * (1.0 / kf)) * g_lo
        b = jnp.log(kf / jnp.maximum(c_hi, 0.5)) * g_hi
        p = _sortable(lax.bitcast_convert_type(v_lo + (v_hi - v_lo) * (a / (a + b)), I32))
        p = jnp.where(it >= FALSE_POSITION_PROBES, (lo_k >> 1) + (hi_k >> 1) + (lo_k & hi_k & 1), p)
        p = jnp.where((it == 0) & (lo_k < 0) & (hi_k > 0), 0, p)
        p = jnp.where(lo_k == 0, 1, p)
        p = jnp.minimum(jnp.maximum(p, lo_k + 1), hi_k - 1)
        c = count_ge(p)
        open_ = ~finished(lo_k, hi_k, c_lo)
        up = open_ & (c >= kf)
        down = open_ & (c < kf)
        lo_k, c_lo = jnp.where(up, p, lo_k), jnp.where(up, c, c_lo)
        hi_k, c_hi = jnp.where(down, p, hi_k), jnp.where(down, c, c_hi)
        g_lo = jnp.where(down, jnp.where(last < 0.0, 0.5 * g_lo, 1.0), jnp.where(up, 1.0, g_lo))
        g_hi = jnp.where(up, jnp.where(last > 0.0, 0.5 * g_hi, 1.0), jnp.where(down, 1.0, g_hi))
        last = jnp.where(up, 1.0, jnp.where(down, -1.0, last))
        n_open = jnp.max(jnp.where(finished(lo_k, hi_k, c_lo), 0, 1))
        return it + 1, n_open, lo_k, hi_k, c_lo, c_hi, g_lo, g_hi, last

    ones = jnp.ones((1, tq), F32)
    init = (jnp.int32(0), jnp.max(jnp.where(finished(lo0, hi0, n_row), 0, 1)), lo0, hi0, n_row, 0.0 * ones,
            ones, ones, 0.0 * ones)
    final = lax.while_loop(lambda c: (c[1] > 0) & (c[0] < MAX_PROBES), probe_step, init)
    t, c_lo, c_hi = final[2], final[4], final[5]

    t = jnp.where(few, KEY_NEG_INF, t)
    need = jnp.where(few, 0.0, jnp.where(c_lo == kf, kf, kf - c_hi))

    heads = range(DSA_HEADS)

    def attend_block(kb, carry):
        tied_before, ms, ls, accs = carry
        k0 = pl.multiple_of(kb * tk, tk)
        blk = key_ref[kb]
        tied = jnp.where(blk == t, 1.0, 0.0)
        tied_rank = (tied_before + _dot(tri_ref[...], tied.astype(BF16))) * tied
        sel = (blk >= t) & (tied_rank <= need)
        k_pairs = [kb_ref[0, pl.ds(k0, tk), p * LANE:(p + 1) * LANE] for p in range(DSA_HEADS // 2)]
        v_pairs = [vt_ref[0, kb, p * LANE:(p + 1) * LANE, :] for p in range(DSA_HEADS // 2)]
        scores = [_nt_dot(k_pairs[h // 2], qm_ref[h]) for h in heads]
        new = [_softmax_update_t(jnp.where(sel, scores[h], NEG), v_pairs[h // 2], ms[h], ls[h], accs[h]) for h in heads]
        return (tied_before + jnp.sum(_fold_rows(tied, jnp.sum), axis=0, keepdims=True),
                tuple(n[0] for n in new), tuple(n[1] for n in new), tuple(n[2] for n in new))

    row = lambda v: tuple(jnp.full((1, tq), v, F32) for _ in heads)
    init = (jnp.zeros((1, tq), F32), row(NEG), row(0.0), tuple(jnp.zeros((LANE, tq), F32) for _ in heads))
    _, _, ls, accs = lax.fori_loop(0, n_vis, attend_block, init)
    _write_pairs_t(o_ref, ls, accs)


def _dsa_attn(qb, qi, wi, kb, vb, ik2, *, q_off, s_real, tq, tk):
    b, t, _ = qb.shape
    s_pad = kb.shape[1]
    n_kb = s_pad // tk
    topk = min(TOPK_MAX, s_real // 4)
    qb, qi, wi = _pad_queries([qb, qi, wi], tq)
    t_pad = qb.shape[1]
    wi_t = jnp.swapaxes(wi[:, :, :SUBLANES], 1, 2)
    v_t = jnp.swapaxes(vb.reshape(b, n_kb, tk, DSA_WIDTH), 2, 3)
    tri = (lax.broadcasted_iota(I32, (tk, tk), 0) >= lax.broadcasted_iota(I32, (tk, tk), 1)).astype(BF16)
    qspec = lambda w: pl.BlockSpec((1, tq, w), lambda bi, i: (bi, i, 0))
    kspec = lambda w: pl.BlockSpec((1, s_pad, w), lambda bi, i: (bi, 0, 0), pipeline_mode=pl.Buffered(1))
    kern = functools.partial(_dsa_kernel, tq=tq, tk=tk, q_off=q_off, s_real=s_real, topk=topk)
    o_t = pl.pallas_call(
        kern, grid=(b, t_pad // tq),
        in_specs=[qspec(DSA_WIDTH), qspec(IDX_HEADS * IDX_HD),
                  pl.BlockSpec((1, SUBLANES, tq), lambda bi, i: (bi, 0, i)),
                  kspec(DSA_WIDTH),
                  pl.BlockSpec((1, n_kb, DSA_WIDTH, tk), lambda bi, i: (bi, 0, 0, 0), pipeline_mode=pl.Buffered(1)),
                  kspec(LANE), _resident((tk, tk))],
        out_specs=pl.BlockSpec((1, DSA_WIDTH, tq), lambda bi, i: (bi, 0, i)),
        out_shape=jax.ShapeDtypeStruct((b, DSA_WIDTH, t_pad), BF16),
        scratch_shapes=[pltpu.VMEM((n_kb, tk, tq), I32),
                        pltpu.VMEM((DSA_HEADS, tq, LANE), BF16), pltpu.VMEM((IDX_HEADS, tq, LANE), BF16)],
        compiler_params=pltpu.CompilerParams(dimension_semantics=("parallel", "arbitrary"),
                                             vmem_limit_bytes=VMEM_LIMIT),
        name="dsa_attn")(qb, qi, wi_t, kb, v_t, ik2, tri)
    return jnp.swapaxes(o_t, 1, 2)[:, :t]


def _pad_cols(w, width):
    return jnp.pad(w, ((0, 0), (0, width - w.shape[1])))


def _layer_weights(p, l):
    w_in = p["w_in"][l]
    off, pieces = 0, []
    for n in (MLA_Q_LORA, MLA_KV_LORA, MLA_ROPE, DSA_WIDTH, DSA_WIDTH, DSA_WIDTH, IDX_HEADS * IDX_HD, IDX_HD, IDX_HEADS):
        pieces.append(w_in[:, off:off + n])
        off += n
    c_q, c_kv, k_r, q_b, k_b, v_b, q_i, k_i, w_i = pieces
    k_r = jnp.pad(k_r, ((0, 0), (MLA_NOPE, LANE - MLA_QK)))
    w_in_p = jnp.concatenate([c_q, c_kv, k_r, q_b, k_b, v_b, q_i, k_i, k_i, _pad_cols(w_i, LANE)], axis=1)
    assert w_in_p.shape[1] == C_END

    d_lora = p["mla_w_uq"].shape[1]
    w_uq = p["mla_w_uq"][l].reshape(d_lora, MLA_HEADS, MLA_QK)
    w_uq = jnp.pad(w_uq, ((0, 0), (0, 0), (0, LANE - MLA_QK))).reshape(d_lora, MLA_HEADS * LANE)
    w_ukv = p["mla_w_ukv"][l].reshape(MLA_KV_LORA, MLA_HEADS, MLA_NOPE + MLA_V)
    w_nope = jnp.pad(w_ukv[:, :, :MLA_NOPE], ((0, 0), (0, 0), (0, LANE - MLA_NOPE))).reshape(MLA_KV_LORA, MLA_HEADS * LANE)
    w_v = w_ukv[:, :, MLA_NOPE:].reshape(MLA_KV_LORA, MLA_WIDTH)
    w_out = p["w_out"][l]

    row = lambda g: g[l][None, :].astype(F32)
    pad96 = lambda g: jnp.pad(g[l].astype(F32), (0, LANE - MLA_QK))[None, :]
    twice = lambda g: jnp.tile(g[l].astype(F32), 2)[None, :]
    lw = {
        "w_in": w_in_p.astype(BF16), "w_uq": w_uq.astype(BF16), "w_ukv_nope": w_nope.astype(BF16),
        "w_ukv_v": w_v.astype(BF16), "w_out_a": w_out[:MLA_WIDTH].astype(BF16), "w_out_b": w_out[MLA_WIDTH:].astype(BF16),
        "mix_norm": row(p["mix_norm"]), "mla_q_norm": row(p["mla_q_norm"]), "mla_kv_norm": row(p["mla_kv_norm"]),
        "mla_q_gain": pad96(p["mla_q_gain"]), "mla_k_gain": pad96(p["mla_k_gain"]),
        "dsa_q_gain": twice(p["dsa_q_gain"]), "dsa_k_gain": twice(p["dsa_k_gain"]),
    }
    for f in ("ffn1", "ffn2"):
        lw[f + "_norm"] = row(p[f + "_norm"])
        for w in ("w_gate", "w_up", "w_down"):
            lw[f + "_" + w] = p[f + "_" + w][l].astype(BF16)
    return lw


def _rope_tables(pos, rows):
    def cs(rot):
        inv = 1.0 / (ROPE_THETA ** (jnp.arange(0, rot, 2, dtype=F32) / rot))
        ang = pos.astype(F32)[:, None] * inv[None, :]
        return jnp.cos(ang), jnp.sin(ang)

    t = pos.shape[0]
    cos_a, sin_a = cs(MLA_ROPE)
    ones = lambda w: jnp.ones((t, w), F32)
    zeros = lambda w: jnp.zeros((t, w), F32)
    ca = jnp.concatenate([ones(MLA_NOPE), cos_a, cos_a, ones(LANE - MLA_QK)], axis=1)
    sa = jnp.concatenate([zeros(MLA_NOPE), -sin_a, sin_a, zeros(LANE - MLA_QK)], axis=1)
    cos_b, sin_b = cs(DSA_ROT)
    cb = jnp.tile(jnp.concatenate([cos_b, cos_b, ones(DSA_HD - DSA_ROT)], axis=1), (1, 2))
    sb = jnp.tile(jnp.concatenate([-sin_b, sin_b, zeros(DSA_HD - DSA_ROT)], axis=1), (1, 2))
    reps = max(1, rows // t)
    return tuple(jnp.tile(x, (reps, 1)) for x in (ca, sa, cb, sb))


def _pad_keys(x, s_pad):
    return jnp.pad(x, ((0, 0), (0, s_pad - x.shape[1]), (0, 0)))


def _trunk_layer(x, lw, tables, past, *, b, t, q_off, tq_mla, tk_mla, tq_dsa, tk_dsa):
    h = _ffn(x, lw["ffn1_norm"], lw["ffn1_w_gate"], lw["ffn1_w_up"], lw["ffn1_w_down"])
    (ckv, krope, kslab, kb, vb, ki, qa, qb, kb16, vb16, qi, ik2, wi) = _proj(h, lw, tables, t)
    per_batch = lambda a: a.reshape(b, t, a.shape[-1])
    if past is None:
        ckv_all, kslab_all = ckv, kslab
        kb_all, vb_all, ik2_all = per_batch(kb16), per_batch(vb16), per_batch(ik2)
        s_real = t
    else:
        p_ckv, p_krope, p_kb, p_vb, p_ki = past
        s_real = p_ckv.shape[1] + t
        cat = lambda old, new: jnp.concatenate([old, per_batch(new)], axis=1)
        ckv_all = cat(p_ckv, ckv).reshape(b * s_real, MLA_KV_LORA)
        p_kslab = jnp.pad(p_krope, ((0, 0), (0, 0), (MLA_NOPE, LANE - MLA_QK)))
        kslab_all = cat(p_kslab, kslab).reshape(b * s_real, LANE)
        kb_all = cat(p_kb.reshape(b, -1, DSA_WIDTH).astype(BF16), kb16)
        vb_all = cat(p_vb.reshape(b, -1, DSA_WIDTH).astype(BF16), vb16)
        ik2_all = cat(jnp.tile(p_ki, (1, 1, 2)).astype(BF16), ik2)
    ka, va = _mla_kv(ckv_all, kslab_all, lw)
    s_pad_a = pl.cdiv(s_real, tk_mla) * tk_mla
    ka = _pad_keys(ka.reshape(b, s_real, -1), s_pad_a)
    va = _pad_keys(va.reshape(b, s_real, -1), s_pad_a)
    oa = _mla_attn(per_batch(qa), ka, va, q_off=q_off, s_real=s_real, tq=tq_mla, tk=tk_mla)
    s_pad_b = pl.cdiv(s_real, tk_dsa) * tk_dsa
    ob = _dsa_attn(per_batch(qb), per_batch(qi), per_batch(wi), _pad_keys(kb_all, s_pad_b), _pad_keys(vb_all, s_pad_b),
                   _pad_keys(ik2_all, s_pad_b), q_off=q_off, s_real=s_real, tq=tq_dsa, tk=tk_dsa)
    y = _ffn(h, lw["ffn2_norm"], lw["ffn2_w_gate"], lw["ffn2_w_up"], lw["ffn2_w_down"],
             attn=(oa.reshape(b * t, MLA_WIDTH), ob.reshape(b * t, DSA_WIDTH), lw["w_out_a"], lw["w_out_b"]))
    rows = (per_batch(ckv), per_batch(krope), per_batch(kb).reshape(b, t, DSA_HEADS, DSA_HD),
            per_batch(vb).reshape(b, t, DSA_HEADS, DSA_HD), per_batch(ki))
    return y, rows


def kernel(x_prompt, x_sample, cache_mla_ckv, cache_mla_krope, cache_dsa_k, cache_dsa_v, cache_idx_k,
           ffn1_norm, ffn1_w_gate, ffn1_w_up, ffn1_w_down, mix_norm, w_in,
           mla_q_norm, mla_w_uq, mla_kv_norm, mla_w_ukv, mla_q_gain, mla_k_gain,
           dsa_q_gain, dsa_k_gain, w_out, ffn2_norm, ffn2_w_gate, ffn2_w_up, ffn2_w_down):
    params = dict(ffn1_norm=ffn1_norm, ffn1_w_gate=ffn1_w_gate, ffn1_w_up=ffn1_w_up, ffn1_w_down=ffn1_w_down,
                  mix_norm=mix_norm, w_in=w_in, mla_q_norm=mla_q_norm, mla_w_uq=mla_w_uq, mla_kv_norm=mla_kv_norm,
                  mla_w_ukv=mla_w_ukv, mla_q_gain=mla_q_gain, mla_k_gain=mla_k_gain, dsa_q_gain=dsa_q_gain,
                  dsa_k_gain=dsa_k_gain, w_out=w_out, ffn2_norm=ffn2_norm, ffn2_w_gate=ffn2_w_gate,
                  ffn2_w_up=ffn2_w_up, ffn2_w_down=ffn2_w_down)
    depth = w_in.shape[0]
    d_model = x_prompt.shape[-1]
    weights = [_layer_weights(params, l) for l in range(depth)]

    b_p, t_p = x_prompt.shape[:2]
    n_p = b_p * t_p
    tabs_p = _rope_tables(jnp.arange(t_p, dtype=I32), _row_tile(n_p, 512))
    tile_p = dict(tq_mla=min(t_p, 512), tk_mla=min(t_p, 512), tq_dsa=min(t_p, 512), tk_dsa=min(t_p, 512))
    h_p = x_prompt.reshape(n_p, d_model)
    p_rows = []
    for l in range(depth):
        h_p, rows = _trunk_layer(h_p, weights[l], tabs_p, None, b=b_p, t=t_p, q_off=0, **tile_p)
        p_rows.append(rows)

    b_s, t_s = x_sample.shape[:2]
    n_s = b_s * t_s
    past_len = cache_mla_ckv.shape[2]
    tabs_s = _rope_tables(past_len + jnp.arange(t_s, dtype=I32), _row_tile(n_s, 512))
    tile_s = dict(tq_mla=LANE, tk_mla=LANE, tq_dsa=LANE, tk_dsa=2 * LANE)
    h_s = x_sample.reshape(n_s, d_model)
    s_rows = []
    for l in range(depth):
        past = (cache_mla_ckv[l], cache_mla_krope[l], cache_dsa_k[l], cache_dsa_v[l], cache_idx_k[l])
        h_s, rows = _trunk_layer(h_s, weights[l], tabs_s, past, b=b_s, t=t_s, q_off=past_len, **tile_s)
        s_rows.append(rows)

    stack = lambda rows_by_layer, i: jnp.stack([r[i] for r in rows_by_layer])
    return (h_p.reshape(b_p, t_p, d_model), h_s.reshape(b_s, t_s, d_model),
            *[stack(p_rows, i) for i in range(5)], *[stack(s_rows, i) for i in range(5)])
```

```python
import functools

import jax
import jax.numpy as jnp
from jax import lax
from jax.experimental import pallas as pl
from jax.experimental.pallas import tpu as pltpu

F32 = jnp.float32
BF16 = jnp.bfloat16
I32 = jnp.int32

CHUNK_SHIFT = 6
ROPE_THETA = 500000.0
EPS = 1e-6
MLA_HEADS = 8
MLA_NOPE = 64
MLA_ROPE = 32
MLA_QK = MLA_NOPE + MLA_ROPE
MLA_V = 64
MLA_Q_LORA = 256
MLA_KV_LORA = 128
DSA_HEADS = 8
DSA_HD = 64
DSA_ROT = 16
IDX_HEADS = 4
IDX_HD = 64
IDX_W_SCALE = (IDX_HD * IDX_HEADS) ** -0.5
TOPK_MAX = 256
DSA_WIDTH = DSA_HEADS * DSA_HD
MLA_WIDTH = MLA_HEADS * MLA_V

LANE = 128
VMEM_LIMIT = 56 * 1024 * 1024

NEG = -1e30
F32_MAX = 3.4028234e38
LOG2E = 1.4426950408889634
INT_MIN = -(2 ** 31)
INT_MAX = 2 ** 31 - 1
KEY_NEG_INF = INT_MIN + 0x7FFFFF


def _nt_dot(a, b):
    return lax.dot_general(a, b, (((1,), (1,)), ((), ())), preferred_element_type=F32)


def _dot(a, b):
    return jnp.dot(a, b, preferred_element_type=F32)


def _rms(x, g):
    return x * lax.rsqrt(jnp.mean(x * x, axis=-1, keepdims=True) + EPS) * g


def _lane_iota(shape):
    return lax.broadcasted_iota(I32, shape, len(shape) - 1)


FFN_CHUNK = 256


def _ffn_body(x, g_ref, wg_ref, wu_ref, wd_ref, o_ref):
    xb = _rms(x, g_ref[...]).astype(BF16)
    d_ff = wg_ref.shape[1]
    acc = jnp.zeros(x.shape, F32)
    for c in range(d_ff // FFN_CHUNK):
        sl = slice(c * FFN_CHUNK, (c + 1) * FFN_CHUNK)
        gate = _dot(xb, wg_ref[:, sl])
        up = _dot(xb, wu_ref[:, sl])
        act = (gate * jax.nn.sigmoid(gate) * up).astype(BF16)
        acc = acc + _dot(act, wd_ref[sl, :])
    o_ref[...] = x + 0.5 * acc


def _ffn_kernel(x_ref, g_ref, wg_ref, wu_ref, wd_ref, o_ref):
    _ffn_body(x_ref[...], g_ref, wg_ref, wu_ref, wd_ref, o_ref)


def _out_ffn_kernel(h_ref, oa_ref, ob_ref, woa_ref, wob_ref, g_ref, wg_ref, wu_ref, wd_ref, o_ref):
    x = h_ref[...] + _dot(oa_ref[...], woa_ref[...]) + _dot(ob_ref[...], wob_ref[...])
    _ffn_body(x, g_ref, wg_ref, wu_ref, wd_ref, o_ref)


def _resident(shape):
    nd = len(shape)
    return pl.BlockSpec(shape, lambda *_: (0,) * nd, pipeline_mode=pl.Buffered(1))


def _row_tile(n, pref):
    for t in range(min(n, pref), 0, -16):
        if n % t == 0:
            return t
    raise ValueError(f"no row tile for {n} rows")


def _ffn(x, g, wg, wu, wd, attn=None):
    n, d = x.shape
    tm = _row_tile(n, 512)
    row = lambda w: pl.BlockSpec((tm, w), lambda i: (i, 0))
    w_specs = [_resident(g.shape), _resident(wg.shape), _resident(wu.shape), _resident(wd.shape)]
    params = pltpu.CompilerParams(dimension_semantics=("parallel",), vmem_limit_bytes=VMEM_LIMIT)
    out_shape = jax.ShapeDtypeStruct((n, d), F32)
    if attn is None:
        return pl.pallas_call(_ffn_kernel, grid=(n // tm,), in_specs=[row(d)] + w_specs, out_specs=row(d),
                              out_shape=out_shape, compiler_params=params, name="ffn")(x, g, wg, wu, wd)
    oa, ob, woa, wob = attn
    return pl.pallas_call(
        _out_ffn_kernel, grid=(n // tm,),
        in_specs=[row(d), row(oa.shape[1]), row(ob.shape[1]), _resident(woa.shape), _resident(wob.shape)] + w_specs,
        out_specs=row(d), out_shape=out_shape, compiler_params=params, name="out_ffn")(x, oa, ob, woa, wob, g, wg, wu, wd)


C_CQ = 0
C_CKV = C_CQ + MLA_Q_LORA
C_KR = C_CKV + MLA_KV_LORA
C_QB = C_KR + LANE
C_KB = C_QB + DSA_WIDTH
C_VB = C_KB + DSA_WIDTH
C_QI = C_VB + DSA_WIDTH
C_KI = C_QI + IDX_HEADS * IDX_HD
C_WI = C_KI + LANE
C_END = C_WI + LANE


def _rope_a(x, c, s):
    lane = _lane_iota(x.shape)
    partner = jnp.where(lane < MLA_NOPE + MLA_ROPE // 2, pltpu.roll(x, LANE - MLA_ROPE // 2, 1),
                        pltpu.roll(x, MLA_ROPE // 2, 1))
    return x * c + partner * s


def _rope_b(x, c, s):
    lane = _lane_iota(x.shape)
    half = DSA_ROT // 2
    partner = jnp.where((lane & (DSA_HD - 1)) < half, pltpu.roll(x, LANE - half, 1), pltpu.roll(x, half, 1))
    return x * c + partner * s


def _head96_norm(x, g):
    ms = jnp.sum(x * x, axis=-1, keepdims=True) * (1.0 / MLA_QK)
    return x * lax.rsqrt(ms + EPS) * g


def _head64_norm(x, g2):
    lane = _lane_iota(x.shape)
    lo = lane < DSA_HD
    sq = x * x
    s_lo = jnp.sum(jnp.where(lo, sq, 0.0), axis=-1, keepdims=True)
    s_hi = jnp.sum(jnp.where(lo, 0.0, sq), axis=-1, keepdims=True)
    ms = jnp.where(lo, s_lo, s_hi) * (1.0 / DSA_HD)
    return x * lax.rsqrt(ms + EPS) * g2


def _proj_kernel(h_ref, gmix_ref, win_ref, gq_ref, wuq_ref, gkv_ref, gqa_ref, gqb_ref, gkb_ref,
                 ca_ref, sa_ref, cb_ref, sb_ref,
                 ckv_ref, krope_ref, kslab_ref, kb_ref, vb_ref, ki_ref,
                 qa_ref, qb_ref, kb16_ref, vb16_ref, qi_ref, ik2_ref, wi_ref):
    u = _rms(h_ref[...], gmix_ref[...]).astype(BF16)
    ca, sa, cb, sb = ca_ref[...], sa_ref[...], cb_ref[...], sb_ref[...]

    def cols(start, width):
        return _dot(u, win_ref[:, start:start + width])

    cq = _rms(cols(C_CQ, MLA_Q_LORA), gq_ref[...]).astype(BF16)
    qa = _dot(cq, wuq_ref[...])
    qa_scale = MLA_QK ** -0.5 * LOG2E
    for h in range(MLA_HEADS):
        sl = slice(h * LANE, (h + 1) * LANE)
        qa_ref[:, sl] = (_head96_norm(_rope_a(qa[:, sl], ca, sa), gqa_ref[...]) * qa_scale).astype(BF16)

    ckv_ref[...] = _rms(cols(C_CKV, MLA_KV_LORA), gkv_ref[...])
    kslab = _rope_a(cols(C_KR, LANE), ca, sa)
    kslab_ref[...] = kslab
    krope_ref[...] = kslab[:, MLA_NOPE:MLA_NOPE + MLA_ROPE]

    qb = cols(C_QB, DSA_WIDTH)
    kb = cols(C_KB, DSA_WIDTH)
    qb_scale = DSA_HD ** -0.5 * LOG2E
    for p in range(DSA_WIDTH // LANE):
        sl = slice(p * LANE, (p + 1) * LANE)
        qb_ref[:, sl] = (_rope_b(_head64_norm(qb[:, sl], gqb_ref[...]), cb, sb) * qb_scale).astype(BF16)
        kp = _rope_b(_head64_norm(kb[:, sl], gkb_ref[...]), cb, sb)
        kb_ref[:, sl] = kp
        kb16_ref[:, sl] = kp.astype(BF16)
    vb = cols(C_VB, DSA_WIDTH)
    vb_ref[...] = vb
    vb16_ref[...] = vb.astype(BF16)

    qi = cols(C_QI, IDX_HEADS * IDX_HD)
    for p in range(IDX_HEADS * IDX_HD // LANE):
        sl = slice(p * LANE, (p + 1) * LANE)
        qi_ref[:, sl] = _rope_b(qi[:, sl], cb, sb).astype(BF16)
    ik2 = _rope_b(cols(C_KI, LANE), cb, sb)
    ki_ref[...] = ik2[:, :IDX_HD]
    ik2_ref[...] = ik2.astype(BF16)
    wi_ref[...] = cols(C_WI, LANE) * IDX_W_SCALE


def _proj(h, lw, tables, t_seq):
    n, d = h.shape
    tm = _row_tile(n, 512)
    ca, sa, cb, sb = tables
    n_tab = ca.shape[0] // tm
    row = lambda w: pl.BlockSpec((tm, w), lambda i: (i, 0))
    tab = pl.BlockSpec((tm, LANE), lambda i: (i % n_tab, 0))
    consts = [lw["mix_norm"], lw["w_in"], lw["mla_q_norm"], lw["w_uq"], lw["mla_kv_norm"],
              lw["mla_q_gain"], lw["dsa_q_gain"], lw["dsa_k_gain"]]
    out_widths = [(MLA_KV_LORA, F32), (MLA_ROPE, F32), (LANE, F32), (DSA_WIDTH, F32), (DSA_WIDTH, F32), (IDX_HD, F32),
                  (MLA_HEADS * LANE, BF16), (DSA_WIDTH, BF16), (DSA_WIDTH, BF16), (DSA_WIDTH, BF16),
                  (IDX_HEADS * IDX_HD, BF16), (LANE, BF16), (LANE, F32)]
    return pl.pallas_call(
        _proj_kernel, grid=(n // tm,),
        in_specs=[row(d)] + [_resident(c.shape) for c in consts] + [tab] * 4,
        out_specs=[row(w) for w, _ in out_widths],
        out_shape=[jax.ShapeDtypeStruct((n, w), dt) for w, dt in out_widths],
        compiler_params=pltpu.CompilerParams(dimension_semantics=("parallel",), vmem_limit_bytes=VMEM_LIMIT),
        name="proj")(h, *consts, ca, sa, cb, sb)


def _mla_kv_kernel(ckv_ref, kslab_ref, wn_ref, wv_ref, gk_ref, ka_ref, va_ref):
    c = ckv_ref[...].astype(BF16)
    kn = _dot(c, wn_ref[...])
    kslab = kslab_ref[...]
    for h in range(MLA_HEADS):
        sl = slice(h * LANE, (h + 1) * LANE)
        ka_ref[:, sl] = _head96_norm(kn[:, sl] + kslab, gk_ref[...]).astype(BF16)
    va_ref[...] = _dot(c, wv_ref[...]).astype(BF16)


def _mla_kv(ckv, kslab, lw):
    m = ckv.shape[0]
    tm = _row_tile(m, 512)
    row = lambda w: pl.BlockSpec((tm, w), lambda i: (i, 0))
    consts = [lw["w_ukv_nope"], lw["w_ukv_v"], lw["mla_k_gain"]]
    return pl.pallas_call(
        _mla_kv_kernel, grid=(m // tm,),
        in_specs=[row(MLA_KV_LORA), row(LANE)] + [_resident(c.shape) for c in consts],
        out_specs=[row(MLA_HEADS * LANE), row(MLA_WIDTH)],
        out_shape=[jax.ShapeDtypeStruct((m, MLA_HEADS * LANE), BF16), jax.ShapeDtypeStruct((m, MLA_WIDTH), BF16)],
        compiler_params=pltpu.CompilerParams(dimension_semantics=("parallel",), vmem_limit_bytes=VMEM_LIMIT),
        name="mla_kv")(ckv, kslab, *consts)


SUBLANES = 8
FOLD_CHAINS = 4


def _visible_t(q0, k0, tk, tq, s_real):
    q_chunk = (q0 + lax.broadcasted_iota(I32, (tk, tq), 1)) >> CHUNK_SHIFT
    k_idx = k0 + lax.broadcasted_iota(I32, (tk, tq), 0)
    return ((k_idx >> CHUNK_SHIFT) <= q_chunk) & (k_idx < s_real)


def _fold_rows(x, reduce):
    groups = x.shape[0] // SUBLANES
    if groups % FOLD_CHAINS == 0 and groups > FOLD_CHAINS:
        x = reduce(x.reshape(groups // FOLD_CHAINS, FOLD_CHAINS * SUBLANES, x.shape[1]), axis=0)
        groups = FOLD_CHAINS
    return reduce(x.reshape(groups, SUBLANES, x.shape[1]), axis=0)


def _softmax_update_t(s, v_t, m, l, acc):
    m_new = jnp.maximum(m, jnp.max(_fold_rows(s, jnp.max), axis=0, keepdims=True))
    alpha = jnp.exp2(m - m_new)
    p = jnp.exp2(s - m_new)
    l_new = alpha * l + jnp.sum(_fold_rows(p, jnp.sum), axis=0, keepdims=True)
    return m_new, l_new, alpha * acc + _dot(v_t, p.astype(BF16))


def _write_pairs_t(o_ref, ls, accs):
    upper = lax.broadcasted_iota(I32, accs[0].shape, 0) < LANE // 2
    for p in range(len(accs) // 2):
        o_ref[0, p * LANE:(p + 1) * LANE, :] = jnp.where(upper, accs[2 * p] / ls[2 * p],
                                                         accs[2 * p + 1] / ls[2 * p + 1]).astype(o_ref.dtype)


def _pad_queries(arrays, tq):
    t = arrays[0].shape[1]
    t_pad = pl.cdiv(t, tq) * tq
    return [jnp.pad(a, ((0, 0), (0, t_pad - t), (0, 0))) for a in arrays] if t_pad != t else list(arrays)


def _visible_blocks(q0, tq, tk, s_real):
    n_full = jnp.minimum(((q0 >> CHUNK_SHIFT) + 1) << CHUNK_SHIFT, s_real) // tk
    vis_end = jnp.minimum((((q0 + tq - 1) >> CHUNK_SHIFT) + 1) << CHUNK_SHIFT, s_real)
    return n_full, (vis_end + tk - 1) // tk


def _mla_attn_kernel(q_ref, k_ref, vt_ref, o_ref, *, tq, tk, q_off, s_real):
    q0 = q_off + pl.program_id(1) * tq
    n_full, n_vis = _visible_blocks(q0, tq, tk, s_real)
    heads = range(MLA_HEADS)

    def attend_block(kb, carry, masked):
        ms, ls, accs = carry
        k0 = pl.multiple_of(kb * tk, tk)
        scores = [_nt_dot(k_ref[0, pl.ds(k0, tk), h * LANE:(h + 1) * LANE], q_ref[0, :, h * LANE:(h + 1) * LANE])
                  for h in heads]
        if masked:
            cap = jnp.where(_visible_t(q0, k0, tk, tq, s_real), F32_MAX, NEG)
            scores = [jnp.minimum(s, cap) for s in scores]
        new = [_softmax_update_t(scores[h], vt_ref[0, kb, (h // 2) * LANE:(h // 2 + 1) * LANE, :], ms[h], ls[h], accs[h])
               for h in heads]
        return tuple(n[0] for n in new), tuple(n[1] for n in new), tuple(n[2] for n in new)

    row = lambda v: tuple(jnp.full((1, tq), v, F32) for _ in heads)
    carry = (row(NEG), row(0.0), tuple(jnp.zeros((LANE, tq), F32) for _ in heads))
    carry = lax.fori_loop(0, n_full, functools.partial(attend_block, masked=False), carry)
    _, ls, accs = lax.fori_loop(n_full, n_vis, functools.partial(attend_block, masked=True), carry)
    _write_pairs_t(o_ref, ls, accs)


def _key_blocks_t(v, tk):
    b, s_pad, w = v.shape
    return jnp.swapaxes(v.reshape(b, s_pad // tk, tk, w), 2, 3)


def _mla_attn(qa, ka, va, *, q_off, s_real, tq, tk):
    b, t, _ = qa.shape
    s_pad = ka.shape[1]
    (qa,) = _pad_queries([qa], tq)
    t_pad = qa.shape[1]
    kern = functools.partial(_mla_attn_kernel, tq=tq, tk=tk, q_off=q_off, s_real=s_real)
    o_t = pl.pallas_call(
        kern, grid=(b, t_pad // tq),
        in_specs=[pl.BlockSpec((1, tq, MLA_HEADS * LANE), lambda bi, qi: (bi, qi, 0)),
                  pl.BlockSpec((1, s_pad, MLA_HEADS * LANE), lambda bi, qi: (bi, 0, 0), pipeline_mode=pl.Buffered(1)),
                  pl.BlockSpec((1, s_pad // tk, MLA_WIDTH, tk), lambda bi, qi: (bi, 0, 0, 0),
                               pipeline_mode=pl.Buffered(1))],
        out_specs=pl.BlockSpec((1, MLA_WIDTH, tq), lambda bi, qi: (bi, 0, qi)),
        out_shape=jax.ShapeDtypeStruct((b, MLA_WIDTH, t_pad), BF16),
        compiler_params=pltpu.CompilerParams(dimension_semantics=("parallel", "arbitrary"),
                                             vmem_limit_bytes=VMEM_LIMIT),
        name="mla_attn")(qa, ka, _key_blocks_t(va, tk))
    return jnp.swapaxes(o_t, 1, 2)[:, :t]


FALSE_POSITION_PROBES = 24
MAX_PROBES = 64


def _sortable(bits):
    return bits ^ ((bits >> 31) & INT_MAX)


def _dsa_kernel(qb_ref, qi_ref, wi_ref, kb_ref, vt_ref, ik2_ref, tri_ref, o_ref,
                key_ref, qm_ref, qim_ref, *, tq, tk, q_off, s_real, topk):
    q0 = q_off + pl.program_id(1) * tq
    n_full, n_vis = _visible_blocks(q0, tq, tk, s_real)
    lane = _lane_iota((tq, LANE))
    lo = lane < LANE // 2

    for h in range(DSA_HEADS):
        pair = qb_ref[0, :, (h // 2) * LANE:(h // 2 + 1) * LANE]
        qm_ref[h] = jnp.where(lo if h % 2 == 0 else ~lo, pair, jnp.zeros_like(pair))
    for h in range(IDX_HEADS):
        pair = qi_ref[0, :, (h // 2) * LANE:(h // 2 + 1) * LANE]
        qim_ref[h] = jnp.where(lo if h % 2 == 0 else ~lo, pair, jnp.zeros_like(pair))
    w_rows = [wi_ref[0, h:h + 1, :] for h in range(IDX_HEADS)]

    def score_block(kb, carry, masked):
        q_max, q_min = carry
        k0 = pl.multiple_of(kb * tk, tk)
        ik = ik2_ref[0, pl.ds(k0, tk), :]
        score = jnp.zeros((tk, tq), F32)
        for h in range(IDX_HEADS):
            score = score + jnp.maximum(_nt_dot(ik, qim_ref[h]), 0.0) * w_rows[h]
        score = jnp.where(score == 0.0, 0.0, score)
        below = above = score
        if masked:
            vis = _visible_t(q0, k0, tk, tq, s_real)
            below, above = jnp.where(vis, score, -jnp.inf), jnp.where(vis, score, jnp.inf)
        key_ref[kb] = _sortable(pltpu.bitcast(below, I32))
        return (jnp.maximum(q_max, _fold_rows(below, jnp.max)), jnp.minimum(q_min, _fold_rows(above, jnp.min)))

    carry = (jnp.full((SUBLANES, tq), -jnp.inf, F32), jnp.full((SUBLANES, tq), jnp.inf, F32))
    carry = lax.fori_loop(0, n_full, functools.partial(score_block, masked=False), carry)
    q_max, q_min = lax.fori_loop(n_full, n_vis, functools.partial(score_block, masked=True), carry)
    row_max = jnp.max(q_max, axis=0, keepdims=True)
    row_min = jnp.min(q_min, axis=0, keepdims=True)

    def count_ge(t):
        def body(kb, acc):
            return acc + _fold_rows(jnp.where(key_ref[kb] >= t, 1.0, 0.0), jnp.sum)
        return jnp.sum(lax.fori_loop(0, n_vis, body, jnp.zeros((SUBLANES, tq), F32)), axis=0, keepdims=True)

    kf = float(topk)
    q_pos = q0 + lax.broadcasted_iota(I32, (1, tq), 1)
    n_row = jnp.minimum(((q_pos >> CHUNK_SHIFT) + 1) << CHUNK_SHIFT, s_real).astype(F32)
    few = n_row < kf
    lo0 = _sortable(lax.bitcast_convert_type(row_min, I32))
    hi0 = _sortable(lax.bitcast_convert_type(row_max, I32)) + 1

    def finished(lo_k, hi_k, c_lo):
        return few | (c_lo == kf) | (hi_k == lo_k + 1)

    def probe_step(carry):
        it, _, lo_k, hi_k, c_lo, c_hi, g_lo, g_hi, last = carry
        v_lo = lax.bitcast_convert_type(_sortable(lo_k), F32)
        v_hi = lax.bitcast_convert_type(_sortable(hi_k), F32)
        a = jnp.log(c_lo * (1.0 / kf)) * g_lo
        b = jnp.log(kf / jnp.maximum(c_hi, 0.5)) * g_hi
        p = _sortable(lax.bitcast_convert_type(v_lo + (v_hi - v_lo) * (a / (a + b)), I32))
        p = jnp.where(it >= FALSE_POSITION_PROBES, (lo_k >> 1) + (hi_k >> 1) + (lo_k & hi_k & 1), p)
        p = jnp.where((it == 0) & (lo_k < 0) & (hi_k > 0), 0, p)
        p = jnp.where(lo_k == 0, 1, p)
        p = jnp.minimum(jnp.maximum(p, lo_k + 1), hi_k - 1)
        c = count_ge(p)
        open_ = ~finished(lo_k, hi_k, c_lo)
        up = open_ & (c >= kf)
        down = open_ & (c < kf)
        lo_k, c_lo = jnp.where(up, p, lo_k), jnp.where(up, c, c_lo)
        hi_k, c_hi = jnp.where(down, p, hi_k), jnp.where(down, c, c_hi)
        g_lo = jnp.where(down, jnp.where(last < 0.0, 0.5 * g_lo, 1.0), jnp.where(up, 1.0, g_lo))
        g_hi = jnp.where(up, jnp.where(last > 0.0, 0.5 * g_hi, 1.0), jnp.where(down, 1.0, g_hi))
        last = jnp.where(up, 1.0, jnp.where(down, -1.0, last))
        n_open = jnp.max(jnp.where(finished(lo_k, hi_k, c_lo), 0, 1))
        return it + 1, n_open, lo_k, hi_k, c_lo, c_hi, g_lo, g_hi, last

    ones = jnp.ones((1, tq), F32)
    init = (jnp.int32(0), jnp.max(jnp.where(finished(lo0, hi0, n_row), 0, 1)), lo0, hi0, n_row, 0.0 * ones,
            ones, ones, 0.0 * ones)
    final = lax.while_loop(lambda c: (c[1] > 0) & (c[0] < MAX_PROBES), probe_step, init)
    t, c_lo, c_hi = final[2], final[4], final[5]

    t = jnp.where(few, KEY_NEG_INF, t)
    need = jnp.where(few, 0.0, jnp.where(c_lo == kf, kf, kf - c_hi))

    heads = range(DSA_HEADS)

    def attend_block(kb, carry):
        tied_before, ms, ls, accs = carry
        k0 = pl.multiple_of(kb * tk, tk)
        blk = key_ref[kb]
        tied = jnp.where(blk == t, 1.0, 0.0)
        tied_rank = (tied_before + _dot(tri_ref[...], tied.astype(BF16))) * tied
        cap = jnp.where((blk >= t) & (tied_rank <= need), F32_MAX, NEG)
        k_pairs = [kb_ref[0, pl.ds(k0, tk), p * LANE:(p + 1) * LANE] for p in range(DSA_HEADS // 2)]
        v_pairs = [vt_ref[0, kb, p * LANE:(p + 1) * LANE, :] for p in range(DSA_HEADS // 2)]
        scores = [_nt_dot(k_pairs[h // 2], qm_ref[h]) for h in heads]
        new = [_softmax_update_t(jnp.minimum(scores[h], cap), v_pairs[h // 2], ms[h], ls[h], accs[h]) for h in heads]
        return (tied_before + jnp.sum(_fold_rows(tied, jnp.sum), axis=0, keepdims=True),
                tuple(n[0] for n in new), tuple(n[1] for n in new), tuple(n[2] for n in new))

    row = lambda v: tuple(jnp.full((1, tq), v, F32) for _ in heads)
    init = (jnp.zeros((1, tq), F32), row(NEG), row(0.0), tuple(jnp.zeros((LANE, tq), F32) for _ in heads))
    _, _, ls, accs = lax.fori_loop(0, n_vis, attend_block, init)
    _write_pairs_t(o_ref, ls, accs)


def _dsa_attn(qb, qi, wi, kb, vb, ik2, *, q_off, s_real, tq, tk):
    b, t, _ = qb.shape
    s_pad = kb.shape[1]
    n_kb = s_pad // tk
    topk = min(TOPK_MAX, s_real // 4)
    qb, qi, wi = _pad_queries([qb, qi, wi], tq)
    t_pad = qb.shape[1]
    wi_t = jnp.swapaxes(wi[:, :, :SUBLANES], 1, 2)
    v_t = _key_blocks_t(vb, tk)
    tri = (lax.broadcasted_iota(I32, (tk, tk), 0) >= lax.broadcasted_iota(I32, (tk, tk), 1)).astype(BF16)
    qspec = lambda w: pl.BlockSpec((1, tq, w), lambda bi, i: (bi, i, 0))
    kspec = lambda w: pl.BlockSpec((1, s_pad, w), lambda bi, i: (bi, 0, 0), pipeline_mode=pl.Buffered(1))
    kern = functools.partial(_dsa_kernel, tq=tq, tk=tk, q_off=q_off, s_real=s_real, topk=topk)
    o_t = pl.pallas_call(
        kern, grid=(b, t_pad // tq),
        in_specs=[qspec(DSA_WIDTH), qspec(IDX_HEADS * IDX_HD),
                  pl.BlockSpec((1, SUBLANES, tq), lambda bi, i: (bi, 0, i)),
                  kspec(DSA_WIDTH),
                  pl.BlockSpec((1, n_kb, DSA_WIDTH, tk), lambda bi, i: (bi, 0, 0, 0), pipeline_mode=pl.Buffered(1)),
                  kspec(LANE), _resident((tk, tk))],
        out_specs=pl.BlockSpec((1, DSA_WIDTH, tq), lambda bi, i: (bi, 0, i)),
        out_shape=jax.ShapeDtypeStruct((b, DSA_WIDTH, t_pad), BF16),
        scratch_shapes=[pltpu.VMEM((n_kb, tk, tq), I32),
                        pltpu.VMEM((DSA_HEADS, tq, LANE), BF16), pltpu.VMEM((IDX_HEADS, tq, LANE), BF16)],
        compiler_params=pltpu.CompilerParams(dimension_semantics=("parallel", "arbitrary"),
                                             vmem_limit_bytes=VMEM_LIMIT),
        name="dsa_attn")(qb, qi, wi_t, kb, v_t, ik2, tri)
    return jnp.swapaxes(o_t, 1, 2)[:, :t]


def _pad_cols(w, width):
    return jnp.pad(w, ((0, 0), (0, width - w.shape[1])))


def _layer_weights(p, l):
    w_in = p["w_in"][l]
    off, pieces = 0, []
    for n in (MLA_Q_LORA, MLA_KV_LORA, MLA_ROPE, DSA_WIDTH, DSA_WIDTH, DSA_WIDTH, IDX_HEADS * IDX_HD, IDX_HD, IDX_HEADS):
        pieces.append(w_in[:, off:off + n])
        off += n
    c_q, c_kv, k_r, q_b, k_b, v_b, q_i, k_i, w_i = pieces
    k_r = jnp.pad(k_r, ((0, 0), (MLA_NOPE, LANE - MLA_QK)))
    w_in_p = jnp.concatenate([c_q, c_kv, k_r, q_b, k_b, v_b, q_i, k_i, k_i, _pad_cols(w_i, LANE)], axis=1)
    assert w_in_p.shape[1] == C_END

    d_lora = p["mla_w_uq"].shape[1]
    w_uq = p["mla_w_uq"][l].reshape(d_lora, MLA_HEADS, MLA_QK)
    w_uq = jnp.pad(w_uq, ((0, 0), (0, 0), (0, LANE - MLA_QK))).reshape(d_lora, MLA_HEADS * LANE)
    w_ukv = p["mla_w_ukv"][l].reshape(MLA_KV_LORA, MLA_HEADS, MLA_NOPE + MLA_V)
    w_nope = jnp.pad(w_ukv[:, :, :MLA_NOPE], ((0, 0), (0, 0), (0, LANE - MLA_NOPE))).reshape(MLA_KV_LORA, MLA_HEADS * LANE)
    w_v = w_ukv[:, :, MLA_NOPE:].reshape(MLA_KV_LORA, MLA_WIDTH)
    w_out = p["w_out"][l]

    row = lambda g: g[l][None, :].astype(F32)
    pad96 = lambda g: jnp.pad(g[l].astype(F32), (0, LANE - MLA_QK))[None, :]
    twice = lambda g: jnp.tile(g[l].astype(F32), 2)[None, :]
    lw = {
        "w_in": w_in_p.astype(BF16), "w_uq": w_uq.astype(BF16), "w_ukv_nope": w_nope.astype(BF16),
        "w_ukv_v": w_v.astype(BF16), "w_out_a": w_out[:MLA_WIDTH].astype(BF16), "w_out_b": w_out[MLA_WIDTH:].astype(BF16),
        "mix_norm": row(p["mix_norm"]), "mla_q_norm": row(p["mla_q_norm"]), "mla_kv_norm": row(p["mla_kv_norm"]),
        "mla_q_gain": pad96(p["mla_q_gain"]), "mla_k_gain": pad96(p["mla_k_gain"]),
        "dsa_q_gain": twice(p["dsa_q_gain"]), "dsa_k_gain": twice(p["dsa_k_gain"]),
    }
    for f in ("ffn1", "ffn2"):
        lw[f + "_norm"] = row(p[f + "_norm"])
        for w in ("w_gate", "w_up", "w_down"):
            lw[f + "_" + w] = p[f + "_" + w][l].astype(BF16)
    return lw


def _rope_tables(pos, rows):
    def cs(rot):
        inv = 1.0 / (ROPE_THETA ** (jnp.arange(0, rot, 2, dtype=F32) / rot))
        ang = pos.astype(F32)[:, None] * inv[None, :]
        return jnp.cos(ang), jnp.sin(ang)

    t = pos.shape[0]
    cos_a, sin_a = cs(MLA_ROPE)
    ones = lambda w: jnp.ones((t, w), F32)
    zeros = lambda w: jnp.zeros((t, w), F32)
    ca = jnp.concatenate([ones(MLA_NOPE), cos_a, cos_a, ones(LANE - MLA_QK)], axis=1)
    sa = jnp.concatenate([zeros(MLA_NOPE), -sin_a, sin_a, zeros(LANE - MLA_QK)], axis=1)
    cos_b, sin_b = cs(DSA_ROT)
    cb = jnp.tile(jnp.concatenate([cos_b, cos_b, ones(DSA_HD - DSA_ROT)], axis=1), (1, 2))
    sb = jnp.tile(jnp.concatenate([-sin_b, sin_b, zeros(DSA_HD - DSA_ROT)], axis=1), (1, 2))
    reps = max(1, rows // t)
    return tuple(jnp.tile(x, (reps, 1)) for x in (ca, sa, cb, sb))


def _pad_keys(x, s_pad):
    return jnp.pad(x, ((0, 0), (0, s_pad - x.shape[1]), (0, 0)))


def _trunk_layer(x, lw, tables, past, *, b, t, q_off, tq_mla, tk_mla, tq_dsa, tk_dsa):
    h = _ffn(x, lw["ffn1_norm"], lw["ffn1_w_gate"], lw["ffn1_w_up"], lw["ffn1_w_down"])
    (ckv, krope, kslab, kb, vb, ki, qa, qb, kb16, vb16, qi, ik2, wi) = _proj(h, lw, tables, t)
    per_batch = lambda a: a.reshape(b, t, a.shape[-1])
    if past is None:
        ckv_all, kslab_all = ckv, kslab
        kb_all, vb_all, ik2_all = per_batch(kb16), per_batch(vb16), per_batch(ik2)
        s_real = t
    else:
        p_ckv, p_krope, p_kb, p_vb, p_ki = past
        s_real = p_ckv.shape[1] + t
        cat = lambda old, new: jnp.concatenate([old, per_batch(new)], axis=1)
        ckv_all = cat(p_ckv, ckv).reshape(b * s_real, MLA_KV_LORA)
        p_kslab = jnp.pad(p_krope, ((0, 0), (0, 0), (MLA_NOPE, LANE - MLA_QK)))
        kslab_all = cat(p_kslab, kslab).reshape(b * s_real, LANE)
        kb_all = cat(p_kb.reshape(b, -1, DSA_WIDTH).astype(BF16), kb16)
        vb_all = cat(p_vb.reshape(b, -1, DSA_WIDTH).astype(BF16), vb16)
        ik2_all = cat(jnp.tile(p_ki, (1, 1, 2)).astype(BF16), ik2)
    ka, va = _mla_kv(ckv_all, kslab_all, lw)
    s_pad_a = pl.cdiv(s_real, tk_mla) * tk_mla
    ka = _pad_keys(ka.reshape(b, s_real, -1), s_pad_a)
    va = _pad_keys(va.reshape(b, s_real, -1), s_pad_a)
    oa = _mla_attn(per_batch(qa), ka, va, q_off=q_off, s_real=s_real, tq=tq_mla, tk=tk_mla)
    s_pad_b = pl.cdiv(s_real, tk_dsa) * tk_dsa
    ob = _dsa_attn(per_batch(qb), per_batch(qi), per_batch(wi), _pad_keys(kb_all, s_pad_b), _pad_keys(vb_all, s_pad_b),
                   _pad_keys(ik2_all, s_pad_b), q_off=q_off, s_real=s_real, tq=tq_dsa, tk=tk_dsa)
    y = _ffn(h, lw["ffn2_norm"], lw["ffn2_w_gate"], lw["ffn2_w_up"], lw["ffn2_w_down"],
             attn=(oa.reshape(b * t, MLA_WIDTH), ob.reshape(b * t, DSA_WIDTH), lw["w_out_a"], lw["w_out_b"]))
    rows = (per_batch(ckv), per_batch(krope), per_batch(kb).reshape(b, t, DSA_HEADS, DSA_HD),
            per_batch(vb).reshape(b, t, DSA_HEADS, DSA_HD), per_batch(ki))
    return y, rows


def kernel(x_prompt, x_sample, cache_mla_ckv, cache_mla_krope, cache_dsa_k, cache_dsa_v, cache_idx_k,
           ffn1_norm, ffn1_w_gate, ffn1_w_up, ffn1_w_down, mix_norm, w_in,
           mla_q_norm, mla_w_uq, mla_kv_norm, mla_w_ukv, mla_q_gain, mla_k_gain,
           dsa_q_gain, dsa_k_gain, w_out, ffn2_norm, ffn2_w_gate, ffn2_w_up, ffn2_w_down):
    params = dict(ffn1_norm=ffn1_norm, ffn1_w_gate=ffn1_w_gate, ffn1_w_up=ffn1_w_up, ffn1_w_down=ffn1_w_down,
                  mix_norm=mix_norm, w_in=w_in, mla_q_norm=mla_q_norm, mla_w_uq=mla_w_uq, mla_kv_norm=mla_kv_norm,
                  mla_w_ukv=mla_w_ukv, mla_q_gain=mla_q_gain, mla_k_gain=mla_k_gain, dsa_q_gain=dsa_q_gain,
                  dsa_k_gain=dsa_k_gain, w_out=w_out, ffn2_norm=ffn2_norm, ffn2_w_gate=ffn2_w_gate,
                  ffn2_w_up=ffn2_w_up, ffn2_w_down=ffn2_w_down)
    depth = w_in.shape[0]
    d_model = x_prompt.shape[-1]
    weights = [_layer_weights(params, l) for l in range(depth)]

    b_p, t_p = x_prompt.shape[:2]
    n_p = b_p * t_p
    tabs_p = _rope_tables(jnp.arange(t_p, dtype=I32), _row_tile(n_p, 512))
    tile_p = dict(tq_mla=min(t_p, 512), tk_mla=min(t_p, 512), tq_dsa=min(t_p, 512), tk_dsa=min(t_p, 512))
    h_p = x_prompt.reshape(n_p, d_model)
    p_rows = []
    for l in range(depth):
        h_p, rows = _trunk_layer(h_p, weights[l], tabs_p, None, b=b_p, t=t_p, q_off=0, **tile_p)
        p_rows.append(rows)

    b_s, t_s = x_sample.shape[:2]
    n_s = b_s * t_s
    past_len = cache_mla_ckv.shape[2]
    tabs_s = _rope_tables(past_len + jnp.arange(t_s, dtype=I32), _row_tile(n_s, 512))
    tile_s = dict(tq_mla=LANE, tk_mla=LANE, tq_dsa=LANE, tk_dsa=2 * LANE)
    h_s = x_sample.reshape(n_s, d_model)
    s_rows = []
    for l in range(depth):
        past = (cache_mla_ckv[l], cache_mla_krope[l], cache_dsa_k[l], cache_dsa_v[l], cache_idx_k[l])
        h_s, rows = _trunk_layer(h_s, weights[l], tabs_s, past, b=b_s, t=t_s, q_off=past_len, **tile_s)
        s_rows.append(rows)

    stack = lambda rows_by_layer, i: jnp.stack([r[i] for r in rows_by_layer])
    return (h_p.reshape(b_p, t_p, d_model), h_s.reshape(b_s, t_s, d_model),
            *[stack(p_rows, i) for i in range(5)], *[stack(s_rows, i) for i in range(5)])
```

```python
import functools

import jax
import jax.numpy as jnp
from jax import lax
from jax.experimental import pallas as pl
from jax.experimental.pallas import tpu as pltpu

F32 = jnp.float32
BF16 = jnp.bfloat16
I32 = jnp.int32

CHUNK_SHIFT = 6
ROPE_THETA = 500000.0
EPS = 1e-6
MLA_HEADS = 8
MLA_NOPE = 64
MLA_ROPE = 32
MLA_QK = MLA_NOPE + MLA_ROPE
MLA_V = 64
MLA_Q_LORA = 256
MLA_KV_LORA = 128
DSA_HEADS = 8
DSA_HD = 64
DSA_ROT = 16
IDX_HEADS = 4
IDX_HD = 64
IDX_W_SCALE = (IDX_HD * IDX_HEADS) ** -0.5
TOPK_MAX = 256
DSA_WIDTH = DSA_HEADS * DSA_HD
MLA_WIDTH = MLA_HEADS * MLA_V

LANE = 128
VMEM_LIMIT = 56 * 1024 * 1024

NEG = -1e30
F32_MAX = 3.4028234e38
LOG2E = 1.4426950408889634
INT_MIN = -(2 ** 31)
INT_MAX = 2 ** 31 - 1
KEY_NEG_INF = INT_MIN + 0x7FFFFF


def _nt_dot(a, b):
    return lax.dot_general(a, b, (((1,), (1,)), ((), ())), preferred_element_type=F32)


def _dot(a, b):
    return jnp.dot(a, b, preferred_element_type=F32)


def _rms(x, g):
    return x * lax.rsqrt(jnp.mean(x * x, axis=-1, keepdims=True) + EPS) * g


def _lane_iota(shape):
    return lax.broadcasted_iota(I32, shape, len(shape) - 1)


FFN_CHUNK = 256


def _ffn_body(x, g_ref, wg_ref, wu_ref, wd_ref, o_ref):
    xb = _rms(x, g_ref[...]).astype(BF16)
    d_ff = wg_ref.shape[1]
    acc = jnp.zeros(x.shape, F32)
    for c in range(d_ff // FFN_CHUNK):
        sl = slice(c * FFN_CHUNK, (c + 1) * FFN_CHUNK)
        gate = _dot(xb, wg_ref[:, sl])
        up = _dot(xb, wu_ref[:, sl])
        act = (gate * jax.nn.sigmoid(gate) * up).astype(BF16)
        acc = acc + _dot(act, wd_ref[sl, :])
    o_ref[...] = x + 0.5 * acc


def _ffn_kernel(x_ref, g_ref, wg_ref, wu_ref, wd_ref, o_ref):
    _ffn_body(x_ref[...], g_ref, wg_ref, wu_ref, wd_ref, o_ref)


def _out_ffn_kernel(h_ref, oa_ref, ob_ref, woa_ref, wob_ref, g_ref, wg_ref, wu_ref, wd_ref, o_ref):
    x = h_ref[...] + _dot(oa_ref[...], woa_ref[...]) + _dot(ob_ref[...], wob_ref[...])
    _ffn_body(x, g_ref, wg_ref, wu_ref, wd_ref, o_ref)


def _resident(shape):
    nd = len(shape)
    return pl.BlockSpec(shape, lambda *_: (0,) * nd, pipeline_mode=pl.Buffered(1))


def _row_tile(n, pref):
    for t in range(min(n, pref), 0, -16):
        if n % t == 0:
            return t
    raise ValueError(f"no row tile for {n} rows")


def _ffn(x, g, wg, wu, wd, attn=None):
    n, d = x.shape
    tm = _row_tile(n, 512)
    row = lambda w: pl.BlockSpec((tm, w), lambda i: (i, 0))
    w_specs = [_resident(g.shape), _resident(wg.shape), _resident(wu.shape), _resident(wd.shape)]
    params = pltpu.CompilerParams(dimension_semantics=("parallel",), vmem_limit_bytes=VMEM_LIMIT)
    out_shape = jax.ShapeDtypeStruct((n, d), F32)
    if attn is None:
        return pl.pallas_call(_ffn_kernel, grid=(n // tm,), in_specs=[row(d)] + w_specs, out_specs=row(d),
                              out_shape=out_shape, compiler_params=params, name="ffn")(x, g, wg, wu, wd)
    oa, ob, woa, wob = attn
    return pl.pallas_call(
        _out_ffn_kernel, grid=(n // tm,),
        in_specs=[row(d), row(oa.shape[1]), row(ob.shape[1]), _resident(woa.shape), _resident(wob.shape)] + w_specs,
        out_specs=row(d), out_shape=out_shape, compiler_params=params, name="out_ffn")(x, oa, ob, woa, wob, g, wg, wu, wd)


C_CQ = 0
C_CKV = C_CQ + MLA_Q_LORA
C_KR = C_CKV + MLA_KV_LORA
C_QB = C_KR + LANE
C_KB = C_QB + DSA_WIDTH
C_VB = C_KB + DSA_WIDTH
C_QI = C_VB + DSA_WIDTH
C_KI = C_QI + IDX_HEADS * IDX_HD
C_WI = C_KI + LANE
C_END = C_WI + LANE


def _rope_a(x, c, s):
    lane = _lane_iota(x.shape)
    partner = jnp.where(lane < MLA_NOPE + MLA_ROPE // 2, pltpu.roll(x, LANE - MLA_ROPE // 2, 1),
                        pltpu.roll(x, MLA_ROPE // 2, 1))
    return x * c + partner * s


def _rope_b(x, c, s):
    lane = _lane_iota(x.shape)
    half = DSA_ROT // 2
    partner = jnp.where((lane & (DSA_HD - 1)) < half, pltpu.roll(x, LANE - half, 1), pltpu.roll(x, half, 1))
    return x * c + partner * s


def _head96_norm(x, g):
    ms = jnp.sum(x * x, axis=-1, keepdims=True) * (1.0 / MLA_QK)
    return x * lax.rsqrt(ms + EPS) * g


def _head64_norm(x, g2):
    lane = _lane_iota(x.shape)
    lo = lane < DSA_HD
    sq = x * x
    s_lo = jnp.sum(jnp.where(lo, sq, 0.0), axis=-1, keepdims=True)
    s_hi = jnp.sum(jnp.where(lo, 0.0, sq), axis=-1, keepdims=True)
    ms = jnp.where(lo, s_lo, s_hi) * (1.0 / DSA_HD)
    return x * lax.rsqrt(ms + EPS) * g2


def _proj_kernel(h_ref, gmix_ref, win_ref, gq_ref, wuq_ref, gkv_ref, gqa_ref, gqb_ref, gkb_ref,
                 ca_ref, sa_ref, cb_ref, sb_ref,
                 ckv_ref, krope_ref, kslab_ref, kb_ref, vb_ref, ki_ref,
                 qa_ref, qb_ref, kb16_ref, vb16_ref, qi_ref, ik2_ref, wi_ref):
    u = _rms(h_ref[...], gmix_ref[...]).astype(BF16)
    ca, sa, cb, sb = ca_ref[...], sa_ref[...], cb_ref[...], sb_ref[...]

    def cols(start, width):
        return _dot(u, win_ref[:, start:start + width])

    cq = _rms(cols(C_CQ, MLA_Q_LORA), gq_ref[...]).astype(BF16)
    qa = _dot(cq, wuq_ref[...])
    qa_scale = MLA_QK ** -0.5 * LOG2E
    for h in range(MLA_HEADS):
        sl = slice(h * LANE, (h + 1) * LANE)
        qa_ref[:, sl] = (_head96_norm(_rope_a(qa[:, sl], ca, sa), gqa_ref[...]) * qa_scale).astype(BF16)

    ckv_ref[...] = _rms(cols(C_CKV, MLA_KV_LORA), gkv_ref[...])
    kslab = _rope_a(cols(C_KR, LANE), ca, sa)
    kslab_ref[...] = kslab
    krope_ref[...] = kslab[:, MLA_NOPE:MLA_NOPE + MLA_ROPE]

    qb = cols(C_QB, DSA_WIDTH)
    kb = cols(C_KB, DSA_WIDTH)
    qb_scale = DSA_HD ** -0.5 * LOG2E
    for p in range(DSA_WIDTH // LANE):
        sl = slice(p * LANE, (p + 1) * LANE)
        qb_ref[:, sl] = (_rope_b(_head64_norm(qb[:, sl], gqb_ref[...]), cb, sb) * qb_scale).astype(BF16)
        kp = _rope_b(_head64_norm(kb[:, sl], gkb_ref[...]), cb, sb)
        kb_ref[:, sl] = kp
        kb16_ref[:, sl] = kp.astype(BF16)
    vb = cols(C_VB, DSA_WIDTH)
    vb_ref[...] = vb
    vb16_ref[...] = vb.astype(BF16)

    qi = cols(C_QI, IDX_HEADS * IDX_HD)
    for p in range(IDX_HEADS * IDX_HD // LANE):
        sl = slice(p * LANE, (p + 1) * LANE)
        qi_ref[:, sl] = _rope_b(qi[:, sl], cb, sb).astype(BF16)
    ik2 = _rope_b(cols(C_KI, LANE), cb, sb)
    ki_ref[...] = ik2[:, :IDX_HD]
    ik2_ref[...] = ik2.astype(BF16)
    wi_ref[...] = cols(C_WI, LANE) * IDX_W_SCALE


def _proj(h, lw, tables, t_seq):
    n, d = h.shape
    tm = _row_tile(n, 512)
    ca, sa, cb, sb = tables
    n_tab = ca.shape[0] // tm
    row = lambda w: pl.BlockSpec((tm, w), lambda i: (i, 0))
    tab = pl.BlockSpec((tm, LANE), lambda i: (i % n_tab, 0))
    consts = [lw["mix_norm"], lw["w_in"], lw["mla_q_norm"], lw["w_uq"], lw["mla_kv_norm"],
              lw["mla_q_gain"], lw["dsa_q_gain"], lw["dsa_k_gain"]]
    out_widths = [(MLA_KV_LORA, F32), (MLA_ROPE, F32), (LANE, F32), (DSA_WIDTH, F32), (DSA_WIDTH, F32), (IDX_HD, F32),
                  (MLA_HEADS * LANE, BF16), (DSA_WIDTH, BF16), (DSA_WIDTH, BF16), (DSA_WIDTH, BF16),
                  (IDX_HEADS * IDX_HD, BF16), (LANE, BF16), (LANE, F32)]
    return pl.pallas_call(
        _proj_kernel, grid=(n // tm,),
        in_specs=[row(d)] + [_resident(c.shape) for c in consts] + [tab] * 4,
        out_specs=[row(w) for w, _ in out_widths],
        out_shape=[jax.ShapeDtypeStruct((n, w), dt) for w, dt in out_widths],
        compiler_params=pltpu.CompilerParams(dimension_semantics=("parallel",), vmem_limit_bytes=VMEM_LIMIT),
        name="proj")(h, *consts, ca, sa, cb, sb)


def _mla_kv_kernel(ckv_ref, kslab_ref, wn_ref, wv_ref, gk_ref, ka_ref, va_ref):
    c = ckv_ref[...].astype(BF16)
    kn = _dot(c, wn_ref[...])
    kslab = kslab_ref[...]
    for h in range(MLA_HEADS):
        sl = slice(h * LANE, (h + 1) * LANE)
        ka_ref[:, sl] = _head96_norm(kn[:, sl] + kslab, gk_ref[...]).astype(BF16)
    va_ref[...] = _dot(c, wv_ref[...]).astype(BF16)


def _mla_kv(ckv, kslab, lw):
    m = ckv.shape[0]
    tm = _row_tile(m, 512)
    row = lambda w: pl.BlockSpec((tm, w), lambda i: (i, 0))
    consts = [lw["w_ukv_nope"], lw["w_ukv_v"], lw["mla_k_gain"]]
    return pl.pallas_call(
        _mla_kv_kernel, grid=(m // tm,),
        in_specs=[row(MLA_KV_LORA), row(LANE)] + [_resident(c.shape) for c in consts],
        out_specs=[row(MLA_HEADS * LANE), row(MLA_WIDTH)],
        out_shape=[jax.ShapeDtypeStruct((m, MLA_HEADS * LANE), BF16), jax.ShapeDtypeStruct((m, MLA_WIDTH), BF16)],
        compiler_params=pltpu.CompilerParams(dimension_semantics=("parallel",), vmem_limit_bytes=VMEM_LIMIT),
        name="mla_kv")(ckv, kslab, *consts)


SUBLANES = 8
FOLD_CHAINS = 4


def _visible_t(q0, k0, tk, tq, s_real):
    q_chunk = (q0 + lax.broadcasted_iota(I32, (tk, tq), 1)) >> CHUNK_SHIFT
    k_idx = k0 + lax.broadcasted_iota(I32, (tk, tq), 0)
    return ((k_idx >> CHUNK_SHIFT) <= q_chunk) & (k_idx < s_real)


def _fold_rows(x, reduce):
    groups = x.shape[0] // SUBLANES
    if groups % FOLD_CHAINS == 0 and groups > FOLD_CHAINS:
        x = reduce(x.reshape(groups // FOLD_CHAINS, FOLD_CHAINS * SUBLANES, x.shape[1]), axis=0)
        groups = FOLD_CHAINS
    return reduce(x.reshape(groups, SUBLANES, x.shape[1]), axis=0)


ONES_ROWS = 16
PAIR_ROWS = LANE + ONES_ROWS


def _softmax_update_t(s, v_t, m, acc):
    m_new = jnp.maximum(m, jnp.max(_fold_rows(s, jnp.max), axis=0, keepdims=True))
    p = jnp.exp2(s - m_new)
    return m_new, jnp.exp2(m - m_new) * acc + _dot(v_t, p.astype(BF16))


def _write_pairs_t(o_ref, accs):
    upper = lax.broadcasted_iota(I32, (LANE, accs[0].shape[1]), 0) < LANE // 2
    out = [a[:LANE] / a[LANE:LANE + 1] for a in accs]
    for p in range(len(accs) // 2):
        o_ref[0, p * LANE:(p + 1) * LANE, :] = jnp.where(upper, out[2 * p], out[2 * p + 1]).astype(o_ref.dtype)


def _pad_queries(arrays, tq):
    t = arrays[0].shape[1]
    t_pad = pl.cdiv(t, tq) * tq
    return [jnp.pad(a, ((0, 0), (0, t_pad - t), (0, 0))) for a in arrays] if t_pad != t else list(arrays)


def _visible_blocks(q0, tq, tk, s_real):
    n_full = jnp.minimum(((q0 >> CHUNK_SHIFT) + 1) << CHUNK_SHIFT, s_real) // tk
    vis_end = jnp.minimum((((q0 + tq - 1) >> CHUNK_SHIFT) + 1) << CHUNK_SHIFT, s_real)
    return n_full, (vis_end + tk - 1) // tk


def _mla_attn_kernel(q_ref, k_ref, vt_ref, o_ref, *, tq, tk, q_off, s_real):
    q0 = q_off + pl.program_id(1) * tq
    n_full, n_vis = _visible_blocks(q0, tq, tk, s_real)
    heads = range(MLA_HEADS)

    def attend_block(kb, carry, masked):
        ms, accs = carry
        k0 = pl.multiple_of(kb * tk, tk)
        scores = [_nt_dot(k_ref[0, pl.ds(k0, tk), h * LANE:(h + 1) * LANE], q_ref[0, :, h * LANE:(h + 1) * LANE])
                  for h in heads]
        if masked:
            cap = jnp.where(_visible_t(q0, k0, tk, tq, s_real), F32_MAX, NEG)
            scores = [jnp.minimum(s, cap) for s in scores]
        new = [_softmax_update_t(scores[h], vt_ref[0, kb, (h // 2) * PAIR_ROWS:(h // 2 + 1) * PAIR_ROWS, :], ms[h], accs[h])
               for h in heads]
        return tuple(n[0] for n in new), tuple(n[1] for n in new)

    carry = _softmax_init(len(heads), tq)
    carry = lax.fori_loop(0, n_full, functools.partial(attend_block, masked=False), carry)
    _, accs = lax.fori_loop(n_full, n_vis, functools.partial(attend_block, masked=True), carry)
    _write_pairs_t(o_ref, accs)


def _softmax_init(n_heads, tq):
    return (tuple(jnp.full((1, tq), NEG, F32) for _ in range(n_heads)),
            tuple(jnp.zeros((PAIR_ROWS, tq), F32) for _ in range(n_heads)))


def _key_blocks_t(v, tk):
    b, s_pad, w = v.shape
    n_kb, pairs = s_pad // tk, w // LANE
    v_t = jnp.swapaxes(v.reshape(b, n_kb, tk, pairs, LANE), 2, 4)
    v_t = jnp.swapaxes(v_t, 2, 3)
    ones = jnp.ones((b, n_kb, pairs, ONES_ROWS, tk), v.dtype)
    return jnp.concatenate([v_t, ones], axis=3).reshape(b, n_kb, pairs * PAIR_ROWS, tk)


def _mla_attn(qa, ka, va, *, q_off, s_real, tq, tk):
    b, t, _ = qa.shape
    s_pad = ka.shape[1]
    (qa,) = _pad_queries([qa], tq)
    t_pad = qa.shape[1]
    kern = functools.partial(_mla_attn_kernel, tq=tq, tk=tk, q_off=q_off, s_real=s_real)
    o_t = pl.pallas_call(
        kern, grid=(b, t_pad // tq),
        in_specs=[pl.BlockSpec((1, tq, MLA_HEADS * LANE), lambda bi, qi: (bi, qi, 0)),
                  pl.BlockSpec((1, s_pad, MLA_HEADS * LANE), lambda bi, qi: (bi, 0, 0), pipeline_mode=pl.Buffered(1)),
                  pl.BlockSpec((1, s_pad // tk, MLA_WIDTH // LANE * PAIR_ROWS, tk), lambda bi, qi: (bi, 0, 0, 0),
                               pipeline_mode=pl.Buffered(1))],
        out_specs=pl.BlockSpec((1, MLA_WIDTH, tq), lambda bi, qi: (bi, 0, qi)),
        out_shape=jax.ShapeDtypeStruct((b, MLA_WIDTH, t_pad), BF16),
        compiler_params=pltpu.CompilerParams(dimension_semantics=("parallel", "arbitrary"),
                                             vmem_limit_bytes=VMEM_LIMIT),
        name="mla_attn")(qa, ka, _key_blocks_t(va, tk))
    return jnp.swapaxes(o_t, 1, 2)[:, :t]


FALSE_POSITION_PROBES = 24
MAX_PROBES = 64


def _sortable(bits):
    return bits ^ ((bits >> 31) & INT_MAX)


def _dsa_kernel(qb_ref, qi_ref, wi_ref, kb_ref, vt_ref, ik2_ref, tri_ref, o_ref,
                key_ref, qm_ref, qim_ref, *, tq, tk, q_off, s_real, topk):
    q0 = q_off + pl.program_id(1) * tq
    n_full, n_vis = _visible_blocks(q0, tq, tk, s_real)
    lane = _lane_iota((tq, LANE))
    lo = lane < LANE // 2

    for h in range(DSA_HEADS):
        pair = qb_ref[0, :, (h // 2) * LANE:(h // 2 + 1) * LANE]
        qm_ref[h] = jnp.where(lo if h % 2 == 0 else ~lo, pair, jnp.zeros_like(pair))
    for h in range(IDX_HEADS):
        pair = qi_ref[0, :, (h // 2) * LANE:(h // 2 + 1) * LANE]
        qim_ref[h] = jnp.where(lo if h % 2 == 0 else ~lo, pair, jnp.zeros_like(pair))
    w_rows = [wi_ref[0, h:h + 1, :] for h in range(IDX_HEADS)]

    def score_block(kb, carry, masked):
        q_max, q_min = carry
        k0 = pl.multiple_of(kb * tk, tk)
        ik = ik2_ref[0, pl.ds(k0, tk), :]
        score = jnp.zeros((tk, tq), F32)
        for h in range(IDX_HEADS):
            score = score + jnp.maximum(_nt_dot(ik, qim_ref[h]), 0.0) * w_rows[h]
        score = jnp.where(score == 0.0, 0.0, score)
        below = above = score
        if masked:
            vis = _visible_t(q0, k0, tk, tq, s_real)
            below, above = jnp.where(vis, score, -jnp.inf), jnp.where(vis, score, jnp.inf)
        key_ref[kb] = _sortable(pltpu.bitcast(below, I32))
        return (jnp.maximum(q_max, _fold_rows(below, jnp.max)), jnp.minimum(q_min, _fold_rows(above, jnp.min)))

    carry = (jnp.full((SUBLANES, tq), -jnp.inf, F32), jnp.full((SUBLANES, tq), jnp.inf, F32))
    carry = lax.fori_loop(0, n_full, functools.partial(score_block, masked=False), carry)
    q_max, q_min = lax.fori_loop(n_full, n_vis, functools.partial(score_block, masked=True), carry)
    row_max = jnp.max(q_max, axis=0, keepdims=True)
    row_min = jnp.min(q_min, axis=0, keepdims=True)

    def count_ge(t):
        def body(kb, acc):
            return acc + _fold_rows(jnp.where(key_ref[kb] >= t, 1.0, 0.0), jnp.sum)
        return jnp.sum(lax.fori_loop(0, n_vis, body, jnp.zeros((SUBLANES, tq), F32)), axis=0, keepdims=True)

    kf = float(topk)
    q_pos = q0 + lax.broadcasted_iota(I32, (1, tq), 1)
    n_row = jnp.minimum(((q_pos >> CHUNK_SHIFT) + 1) << CHUNK_SHIFT, s_real).astype(F32)
    few = n_row < kf
    lo0 = _sortable(lax.bitcast_convert_type(row_min, I32))
    hi0 = _sortable(lax.bitcast_convert_type(row_max, I32)) + 1

    def finished(lo_k, hi_k, c_lo):
        return few | (c_lo == kf) | (hi_k == lo_k + 1)

    def probe_step(carry):
        it, _, lo_k, hi_k, c_lo, c_hi, g_lo, g_hi, last = carry
        v_lo = lax.bitcast_convert_type(_sortable(lo_k), F32)
        v_hi = lax.bitcast_convert_type(_sortable(hi_k), F32)
        a = jnp.log(c_lo * (1.0 / kf)) * g_lo
        b = jnp.log(kf / jnp.maximum(c_hi, 0.5)) * g_hi
        p = _sortable(lax.bitcast_convert_type(v_lo + (v_hi - v_lo) * (a / (a + b)), I32))
        p = jnp.where(it >= FALSE_POSITION_PROBES, (lo_k >> 1) + (hi_k >> 1) + (lo_k & hi_k & 1), p)
        p = jnp.where((it == 0) & (lo_k < 0) & (hi_k > 0), 0, p)
        p = jnp.where(lo_k == 0, 1, p)
        p = jnp.minimum(jnp.maximum(p, lo_k + 1), hi_k - 1)
        c = count_ge(p)
        open_ = ~finished(lo_k, hi_k, c_lo)
        up = open_ & (c >= kf)
        down = open_ & (c < kf)
        lo_k, c_lo = jnp.where(up, p, lo_k), jnp.where(up, c, c_lo)
        hi_k, c_hi = jnp.where(down, p, hi_k), jnp.where(down, c, c_hi)
        g_lo = jnp.where(down, jnp.where(last < 0.0, 0.5 * g_lo, 1.0), jnp.where(up, 1.0, g_lo))
        g_hi = jnp.where(up, jnp.where(last > 0.0, 0.5 * g_hi, 1.0), jnp.where(down, 1.0, g_hi))
        last = jnp.where(up, 1.0, jnp.where(down, -1.0, last))
        n_open = jnp.max(jnp.where(finished(lo_k, hi_k, c_lo), 0, 1))
        return it + 1, n_open, lo_k, hi_k, c_lo, c_hi, g_lo, g_hi, last

    ones = jnp.ones((1, tq), F32)
    init = (jnp.int32(0), jnp.max(jnp.where(finished(lo0, hi0, n_row), 0, 1)), lo0, hi0, n_row, 0.0 * ones,
            ones, ones, 0.0 * ones)
    final = lax.while_loop(lambda c: (c[1] > 0) & (c[0] < MAX_PROBES), probe_step, init)
    t, c_lo, c_hi = final[2], final[4], final[5]

    t = jnp.where(few, KEY_NEG_INF, t)
    need = jnp.where(few, 0.0, jnp.where(c_lo == kf, kf, kf - c_hi))

    heads = range(DSA_HEADS)

    def attend_block(kb, carry):
        tied_before, ms, accs = carry
        k0 = pl.multiple_of(kb * tk, tk)
        blk = key_ref[kb]
        tied = jnp.where(blk == t, 1.0, 0.0)
        tied_rank = (tied_before + _dot(tri_ref[...], tied.astype(BF16))) * tied
        cap = jnp.where((blk >= t) & (tied_rank <= need), F32_MAX, NEG)
        k_pairs = [kb_ref[0, pl.ds(k0, tk), p * LANE:(p + 1) * LANE] for p in range(DSA_HEADS // 2)]
        v_pairs = [vt_ref[0, kb, p * PAIR_ROWS:(p + 1) * PAIR_ROWS, :] for p in range(DSA_HEADS // 2)]
        scores = [_nt_dot(k_pairs[h // 2], qm_ref[h]) for h in heads]
        new = [_softmax_update_t(jnp.minimum(scores[h], cap), v_pairs[h // 2], ms[h], accs[h]) for h in heads]
        return (tied_before + jnp.sum(_fold_rows(tied, jnp.sum), axis=0, keepdims=True),
                tuple(n[0] for n in new), tuple(n[1] for n in new))

    _, _, accs = lax.fori_loop(0, n_vis, attend_block, (jnp.zeros((1, tq), F32),) + _softmax_init(len(heads), tq))
    _write_pairs_t(o_ref, accs)


def _dsa_attn(qb, qi, wi, kb, vb, ik2, *, q_off, s_real, tq, tk):
    b, t, _ = qb.shape
    s_pad = kb.shape[1]
    n_kb = s_pad // tk
    topk = min(TOPK_MAX, s_real // 4)
    qb, qi, wi = _pad_queries([qb, qi, wi], tq)
    t_pad = qb.shape[1]
    wi_t = jnp.swapaxes(wi[:, :, :SUBLANES], 1, 2)
    v_t = _key_blocks_t(vb, tk)
    tri = (lax.broadcasted_iota(I32, (tk, tk), 0) >= lax.broadcasted_iota(I32, (tk, tk), 1)).astype(BF16)
    qspec = lambda w: pl.BlockSpec((1, tq, w), lambda bi, i: (bi, i, 0))
    kspec = lambda w: pl.BlockSpec((1, s_pad, w), lambda bi, i: (bi, 0, 0), pipeline_mode=pl.Buffered(1))
    kern = functools.partial(_dsa_kernel, tq=tq, tk=tk, q_off=q_off, s_real=s_real, topk=topk)
    o_t = pl.pallas_call(
        kern, grid=(b, t_pad // tq),
        in_specs=[qspec(DSA_WIDTH), qspec(IDX_HEADS * IDX_HD),
                  pl.BlockSpec((1, SUBLANES, tq), lambda bi, i: (bi, 0, i)),
                  kspec(DSA_WIDTH),
                  pl.BlockSpec((1, n_kb, DSA_WIDTH // LANE * PAIR_ROWS, tk), lambda bi, i: (bi, 0, 0, 0),
                               pipeline_mode=pl.Buffered(1)),
                  kspec(LANE), _resident((tk, tk))],
        out_specs=pl.BlockSpec((1, DSA_WIDTH, tq), lambda bi, i: (bi, 0, i)),
        out_shape=jax.ShapeDtypeStruct((b, DSA_WIDTH, t_pad), BF16),
        scratch_shapes=[pltpu.VMEM((n_kb, tk, tq), I32),
                        pltpu.VMEM((DSA_HEADS, tq, LANE), BF16), pltpu.VMEM((IDX_HEADS, tq, LANE), BF16)],
        compiler_params=pltpu.CompilerParams(dimension_semantics=("parallel", "arbitrary"),
                                             vmem_limit_bytes=VMEM_LIMIT),
        name="dsa_attn")(qb, qi, wi_t, kb, v_t, ik2, tri)
    return jnp.swapaxes(o_t, 1, 2)[:, :t]


def _pad_cols(w, width):
    return jnp.pad(w, ((0, 0), (0, width - w.shape[1])))


def _layer_weights(p, l):
    w_in = p["w_in"][l]
    off, pieces = 0, []
    for n in (MLA_Q_LORA, MLA_KV_LORA, MLA_ROPE, DSA_WIDTH, DSA_WIDTH, DSA_WIDTH, IDX_HEADS * IDX_HD, IDX_HD, IDX_HEADS):
        pieces.append(w_in[:, off:off + n])
        off += n
    c_q, c_kv, k_r, q_b, k_b, v_b, q_i, k_i, w_i = pieces
    k_r = jnp.pad(k_r, ((0, 0), (MLA_NOPE, LANE - MLA_QK)))
    w_in_p = jnp.concatenate([c_q, c_kv, k_r, q_b, k_b, v_b, q_i, k_i, k_i, _pad_cols(w_i, LANE)], axis=1)
    assert w_in_p.shape[1] == C_END

    d_lora = p["mla_w_uq"].shape[1]
    w_uq = p["mla_w_uq"][l].reshape(d_lora, MLA_HEADS, MLA_QK)
    w_uq = jnp.pad(w_uq, ((0, 0), (0, 0), (0, LANE - MLA_QK))).reshape(d_lora, MLA_HEADS * LANE)
    w_ukv = p["mla_w_ukv"][l].reshape(MLA_KV_LORA, MLA_HEADS, MLA_NOPE + MLA_V)
    w_nope = jnp.pad(w_ukv[:, :, :MLA_NOPE], ((0, 0), (0, 0), (0, LANE - MLA_NOPE))).reshape(MLA_KV_LORA, MLA_HEADS * LANE)
    w_v = w_ukv[:, :, MLA_NOPE:].reshape(MLA_KV_LORA, MLA_WIDTH)
    w_out = p["w_out"][l]

    row = lambda g: g[l][None, :].astype(F32)
    pad96 = lambda g: jnp.pad(g[l].astype(F32), (0, LANE - MLA_QK))[None, :]
    twice = lambda g: jnp.tile(g[l].astype(F32), 2)[None, :]
    lw = {
        "w_in": w_in_p.astype(BF16), "w_uq": w_uq.astype(BF16), "w_ukv_nope": w_nope.astype(BF16),
        "w_ukv_v": w_v.astype(BF16), "w_out_a": w_out[:MLA_WIDTH].astype(BF16), "w_out_b": w_out[MLA_WIDTH:].astype(BF16),
        "mix_norm": row(p["mix_norm"]), "mla_q_norm": row(p["mla_q_norm"]), "mla_kv_norm": row(p["mla_kv_norm"]),
        "mla_q_gain": pad96(p["mla_q_gain"]), "mla_k_gain": pad96(p["mla_k_gain"]),
        "dsa_q_gain": twice(p["dsa_q_gain"]), "dsa_k_gain": twice(p["dsa_k_gain"]),
    }
    for f in ("ffn1", "ffn2"):
        lw[f + "_norm"] = row(p[f + "_norm"])
        for w in ("w_gate", "w_up", "w_down"):
            lw[f + "_" + w] = p[f + "_" + w][l].astype(BF16)
    return lw


def _rope_tables(pos, rows):
    def cs(rot):
        inv = 1.0 / (ROPE_THETA ** (jnp.arange(0, rot, 2, dtype=F32) / rot))
        ang = pos.astype(F32)[:, None] * inv[None, :]
        return jnp.cos(ang), jnp.sin(ang)

    t = pos.shape[0]
    cos_a, sin_a = cs(MLA_ROPE)
    ones = lambda w: jnp.ones((t, w), F32)
    zeros = lambda w: jnp.zeros((t, w), F32)
    ca = jnp.concatenate([ones(MLA_NOPE), cos_a, cos_a, ones(LANE - MLA_QK)], axis=1)
    sa = jnp.concatenate([zeros(MLA_NOPE), -sin_a, sin_a, zeros(LANE - MLA_QK)], axis=1)
    cos_b, sin_b = cs(DSA_ROT)
    cb = jnp.tile(jnp.concatenate([cos_b, cos_b, ones(DSA_HD - DSA_ROT)], axis=1), (1, 2))
    sb = jnp.tile(jnp.concatenate([-sin_b, sin_b, zeros(DSA_HD - DSA_ROT)], axis=1), (1, 2))
    reps = max(1, rows // t)
    return tuple(jnp.tile(x, (reps, 1)) for x in (ca, sa, cb, sb))


def _pad_keys(x, s_pad):
    return jnp.pad(x, ((0, 0), (0, s_pad - x.shape[1]), (0, 0)))


def _trunk_layer(x, lw, tables, past, *, b, t, q_off, tq_mla, tk_mla, tq_dsa, tk_dsa):
    h = _ffn(x, lw["ffn1_norm"], lw["ffn1_w_gate"], lw["ffn1_w_up"], lw["ffn1_w_down"])
    (ckv, krope, kslab, kb, vb, ki, qa, qb, kb16, vb16, qi, ik2, wi) = _proj(h, lw, tables, t)
    per_batch = lambda a: a.reshape(b, t, a.shape[-1])
    if past is None:
        ckv_all, kslab_all = ckv, kslab
        kb_all, vb_all, ik2_all = per_batch(kb16), per_batch(vb16), per_batch(ik2)
        s_real = t
    else:
        p_ckv, p_krope, p_kb, p_vb, p_ki = past
        s_real = p_ckv.shape[1] + t
        cat = lambda old, new: jnp.concatenate([old, per_batch(new)], axis=1)
        ckv_all = cat(p_ckv, ckv).reshape(b * s_real, MLA_KV_LORA)
        p_kslab = jnp.pad(p_krope, ((0, 0), (0, 0), (MLA_NOPE, LANE - MLA_QK)))
        kslab_all = cat(p_kslab, kslab).reshape(b * s_real, LANE)
        kb_all = cat(p_kb.reshape(b, -1, DSA_WIDTH).astype(BF16), kb16)
        vb_all = cat(p_vb.reshape(b, -1, DSA_WIDTH).astype(BF16), vb16)
        ik2_all = cat(jnp.tile(p_ki, (1, 1, 2)).astype(BF16), ik2)
    ka, va = _mla_kv(ckv_all, kslab_all, lw)
    s_pad_a = pl.cdiv(s_real, tk_mla) * tk_mla
    ka = _pad_keys(ka.reshape(b, s_real, -1), s_pad_a)
    va = _pad_keys(va.reshape(b, s_real, -1), s_pad_a)
    oa = _mla_attn(per_batch(qa), ka, va, q_off=q_off, s_real=s_real, tq=tq_mla, tk=tk_mla)
    s_pad_b = pl.cdiv(s_real, tk_dsa) * tk_dsa
    ob = _dsa_attn(per_batch(qb), per_batch(qi), per_batch(wi), _pad_keys(kb_all, s_pad_b), _pad_keys(vb_all, s_pad_b),
                   _pad_keys(ik2_all, s_pad_b), q_off=q_off, s_real=s_real, tq=tq_dsa, tk=tk_dsa)
    y = _ffn(h, lw["ffn2_norm"], lw["ffn2_w_gate"], lw["ffn2_w_up"], lw["ffn2_w_down"],
             attn=(oa.reshape(b * t, MLA_WIDTH), ob.reshape(b * t, DSA_WIDTH), lw["w_out_a"], lw["w_out_b"]))
    rows = (per_batch(ckv), per_batch(krope), per_batch(kb).reshape(b, t, DSA_HEADS, DSA_HD),
            per_batch(vb).reshape(b, t, DSA_HEADS, DSA_HD), per_batch(ki))
    return y, rows


def kernel(x_prompt, x_sample, cache_mla_ckv, cache_mla_krope, cache_dsa_k, cache_dsa_v, cache_idx_k,
           ffn1_norm, ffn1_w_gate, ffn1_w_up, ffn1_w_down, mix_norm, w_in,
           mla_q_norm, mla_w_uq, mla_kv_norm, mla_w_ukv, mla_q_gain, mla_k_gain,
           dsa_q_gain, dsa_k_gain, w_out, ffn2_norm, ffn2_w_gate, ffn2_w_up, ffn2_w_down):
    params = dict(ffn1_norm=ffn1_norm, ffn1_w_gate=ffn1_w_gate, ffn1_w_up=ffn1_w_up, ffn1_w_down=ffn1_w_down,
                  mix_norm=mix_norm, w_in=w_in, mla_q_norm=mla_q_norm, mla_w_uq=mla_w_uq, mla_kv_norm=mla_kv_norm,
                  mla_w_ukv=mla_w_ukv, mla_q_gain=mla_q_gain, mla_k_gain=mla_k_gain, dsa_q_gain=dsa_q_gain,
                  dsa_k_gain=dsa_k_gain, w_out=w_out, ffn2_norm=ffn2_norm, ffn2_w_gate=ffn2_w_gate,
                  ffn2_w_up=ffn2_w_up, ffn2_w_down=ffn2_w_down)
    depth = w_in.shape[0]
    d_model = x_prompt.shape[-1]
    weights = [_layer_weights(params, l) for l in range(depth)]

    b_p, t_p = x_prompt.shape[:2]
    n_p = b_p * t_p
    tabs_p = _rope_tables(jnp.arange(t_p, dtype=I32), _row_tile(n_p, 512))
    tile_p = dict(tq_mla=min(t_p, 512), tk_mla=min(t_p, 512), tq_dsa=min(t_p, 512), tk_dsa=min(t_p, 512))
    h_p = x_prompt.reshape(n_p, d_model)
    p_rows = []
    for l in range(depth):
        h_p, rows = _trunk_layer(h_p, weights[l], tabs_p, None, b=b_p, t=t_p, q_off=0, **tile_p)
        p_rows.append(rows)

    b_s, t_s = x_sample.shape[:2]
    n_s = b_s * t_s
    past_len = cache_mla_ckv.shape[2]
    tabs_s = _rope_tables(past_len + jnp.arange(t_s, dtype=I32), _row_tile(n_s, 512))
    tile_s = dict(tq_mla=LANE, tk_mla=LANE, tq_dsa=LANE, tk_dsa=2 * LANE)
    h_s = x_sample.reshape(n_s, d_model)
    s_rows = []
    for l in range(depth):
        past = (cache_mla_ckv[l], cache_mla_krope[l], cache_dsa_k[l], cache_dsa_v[l], cache_idx_k[l])
        h_s, rows = _trunk_layer(h_s, weights[l], tabs_s, past, b=b_s, t=t_s, q_off=past_len, **tile_s)
        s_rows.append(rows)

    stack = lambda rows_by_layer, i: jnp.stack([r[i] for r in rows_by_layer])
    return (h_p.reshape(b_p, t_p, d_model), h_s.reshape(b_s, t_s, d_model),
            *[stack(p_rows, i) for i in range(5)], *[stack(s_rows, i) for i in range(5)])
```

```python
import functools

import jax
import jax.numpy as jnp
from jax import lax
from jax.experimental import pallas as pl
from jax.experimental.pallas import tpu as pltpu

F32 = jnp.float32
BF16 = jnp.bfloat16
I32 = jnp.int32

CHUNK_SHIFT = 6
ROPE_THETA = 500000.0
EPS = 1e-6
MLA_HEADS = 8
MLA_NOPE = 64
MLA_ROPE = 32
MLA_QK = MLA_NOPE + MLA_ROPE
MLA_V = 64
MLA_Q_LORA = 256
MLA_KV_LORA = 128
DSA_HEADS = 8
DSA_HD = 64
DSA_ROT = 16
IDX_HEADS = 4
IDX_HD = 64
IDX_W_SCALE = (IDX_HD * IDX_HEADS) ** -0.5
TOPK_MAX = 256
DSA_WIDTH = DSA_HEADS * DSA_HD
MLA_WIDTH = MLA_HEADS * MLA_V

LANE = 128
VMEM_LIMIT = 56 * 1024 * 1024

NEG = -1e30
F32_MAX = 3.4028234e38
LOG2E = 1.4426950408889634
INT_MIN = -(2 ** 31)
INT_MAX = 2 ** 31 - 1
KEY_NEG_INF = INT_MIN + 0x7FFFFF


def _nt_dot(a, b):
    return lax.dot_general(a, b, (((1,), (1,)), ((), ())), preferred_element_type=F32)


def _dot(a, b):
    return jnp.dot(a, b, preferred_element_type=F32)


def _rms(x, g):
    return x * lax.rsqrt(jnp.mean(x * x, axis=-1, keepdims=True) + EPS) * g


def _lane_iota(shape):
    return lax.broadcasted_iota(I32, shape, len(shape) - 1)


FFN_CHUNK = 256


def _ffn_body(x, g_ref, wg_ref, wu_ref, wd_ref, o_ref):
    xb = _rms(x, g_ref[...]).astype(BF16)
    d_ff = wg_ref.shape[1]
    acc = jnp.zeros(x.shape, F32)
    for c in range(d_ff // FFN_CHUNK):
        sl = slice(c * FFN_CHUNK, (c + 1) * FFN_CHUNK)
        gate = _dot(xb, wg_ref[:, sl])
        up = _dot(xb, wu_ref[:, sl])
        act = (gate * jax.nn.sigmoid(gate) * up).astype(BF16)
        acc = acc + _dot(act, wd_ref[sl, :])
    o_ref[...] = x + 0.5 * acc


def _ffn_kernel(x_ref, g_ref, wg_ref, wu_ref, wd_ref, o_ref):
    _ffn_body(x_ref[...], g_ref, wg_ref, wu_ref, wd_ref, o_ref)


def _out_ffn_kernel(h_ref, oa_ref, ob_ref, woa_ref, wob_ref, g_ref, wg_ref, wu_ref, wd_ref, o_ref):
    x = h_ref[...] + _dot(oa_ref[...], woa_ref[...]) + _dot(ob_ref[...], wob_ref[...])
    _ffn_body(x, g_ref, wg_ref, wu_ref, wd_ref, o_ref)


def _resident(shape):
    nd = len(shape)
    return pl.BlockSpec(shape, lambda *_: (0,) * nd, pipeline_mode=pl.Buffered(1))


def _row_tile(n, pref):
    for t in range(min(n, pref), 0, -16):
        if n % t == 0:
            return t
    raise ValueError(f"no row tile for {n} rows")


def _ffn(x, g, wg, wu, wd, attn=None):
    n, d = x.shape
    tm = _row_tile(n, 512)
    row = lambda w: pl.BlockSpec((tm, w), lambda i: (i, 0))
    w_specs = [_resident(g.shape), _resident(wg.shape), _resident(wu.shape), _resident(wd.shape)]
    params = pltpu.CompilerParams(dimension_semantics=("parallel",), vmem_limit_bytes=VMEM_LIMIT)
    out_shape = jax.ShapeDtypeStruct((n, d), F32)
    if attn is None:
        return pl.pallas_call(_ffn_kernel, grid=(n // tm,), in_specs=[row(d)] + w_specs, out_specs=row(d),
                              out_shape=out_shape, compiler_params=params, name="ffn")(x, g, wg, wu, wd)
    oa, ob, woa, wob = attn
    return pl.pallas_call(
        _out_ffn_kernel, grid=(n // tm,),
        in_specs=[row(d), row(oa.shape[1]), row(ob.shape[1]), _resident(woa.shape), _resident(wob.shape)] + w_specs,
        out_specs=row(d), out_shape=out_shape, compiler_params=params, name="out_ffn")(x, oa, ob, woa, wob, g, wg, wu, wd)


C_CQ = 0
C_CKV = C_CQ + MLA_Q_LORA
C_KR = C_CKV + MLA_KV_LORA
C_QB = C_KR + LANE
C_KB = C_QB + DSA_WIDTH
C_VB = C_KB + DSA_WIDTH
C_QI = C_VB + DSA_WIDTH
C_KI = C_QI + IDX_HEADS * IDX_HD
C_WI = C_KI + LANE
C_END = C_WI + LANE


def _rope_a(x, c, s):
    lane = _lane_iota(x.shape)
    partner = jnp.where(lane < MLA_NOPE + MLA_ROPE // 2, pltpu.roll(x, LANE - MLA_ROPE // 2, 1),
                        pltpu.roll(x, MLA_ROPE // 2, 1))
    return x * c + partner * s


def _rope_b(x, c, s):
    lane = _lane_iota(x.shape)
    half = DSA_ROT // 2
    partner = jnp.where((lane & (DSA_HD - 1)) < half, pltpu.roll(x, LANE - half, 1), pltpu.roll(x, half, 1))
    return x * c + partner * s


def _head96_norm(x, g):
    ms = jnp.sum(x * x, axis=-1, keepdims=True) * (1.0 / MLA_QK)
    return x * lax.rsqrt(ms + EPS) * g


def _head64_norm(x, g2):
    lane = _lane_iota(x.shape)
    lo = lane < DSA_HD
    sq = x * x
    s_lo = jnp.sum(jnp.where(lo, sq, 0.0), axis=-1, keepdims=True)
    s_hi = jnp.sum(jnp.where(lo, 0.0, sq), axis=-1, keepdims=True)
    ms = jnp.where(lo, s_lo, s_hi) * (1.0 / DSA_HD)
    return x * lax.rsqrt(ms + EPS) * g2


def _proj_kernel(h_ref, gmix_ref, win_ref, gq_ref, wuq_ref, gkv_ref, gqa_ref, gqb_ref, gkb_ref,
                 ca_ref, sa_ref, cb_ref, sb_ref,
                 ckv_ref, krope_ref, kslab_ref, kb_ref, vb_ref, ki_ref,
                 qa_ref, qb_ref, kb16_ref, vb16_ref, qi_ref, ik2_ref, wi_ref):
    u = _rms(h_ref[...], gmix_ref[...]).astype(BF16)
    ca, sa, cb, sb = ca_ref[...], sa_ref[...], cb_ref[...], sb_ref[...]

    def cols(start, width):
        return _dot(u, win_ref[:, start:start + width])

    cq = _rms(cols(C_CQ, MLA_Q_LORA), gq_ref[...]).astype(BF16)
    qa = _dot(cq, wuq_ref[...])
    qa_scale = MLA_QK ** -0.5 * LOG2E
    for h in range(MLA_HEADS):
        sl = slice(h * LANE, (h + 1) * LANE)
        qa_ref[:, sl] = (_head96_norm(_rope_a(qa[:, sl], ca, sa), gqa_ref[...]) * qa_scale).astype(BF16)

    ckv_ref[...] = _rms(cols(C_CKV, MLA_KV_LORA), gkv_ref[...])
    kslab = _rope_a(cols(C_KR, LANE), ca, sa)
    kslab_ref[...] = kslab
    krope_ref[...] = kslab[:, MLA_NOPE:MLA_NOPE + MLA_ROPE]

    qb = cols(C_QB, DSA_WIDTH)
    kb = cols(C_KB, DSA_WIDTH)
    qb_scale = DSA_HD ** -0.5 * LOG2E
    for p in range(DSA_WIDTH // LANE):
        sl = slice(p * LANE, (p + 1) * LANE)
        qb_ref[:, sl] = (_rope_b(_head64_norm(qb[:, sl], gqb_ref[...]), cb, sb) * qb_scale).astype(BF16)
        kp = _rope_b(_head64_norm(kb[:, sl], gkb_ref[...]), cb, sb)
        kb_ref[:, sl] = kp
        kb16_ref[:, sl] = kp.astype(BF16)
    vb = cols(C_VB, DSA_WIDTH)
    vb_ref[...] = vb
    vb16_ref[...] = vb.astype(BF16)

    qi = cols(C_QI, IDX_HEADS * IDX_HD)
    for p in range(IDX_HEADS * IDX_HD // LANE):
        sl = slice(p * LANE, (p + 1) * LANE)
        qi_ref[:, sl] = _rope_b(qi[:, sl], cb, sb).astype(BF16)
    ik2 = _rope_b(cols(C_KI, LANE), cb, sb)
    ki_ref[...] = ik2[:, :IDX_HD]
    ik2_ref[...] = ik2.astype(BF16)
    wi_ref[...] = cols(C_WI, LANE) * IDX_W_SCALE


def _proj(h, lw, tables, t_seq):
    n, d = h.shape
    tm = _row_tile(n, 512)
    ca, sa, cb, sb = tables
    n_tab = ca.shape[0] // tm
    row = lambda w: pl.BlockSpec((tm, w), lambda i: (i, 0))
    tab = pl.BlockSpec((tm, LANE), lambda i: (i % n_tab, 0))
    consts = [lw["mix_norm"], lw["w_in"], lw["mla_q_norm"], lw["w_uq"], lw["mla_kv_norm"],
              lw["mla_q_gain"], lw["dsa_q_gain"], lw["dsa_k_gain"]]
    out_widths = [(MLA_KV_LORA, F32), (MLA_ROPE, F32), (LANE, F32), (DSA_WIDTH, F32), (DSA_WIDTH, F32), (IDX_HD, F32),
                  (MLA_HEADS * LANE, BF16), (DSA_WIDTH, BF16), (DSA_WIDTH, BF16), (DSA_WIDTH, BF16),
                  (IDX_HEADS * IDX_HD, BF16), (LANE, BF16), (LANE, F32)]
    return pl.pallas_call(
        _proj_kernel, grid=(n // tm,),
        in_specs=[row(d)] + [_resident(c.shape) for c in consts] + [tab] * 4,
        out_specs=[row(w) for w, _ in out_widths],
        out_shape=[jax.ShapeDtypeStruct((n, w), dt) for w, dt in out_widths],
        compiler_params=pltpu.CompilerParams(dimension_semantics=("parallel",), vmem_limit_bytes=VMEM_LIMIT),
        name="proj")(h, *consts, ca, sa, cb, sb)


def _mla_kv_kernel(ckv_ref, kslab_ref, wn_ref, wv_ref, gk_ref, ka_ref, va_ref):
    c = ckv_ref[...].astype(BF16)
    kn = _dot(c, wn_ref[...])
    kslab = kslab_ref[...]
    for h in range(MLA_HEADS):
        sl = slice(h * LANE, (h + 1) * LANE)
        ka_ref[:, sl] = _head96_norm(kn[:, sl] + kslab, gk_ref[...]).astype(BF16)
    va_ref[...] = _dot(c, wv_ref[...]).astype(BF16)


def _mla_kv(ckv, kslab, lw):
    m = ckv.shape[0]
    tm = _row_tile(m, 512)
    row = lambda w: pl.BlockSpec((tm, w), lambda i: (i, 0))
    consts = [lw["w_ukv_nope"], lw["w_ukv_v"], lw["mla_k_gain"]]
    return pl.pallas_call(
        _mla_kv_kernel, grid=(m // tm,),
        in_specs=[row(MLA_KV_LORA), row(LANE)] + [_resident(c.shape) for c in consts],
        out_specs=[row(MLA_HEADS * LANE), row(MLA_WIDTH)],
        out_shape=[jax.ShapeDtypeStruct((m, MLA_HEADS * LANE), BF16), jax.ShapeDtypeStruct((m, MLA_WIDTH), BF16)],
        compiler_params=pltpu.CompilerParams(dimension_semantics=("parallel",), vmem_limit_bytes=VMEM_LIMIT),
        name="mla_kv")(ckv, kslab, *consts)


SUBLANES = 8
FOLD_CHAINS = 4


def _visible_t(q0, k0, tk, tq, s_real):
    q_chunk = (q0 + lax.broadcasted_iota(I32, (tk, tq), 1)) >> CHUNK_SHIFT
    k_idx = k0 + lax.broadcasted_iota(I32, (tk, tq), 0)
    return ((k_idx >> CHUNK_SHIFT) <= q_chunk) & (k_idx < s_real)


def _fold_rows(x, reduce):
    groups = x.shape[0] // SUBLANES
    if groups % FOLD_CHAINS == 0 and groups > FOLD_CHAINS:
        x = reduce(x.reshape(groups // FOLD_CHAINS, FOLD_CHAINS * SUBLANES, x.shape[1]), axis=0)
        groups = FOLD_CHAINS
    return reduce(x.reshape(groups, SUBLANES, x.shape[1]), axis=0)


ONES_ROWS = 16
PAIR_ROWS = LANE + ONES_ROWS


def _softmax_update_t(s, v_t, m, acc):
    m_new = jnp.maximum(m, jnp.max(_fold_rows(s, jnp.max), axis=0, keepdims=True))
    p = jnp.exp2(s - m_new)
    return m_new, jnp.exp2(m - m_new) * acc + _dot(v_t, p.astype(BF16))


def _write_pairs_t(o_ref, accs):
    upper = lax.broadcasted_iota(I32, (LANE, accs[0].shape[1]), 0) < LANE // 2
    out = [a[:LANE] / a[LANE:LANE + 1] for a in accs]
    for p in range(len(accs) // 2):
        o_ref[0, p * LANE:(p + 1) * LANE, :] = jnp.where(upper, out[2 * p], out[2 * p + 1]).astype(o_ref.dtype)


def _pad_queries(arrays, tq):
    t = arrays[0].shape[1]
    t_pad = pl.cdiv(t, tq) * tq
    return [jnp.pad(a, ((0, 0), (0, t_pad - t), (0, 0))) for a in arrays] if t_pad != t else list(arrays)


def _visible_blocks(q0, tq, tk, s_real):
    n_full = jnp.minimum(((q0 >> CHUNK_SHIFT) + 1) << CHUNK_SHIFT, s_real) // tk
    vis_end = jnp.minimum((((q0 + tq - 1) >> CHUNK_SHIFT) + 1) << CHUNK_SHIFT, s_real)
    return n_full, (vis_end + tk - 1) // tk


def _mla_attn_kernel(q_ref, k_ref, vt_ref, o_ref, *, tq, tk, q_off, s_real):
    q0 = q_off + pl.program_id(1) * tq
    n_full, n_vis = _visible_blocks(q0, tq, tk, s_real)
    heads = range(MLA_HEADS)

    def attend_block(kb, carry, masked):
        ms, accs = carry
        k0 = pl.multiple_of(kb * tk, tk)
        scores = [_nt_dot(k_ref[0, pl.ds(k0, tk), h * LANE:(h + 1) * LANE], q_ref[0, :, h * LANE:(h + 1) * LANE])
                  for h in heads]
        if masked:
            cap = jnp.where(_visible_t(q0, k0, tk, tq, s_real), F32_MAX, NEG)
            scores = [jnp.minimum(s, cap) for s in scores]
        v_pairs = [_with_ones(vt_ref[0, kb, p * LANE:(p + 1) * LANE, :]) for p in range(MLA_HEADS // 2)]
        new = [_softmax_update_t(scores[h], v_pairs[h // 2], ms[h], accs[h]) for h in heads]
        return tuple(n[0] for n in new), tuple(n[1] for n in new)

    carry = _softmax_init(len(heads), tq)
    carry = lax.fori_loop(0, n_full, functools.partial(attend_block, masked=False), carry)
    _, accs = lax.fori_loop(n_full, n_vis, functools.partial(attend_block, masked=True), carry)
    _write_pairs_t(o_ref, accs)


def _softmax_init(n_heads, tq):
    return (tuple(jnp.full((1, tq), NEG, F32) for _ in range(n_heads)),
            tuple(jnp.zeros((PAIR_ROWS, tq), F32) for _ in range(n_heads)))


def _key_blocks_t(v, tk):
    b, s_pad, w = v.shape
    return jnp.swapaxes(v.reshape(b, s_pad // tk, tk, w), 2, 3)


def _with_ones(v_t):
    return jnp.concatenate([v_t, jnp.ones((ONES_ROWS, v_t.shape[1]), v_t.dtype)], axis=0)


def _mla_attn(qa, ka, va, *, q_off, s_real, tq, tk):
    b, t, _ = qa.shape
    s_pad = ka.shape[1]
    (qa,) = _pad_queries([qa], tq)
    t_pad = qa.shape[1]
    kern = functools.partial(_mla_attn_kernel, tq=tq, tk=tk, q_off=q_off, s_real=s_real)
    o_t = pl.pallas_call(
        kern, grid=(b, t_pad // tq),
        in_specs=[pl.BlockSpec((1, tq, MLA_HEADS * LANE), lambda bi, qi: (bi, qi, 0)),
                  pl.BlockSpec((1, s_pad, MLA_HEADS * LANE), lambda bi, qi: (bi, 0, 0), pipeline_mode=pl.Buffered(1)),
                  pl.BlockSpec((1, s_pad // tk, MLA_WIDTH, tk), lambda bi, qi: (bi, 0, 0, 0),
                               pipeline_mode=pl.Buffered(1))],
        out_specs=pl.BlockSpec((1, MLA_WIDTH, tq), lambda bi, qi: (bi, 0, qi)),
        out_shape=jax.ShapeDtypeStruct((b, MLA_WIDTH, t_pad), BF16),
        compiler_params=pltpu.CompilerParams(dimension_semantics=("parallel", "arbitrary"),
                                             vmem_limit_bytes=VMEM_LIMIT),
        name="mla_attn")(qa, ka, _key_blocks_t(va, tk))
    return jnp.swapaxes(o_t, 1, 2)[:, :t]


FALSE_POSITION_PROBES = 24
MAX_PROBES = 64


def _sortable(bits):
    return bits ^ ((bits >> 31) & INT_MAX)


def _dsa_kernel(qb_ref, qi_ref, wi_ref, kb_ref, vt_ref, ik2_ref, tri_ref, o_ref,
                key_ref, qm_ref, qim_ref, *, tq, tk, q_off, s_real, topk):
    q0 = q_off + pl.program_id(1) * tq
    n_full, n_vis = _visible_blocks(q0, tq, tk, s_real)
    lane = _lane_iota((tq, LANE))
    lo = lane < LANE // 2

    for h in range(DSA_HEADS):
        pair = qb_ref[0, :, (h // 2) * LANE:(h // 2 + 1) * LANE]
        qm_ref[h] = jnp.where(lo if h % 2 == 0 else ~lo, pair, jnp.zeros_like(pair))
    for h in range(IDX_HEADS):
        pair = qi_ref[0, :, (h // 2) * LANE:(h // 2 + 1) * LANE]
        qim_ref[h] = jnp.where(lo if h % 2 == 0 else ~lo, pair, jnp.zeros_like(pair))
    w_rows = [wi_ref[0, h:h + 1, :] for h in range(IDX_HEADS)]

    def score_block(kb, carry, masked):
        q_max, q_min = carry
        k0 = pl.multiple_of(kb * tk, tk)
        ik = ik2_ref[0, pl.ds(k0, tk), :]
        score = jnp.zeros((tk, tq), F32)
        for h in range(IDX_HEADS):
            score = score + jnp.maximum(_nt_dot(ik, qim_ref[h]), 0.0) * w_rows[h]
        score = jnp.where(score == 0.0, 0.0, score)
        below = above = score
        if masked:
            vis = _visible_t(q0, k0, tk, tq, s_real)
            below, above = jnp.where(vis, score, -jnp.inf), jnp.where(vis, score, jnp.inf)
        key_ref[kb] = _sortable(pltpu.bitcast(below, I32))
        return (jnp.maximum(q_max, _fold_rows(below, jnp.max)), jnp.minimum(q_min, _fold_rows(above, jnp.min)))

    carry = (jnp.full((SUBLANES, tq), -jnp.inf, F32), jnp.full((SUBLANES, tq), jnp.inf, F32))
    carry = lax.fori_loop(0, n_full, functools.partial(score_block, masked=False), carry)
    q_max, q_min = lax.fori_loop(n_full, n_vis, functools.partial(score_block, masked=True), carry)
    row_max = jnp.max(q_max, axis=0, keepdims=True)
    row_min = jnp.min(q_min, axis=0, keepdims=True)

    def count_ge(t):
        def body(kb, acc):
            return acc + _fold_rows(jnp.where(key_ref[kb] >= t, 1.0, 0.0), jnp.sum)
        return jnp.sum(lax.fori_loop(0, n_vis, body, jnp.zeros((SUBLANES, tq), F32)), axis=0, keepdims=True)

    kf = float(topk)
    q_pos = q0 + lax.broadcasted_iota(I32, (1, tq), 1)
    n_row = jnp.minimum(((q_pos >> CHUNK_SHIFT) + 1) << CHUNK_SHIFT, s_real).astype(F32)
    few = n_row < kf
    lo0 = _sortable(lax.bitcast_convert_type(row_min, I32))
    hi0 = _sortable(lax.bitcast_convert_type(row_max, I32)) + 1

    def finished(lo_k, hi_k, c_lo):
        return few | (c_lo == kf) | (hi_k == lo_k + 1)

    def probe_step(carry):
        it, _, lo_k, hi_k, c_lo, c_hi, g_lo, g_hi, last = carry
        v_lo = lax.bitcast_convert_type(_sortable(lo_k), F32)
        v_hi = lax.bitcast_convert_type(_sortable(hi_k), F32)
        a = jnp.log(c_lo * (1.0 / kf)) * g_lo
        b = jnp.log(kf / jnp.maximum(c_hi, 0.5)) * g_hi
        p = _sortable(lax.bitcast_convert_type(v_lo + (v_hi - v_lo) * (a / (a + b)), I32))
        p = jnp.where(it >= FALSE_POSITION_PROBES, (lo_k >> 1) + (hi_k >> 1) + (lo_k & hi_k & 1), p)
        p = jnp.where((it == 0) & (lo_k < 0) & (hi_k > 0), 0, p)
        p = jnp.where(lo_k == 0, 1, p)
        p = jnp.minimum(jnp.maximum(p, lo_k + 1), hi_k - 1)
        c = count_ge(p)
        open_ = ~finished(lo_k, hi_k, c_lo)
        up = open_ & (c >= kf)
        down = open_ & (c < kf)
        lo_k, c_lo = jnp.where(up, p, lo_k), jnp.where(up, c, c_lo)
        hi_k, c_hi = jnp.where(down, p, hi_k), jnp.where(down, c, c_hi)
        g_lo = jnp.where(down, jnp.where(last < 0.0, 0.5 * g_lo, 1.0), jnp.where(up, 1.0, g_lo))
        g_hi = jnp.where(up, jnp.where(last > 0.0, 0.5 * g_hi, 1.0), jnp.where(down, 1.0, g_hi))
        last = jnp.where(up, 1.0, jnp.where(down, -1.0, last))
        n_open = jnp.max(jnp.where(finished(lo_k, hi_k, c_lo), 0, 1))
        return it + 1, n_open, lo_k, hi_k, c_lo, c_hi, g_lo, g_hi, last

    ones = jnp.ones((1, tq), F32)
    init = (jnp.int32(0), jnp.max(jnp.where(finished(lo0, hi0, n_row), 0, 1)), lo0, hi0, n_row, 0.0 * ones,
            ones, ones, 0.0 * ones)
    final = lax.while_loop(lambda c: (c[1] > 0) & (c[0] < MAX_PROBES), probe_step, init)
    t, c_lo, c_hi = final[2], final[4], final[5]

    t = jnp.where(few, KEY_NEG_INF, t)
    need = jnp.where(few, 0.0, jnp.where(c_lo == kf, kf, kf - c_hi))

    heads = range(DSA_HEADS)

    def attend_block(kb, carry):
        tied_before, ms, accs = carry
        k0 = pl.multiple_of(kb * tk, tk)
        blk = key_ref[kb]
        tied = jnp.where(blk == t, 1.0, 0.0)
        tied_rank = (tied_before + _dot(tri_ref[...], tied.astype(BF16))) * tied
        cap = jnp.where((blk >= t) & (tied_rank <= need), F32_MAX, NEG)
        k_pairs = [kb_ref[0, pl.ds(k0, tk), p * LANE:(p + 1) * LANE] for p in range(DSA_HEADS // 2)]
        v_pairs = [_with_ones(vt_ref[0, kb, p * LANE:(p + 1) * LANE, :]) for p in range(DSA_HEADS // 2)]
        scores = [_nt_dot(k_pairs[h // 2], qm_ref[h]) for h in heads]
        new = [_softmax_update_t(jnp.minimum(scores[h], cap), v_pairs[h // 2], ms[h], accs[h]) for h in heads]
        return (tied_before + jnp.sum(_fold_rows(tied, jnp.sum), axis=0, keepdims=True),
                tuple(n[0] for n in new), tuple(n[1] for n in new))

    _, _, accs = lax.fori_loop(0, n_vis, attend_block, (jnp.zeros((1, tq), F32),) + _softmax_init(len(heads), tq))
    _write_pairs_t(o_ref, accs)


def _dsa_attn(qb, qi, wi, kb, vb, ik2, *, q_off, s_real, tq, tk):
    b, t, _ = qb.shape
    s_pad = kb.shape[1]
    n_kb = s_pad // tk
    topk = min(TOPK_MAX, s_real // 4)
    qb, qi, wi = _pad_queries([qb, qi, wi], tq)
    t_pad = qb.shape[1]
    wi_t = jnp.swapaxes(wi[:, :, :SUBLANES], 1, 2)
    v_t = _key_blocks_t(vb, tk)
    tri = (lax.broadcasted_iota(I32, (tk, tk), 0) >= lax.broadcasted_iota(I32, (tk, tk), 1)).astype(BF16)
    qspec = lambda w: pl.BlockSpec((1, tq, w), lambda bi, i: (bi, i, 0))
    kspec = lambda w: pl.BlockSpec((1, s_pad, w), lambda bi, i: (bi, 0, 0), pipeline_mode=pl.Buffered(1))
    kern = functools.partial(_dsa_kernel, tq=tq, tk=tk, q_off=q_off, s_real=s_real, topk=topk)
    o_t = pl.pallas_call(
        kern, grid=(b, t_pad // tq),
        in_specs=[qspec(DSA_WIDTH), qspec(IDX_HEADS * IDX_HD),
                  pl.BlockSpec((1, SUBLANES, tq), lambda bi, i: (bi, 0, i)),
                  kspec(DSA_WIDTH),
                  pl.BlockSpec((1, n_kb, DSA_WIDTH, tk), lambda bi, i: (bi, 0, 0, 0), pipeline_mode=pl.Buffered(1)),
                  kspec(LANE), _resident((tk, tk))],
        out_specs=pl.BlockSpec((1, DSA_WIDTH, tq), lambda bi, i: (bi, 0, i)),
        out_shape=jax.ShapeDtypeStruct((b, DSA_WIDTH, t_pad), BF16),
        scratch_shapes=[pltpu.VMEM((n_kb, tk, tq), I32),
                        pltpu.VMEM((DSA_HEADS, tq, LANE), BF16), pltpu.VMEM((IDX_HEADS, tq, LANE), BF16)],
        compiler_params=pltpu.CompilerParams(dimension_semantics=("parallel", "arbitrary"),
                                             vmem_limit_bytes=VMEM_LIMIT),
        name="dsa_attn")(qb, qi, wi_t, kb, v_t, ik2, tri)
    return jnp.swapaxes(o_t, 1, 2)[:, :t]


def _pad_cols(w, width):
    return jnp.pad(w, ((0, 0), (0, width - w.shape[1])))


def _layer_weights(p, l):
    w_in = p["w_in"][l]
    off, pieces = 0, []
    for n in (MLA_Q_LORA, MLA_KV_LORA, MLA_ROPE, DSA_WIDTH, DSA_WIDTH, DSA_WIDTH, IDX_HEADS * IDX_HD, IDX_HD, IDX_HEADS):
        pieces.append(w_in[:, off:off + n])
        off += n
    c_q, c_kv, k_r, q_b, k_b, v_b, q_i, k_i, w_i = pieces
    k_r = jnp.pad(k_r, ((0, 0), (MLA_NOPE, LANE - MLA_QK)))
    w_in_p = jnp.concatenate([c_q, c_kv, k_r, q_b, k_b, v_b, q_i, k_i, k_i, _pad_cols(w_i, LANE)], axis=1)
    assert w_in_p.shape[1] == C_END

    d_lora = p["mla_w_uq"].shape[1]
    w_uq = p["mla_w_uq"][l].reshape(d_lora, MLA_HEADS, MLA_QK)
    w_uq = jnp.pad(w_uq, ((0, 0), (0, 0), (0, LANE - MLA_QK))).reshape(d_lora, MLA_HEADS * LANE)
    w_ukv = p["mla_w_ukv"][l].reshape(MLA_KV_LORA, MLA_HEADS, MLA_NOPE + MLA_V)
    w_nope = jnp.pad(w_ukv[:, :, :MLA_NOPE], ((0, 0), (0, 0), (0, LANE - MLA_NOPE))).reshape(MLA_KV_LORA, MLA_HEADS * LANE)
    w_v = w_ukv[:, :, MLA_NOPE:].reshape(MLA_KV_LORA, MLA_WIDTH)
    w_out = p["w_out"][l]

    row = lambda g: g[l][None, :].astype(F32)
    pad96 = lambda g: jnp.pad(g[l].astype(F32), (0, LANE - MLA_QK))[None, :]
    twice = lambda g: jnp.tile(g[l].astype(F32), 2)[None, :]
    lw = {
        "w_in": w_in_p.astype(BF16), "w_uq": w_uq.astype(BF16), "w_ukv_nope": w_nope.astype(BF16),
        "w_ukv_v": w_v.astype(BF16), "w_out_a": w_out[:MLA_WIDTH].astype(BF16), "w_out_b": w_out[MLA_WIDTH:].astype(BF16),
        "mix_norm": row(p["mix_norm"]), "mla_q_norm": row(p["mla_q_norm"]), "mla_kv_norm": row(p["mla_kv_norm"]),
        "mla_q_gain": pad96(p["mla_q_gain"]), "mla_k_gain": pad96(p["mla_k_gain"]),
        "dsa_q_gain": twice(p["dsa_q_gain"]), "dsa_k_gain": twice(p["dsa_k_gain"]),
    }
    for f in ("ffn1", "ffn2"):
        lw[f + "_norm"] = row(p[f + "_norm"])
        for w in ("w_gate", "w_up", "w_down"):
            lw[f + "_" + w] = p[f + "_" + w][l].astype(BF16)
    return lw


def _rope_tables(pos, rows):
    def cs(rot):
        inv = 1.0 / (ROPE_THETA ** (jnp.arange(0, rot, 2, dtype=F32) / rot))
        ang = pos.astype(F32)[:, None] * inv[None, :]
        return jnp.cos(ang), jnp.sin(ang)

    t = pos.shape[0]
    cos_a, sin_a = cs(MLA_ROPE)
    ones = lambda w: jnp.ones((t, w), F32)
    zeros = lambda w: jnp.zeros((t, w), F32)
    ca = jnp.concatenate([ones(MLA_NOPE), cos_a, cos_a, ones(LANE - MLA_QK)], axis=1)
    sa = jnp.concatenate([zeros(MLA_NOPE), -sin_a, sin_a, zeros(LANE - MLA_QK)], axis=1)
    cos_b, sin_b = cs(DSA_ROT)
    cb = jnp.tile(jnp.concatenate([cos_b, cos_b, ones(DSA_HD - DSA_ROT)], axis=1), (1, 2))
    sb = jnp.tile(jnp.concatenate([-sin_b, sin_b, zeros(DSA_HD - DSA_ROT)], axis=1), (1, 2))
    reps = max(1, rows // t)
    return tuple(jnp.tile(x, (reps, 1)) for x in (ca, sa, cb, sb))


def _pad_keys(x, s_pad):
    return jnp.pad(x, ((0, 0), (0, s_pad - x.shape[1]), (0, 0)))


def _trunk_layer(x, lw, tables, past, *, b, t, q_off, tq_mla, tk_mla, tq_dsa, tk_dsa):
    h = _ffn(x, lw["ffn1_norm"], lw["ffn1_w_gate"], lw["ffn1_w_up"], lw["ffn1_w_down"])
    (ckv, krope, kslab, kb, vb, ki, qa, qb, kb16, vb16, qi, ik2, wi) = _proj(h, lw, tables, t)
    per_batch = lambda a: a.reshape(b, t, a.shape[-1])
    if past is None:
        ckv_all, kslab_all = ckv, kslab
        kb_all, vb_all, ik2_all = per_batch(kb16), per_batch(vb16), per_batch(ik2)
        s_real = t
    else:
        p_ckv, p_krope, p_kb, p_vb, p_ki = past
        s_real = p_ckv.shape[1] + t
        cat = lambda old, new: jnp.concatenate([old, per_batch(new)], axis=1)
        ckv_all = cat(p_ckv, ckv).reshape(b * s_real, MLA_KV_LORA)
        p_kslab = jnp.pad(p_krope, ((0, 0), (0, 0), (MLA_NOPE, LANE - MLA_QK)))
        kslab_all = cat(p_kslab, kslab).reshape(b * s_real, LANE)
        kb_all = cat(p_kb.reshape(b, -1, DSA_WIDTH).astype(BF16), kb16)
        vb_all = cat(p_vb.reshape(b, -1, DSA_WIDTH).astype(BF16), vb16)
        ik2_all = cat(jnp.tile(p_ki, (1, 1, 2)).astype(BF16), ik2)
    ka, va = _mla_kv(ckv_all, kslab_all, lw)
    s_pad_a = pl.cdiv(s_real, tk_mla) * tk_mla
    ka = _pad_keys(ka.reshape(b, s_real, -1), s_pad_a)
    va = _pad_keys(va.reshape(b, s_real, -1), s_pad_a)
    oa = _mla_attn(per_batch(qa), ka, va, q_off=q_off, s_real=s_real, tq=tq_mla, tk=tk_mla)
    s_pad_b = pl.cdiv(s_real, tk_dsa) * tk_dsa
    ob = _dsa_attn(per_batch(qb), per_batch(qi), per_batch(wi), _pad_keys(kb_all, s_pad_b), _pad_keys(vb_all, s_pad_b),
                   _pad_keys(ik2_all, s_pad_b), q_off=q_off, s_real=s_real, tq=tq_dsa, tk=tk_dsa)
    y = _ffn(h, lw["ffn2_norm"], lw["ffn2_w_gate"], lw["ffn2_w_up"], lw["ffn2_w_down"],
             attn=(oa.reshape(b * t, MLA_WIDTH), ob.reshape(b * t, DSA_WIDTH), lw["w_out_a"], lw["w_out_b"]))
    rows = (per_batch(ckv), per_batch(krope), per_batch(kb).reshape(b, t, DSA_HEADS, DSA_HD),
            per_batch(vb).reshape(b, t, DSA_HEADS, DSA_HD), per_batch(ki))
    return y, rows


def kernel(x_prompt, x_sample, cache_mla_ckv, cache_mla_krope, cache_dsa_k, cache_dsa_v, cache_idx_k,
           ffn1_norm, ffn1_w_gate, ffn1_w_up, ffn1_w_down, mix_norm, w_in,
           mla_q_norm, mla_w_uq, mla_kv_norm, mla_w_ukv, mla_q_gain, mla_k_gain,
           dsa_q_gain, dsa_k_gain, w_out, ffn2_norm, ffn2_w_gate, ffn2_w_up, ffn2_w_down):
    params = dict(ffn1_norm=ffn1_norm, ffn1_w_gate=ffn1_w_gate, ffn1_w_up=ffn1_w_up, ffn1_w_down=ffn1_w_down,
                  mix_norm=mix_norm, w_in=w_in, mla_q_norm=mla_q_norm, mla_w_uq=mla_w_uq, mla_kv_norm=mla_kv_norm,
                  mla_w_ukv=mla_w_ukv, mla_q_gain=mla_q_gain, mla_k_gain=mla_k_gain, dsa_q_gain=dsa_q_gain,
                  dsa_k_gain=dsa_k_gain, w_out=w_out, ffn2_norm=ffn2_norm, ffn2_w_gate=ffn2_w_gate,
                  ffn2_w_up=ffn2_w_up, ffn2_w_down=ffn2_w_down)
    depth = w_in.shape[0]
    d_model = x_prompt.shape[-1]
    weights = [_layer_weights(params, l) for l in range(depth)]

    b_p, t_p = x_prompt.shape[:2]
    n_p = b_p * t_p
    tabs_p = _rope_tables(jnp.arange(t_p, dtype=I32), _row_tile(n_p, 512))
    tile_p = dict(tq_mla=min(t_p, 512), tk_mla=min(t_p, 512), tq_dsa=min(t_p, 512), tk_dsa=min(t_p, 512))
    h_p = x_prompt.reshape(n_p, d_model)
    p_rows = []
    for l in range(depth):
        h_p, rows = _trunk_layer(h_p, weights[l], tabs_p, None, b=b_p, t=t_p, q_off=0, **tile_p)
        p_rows.append(rows)

    b_s, t_s = x_sample.shape[:2]
    n_s = b_s * t_s
    past_len = cache_mla_ckv.shape[2]
    tabs_s = _rope_tables(past_len + jnp.arange(t_s, dtype=I32), _row_tile(n_s, 512))
    tile_s = dict(tq_mla=LANE, tk_mla=3 * LANE, tq_dsa=LANE, tk_dsa=3 * LANE)
    h_s = x_sample.reshape(n_s, d_model)
    s_rows = []
    for l in range(depth):
        past = (cache_mla_ckv[l], cache_mla_krope[l], cache_dsa_k[l], cache_dsa_v[l], cache_idx_k[l])
        h_s, rows = _trunk_layer(h_s, weights[l], tabs_s, past, b=b_s, t=t_s, q_off=past_len, **tile_s)
        s_rows.append(rows)

    stack = lambda rows_by_layer, i: jnp.stack([r[i] for r in rows_by_layer])
    return (h_p.reshape(b_p, t_p, d_model), h_s.reshape(b_s, t_s, d_model),
            *[stack(p_rows, i) for i in range(5)], *[stack(s_rows, i) for i in range(5)])
```

```python
import functools

import jax
import jax.numpy as jnp
from jax import lax
from jax.experimental import pallas as pl
from jax.experimental.pallas import tpu as pltpu

F32 = jnp.float32
BF16 = jnp.bfloat16
I32 = jnp.int32

CHUNK_SHIFT = 6
ROPE_THETA = 500000.0
EPS = 1e-6
MLA_HEADS = 8
MLA_NOPE = 64
MLA_ROPE = 32
MLA_QK = MLA_NOPE + MLA_ROPE
MLA_V = 64
MLA_Q_LORA = 256
MLA_KV_LORA = 128
DSA_HEADS = 8
DSA_HD = 64
DSA_ROT = 16
IDX_HEADS = 4
IDX_HD = 64
IDX_W_SCALE = (IDX_HD * IDX_HEADS) ** -0.5
TOPK_MAX = 256
DSA_WIDTH = DSA_HEADS * DSA_HD
MLA_WIDTH = MLA_HEADS * MLA_V

LANE = 128
VMEM_LIMIT = 56 * 1024 * 1024

NEG = -1e30
F32_MAX = 3.4028234e38
LOG2E = 1.4426950408889634
INT_MIN = -(2 ** 31)
INT_MAX = 2 ** 31 - 1
KEY_NEG_INF = INT_MIN + 0x7FFFFF


def _nt_dot(a, b):
    return lax.dot_general(a, b, (((1,), (1,)), ((), ())), preferred_element_type=F32)


def _dot(a, b):
    return jnp.dot(a, b, preferred_element_type=F32)


def _rms(x, g):
    return x * lax.rsqrt(jnp.mean(x * x, axis=-1, keepdims=True) + EPS) * g


def _lane_iota(shape):
    return lax.broadcasted_iota(I32, shape, len(shape) - 1)


FFN_CHUNK = 256


def _ffn_body(x, g_ref, wg_ref, wu_ref, wd_ref, o_ref):
    xb = _rms(x, g_ref[...]).astype(BF16)
    d_ff = wg_ref.shape[1]
    acc = jnp.zeros(x.shape, F32)
    for c in range(d_ff // FFN_CHUNK):
        sl = slice(c * FFN_CHUNK, (c + 1) * FFN_CHUNK)
        gate = _dot(xb, wg_ref[:, sl])
        up = _dot(xb, wu_ref[:, sl])
        act = (gate * jax.nn.sigmoid(gate) * up).astype(BF16)
        acc = acc + _dot(act, wd_ref[sl, :])
    o_ref[...] = x + 0.5 * acc


def _ffn_kernel(x_ref, g_ref, wg_ref, wu_ref, wd_ref, o_ref):
    _ffn_body(x_ref[...], g_ref, wg_ref, wu_ref, wd_ref, o_ref)


def _out_ffn_kernel(h_ref, oa_ref, ob_ref, woa_ref, wob_ref, g_ref, wg_ref, wu_ref, wd_ref, o_ref):
    x = h_ref[...] + _dot(oa_ref[...], woa_ref[...]) + _dot(ob_ref[...], wob_ref[...])
    _ffn_body(x, g_ref, wg_ref, wu_ref, wd_ref, o_ref)


def _resident(shape):
    nd = len(shape)
    return pl.BlockSpec(shape, lambda *_: (0,) * nd, pipeline_mode=pl.Buffered(1))


def _row_tile(n, pref):
    for t in range(min(n, pref), 0, -16):
        if n % t == 0:
            return t
    raise ValueError(f"no row tile for {n} rows")


def _ffn(x, g, wg, wu, wd, attn=None):
    n, d = x.shape
    tm = _row_tile(n, 512)
    row = lambda w: pl.BlockSpec((tm, w), lambda i: (i, 0))
    w_specs = [_resident(g.shape), _resident(wg.shape), _resident(wu.shape), _resident(wd.shape)]
    params = pltpu.CompilerParams(dimension_semantics=("parallel",), vmem_limit_bytes=VMEM_LIMIT)
    out_shape = jax.ShapeDtypeStruct((n, d), F32)
    if attn is None:
        return pl.pallas_call(_ffn_kernel, grid=(n // tm,), in_specs=[row(d)] + w_specs, out_specs=row(d),
                              out_shape=out_shape, compiler_params=params, name="ffn")(x, g, wg, wu, wd)
    oa, ob, woa, wob = attn
    return pl.pallas_call(
        _out_ffn_kernel, grid=(n // tm,),
        in_specs=[row(d), row(oa.shape[1]), row(ob.shape[1]), _resident(woa.shape), _resident(wob.shape)] + w_specs,
        out_specs=row(d), out_shape=out_shape, compiler_params=params, name="out_ffn")(x, oa, ob, woa, wob, g, wg, wu, wd)


C_CQ = 0
C_CKV = C_CQ + MLA_Q_LORA
C_KR = C_CKV + MLA_KV_LORA
C_QB = C_KR + LANE
C_KB = C_QB + DSA_WIDTH
C_VB = C_KB + DSA_WIDTH
C_QI = C_VB + DSA_WIDTH
C_KI = C_QI + IDX_HEADS * IDX_HD
C_WI = C_KI + LANE
C_END = C_WI + LANE


def _rope_a(x, c, s):
    lane = _lane_iota(x.shape)
    partner = jnp.where(lane < MLA_NOPE + MLA_ROPE // 2, pltpu.roll(x, LANE - MLA_ROPE // 2, 1),
                        pltpu.roll(x, MLA_ROPE // 2, 1))
    return x * c + partner * s


def _rope_b(x, c, s):
    lane = _lane_iota(x.shape)
    half = DSA_ROT // 2
    partner = jnp.where((lane & (DSA_HD - 1)) < half, pltpu.roll(x, LANE - half, 1), pltpu.roll(x, half, 1))
    return x * c + partner * s


def _head96_norm(x, g):
    ms = jnp.sum(x * x, axis=-1, keepdims=True) * (1.0 / MLA_QK)
    return x * lax.rsqrt(ms + EPS) * g


def _head64_norm(x, g2):
    lane = _lane_iota(x.shape)
    lo = lane < DSA_HD
    sq = x * x
    s_lo = jnp.sum(jnp.where(lo, sq, 0.0), axis=-1, keepdims=True)
    s_hi = jnp.sum(jnp.where(lo, 0.0, sq), axis=-1, keepdims=True)
    ms = jnp.where(lo, s_lo, s_hi) * (1.0 / DSA_HD)
    return x * lax.rsqrt(ms + EPS) * g2


def _proj_kernel(h_ref, gmix_ref, win_ref, gq_ref, wuq_ref, gkv_ref, gqa_ref, gqb_ref, gkb_ref,
                 ca_ref, sa_ref, cb_ref, sb_ref,
                 ckv_ref, krope_ref, kslab_ref, kb_ref, vb_ref, ki_ref,
                 qa_ref, qb_ref, kb16_ref, vb16_ref, qi_ref, ik2_ref, wi_ref):
    u = _rms(h_ref[...], gmix_ref[...]).astype(BF16)
    ca, sa, cb, sb = ca_ref[...], sa_ref[...], cb_ref[...], sb_ref[...]

    def cols(start, width):
        return _dot(u, win_ref[:, start:start + width])

    cq = _rms(cols(C_CQ, MLA_Q_LORA), gq_ref[...]).astype(BF16)
    qa = _dot(cq, wuq_ref[...])
    qa_scale = MLA_QK ** -0.5 * LOG2E
    for h in range(MLA_HEADS):
        sl = slice(h * LANE, (h + 1) * LANE)
        qa_ref[:, sl] = (_head96_norm(_rope_a(qa[:, sl], ca, sa), gqa_ref[...]) * qa_scale).astype(BF16)

    ckv_ref[...] = _rms(cols(C_CKV, MLA_KV_LORA), gkv_ref[...])
    kslab = _rope_a(cols(C_KR, LANE), ca, sa)
    kslab_ref[...] = kslab
    krope_ref[...] = kslab[:, MLA_NOPE:MLA_NOPE + MLA_ROPE]

    qb = cols(C_QB, DSA_WIDTH)
    kb = cols(C_KB, DSA_WIDTH)
    qb_scale = DSA_HD ** -0.5 * LOG2E
    for p in range(DSA_WIDTH // LANE):
        sl = slice(p * LANE, (p + 1) * LANE)
        qb_ref[:, sl] = (_rope_b(_head64_norm(qb[:, sl], gqb_ref[...]), cb, sb) * qb_scale).astype(BF16)
        kp = _rope_b(_head64_norm(kb[:, sl], gkb_ref[...]), cb, sb)
        kb_ref[:, 2 * p:2 * p + 2, :] = kp.reshape(kp.shape[0], 2, DSA_HD)
        kb16_ref[:, sl] = kp.astype(BF16)
    vb = cols(C_VB, DSA_WIDTH)
    vb_ref[...] = vb.reshape(vb.shape[0], DSA_HEADS, DSA_HD)
    vb16_ref[...] = vb.astype(BF16)

    qi = cols(C_QI, IDX_HEADS * IDX_HD)
    for p in range(IDX_HEADS * IDX_HD // LANE):
        sl = slice(p * LANE, (p + 1) * LANE)
        qi_ref[:, sl] = _rope_b(qi[:, sl], cb, sb).astype(BF16)
    ik2 = _rope_b(cols(C_KI, LANE), cb, sb)
    ki_ref[...] = ik2[:, :IDX_HD]
    ik2_ref[...] = ik2.astype(BF16)
    wi_ref[...] = cols(C_WI, LANE) * IDX_W_SCALE


def _proj(h, lw, tables, t_seq):
    n, d = h.shape
    tm = _row_tile(n, 512)
    ca, sa, cb, sb = tables
    n_tab = ca.shape[0] // tm
    row = lambda w: pl.BlockSpec((tm, w), lambda i: (i, 0))
    tab = pl.BlockSpec((tm, LANE), lambda i: (i % n_tab, 0))
    consts = [lw["mix_norm"], lw["w_in"], lw["mla_q_norm"], lw["w_uq"], lw["mla_kv_norm"],
              lw["mla_q_gain"], lw["dsa_q_gain"], lw["dsa_k_gain"]]
    out_widths = [(MLA_KV_LORA, F32), (MLA_ROPE, F32), (LANE, F32), (DSA_WIDTH, F32), (DSA_WIDTH, F32), (IDX_HD, F32),
                  (MLA_HEADS * LANE, BF16), (DSA_WIDTH, BF16), (DSA_WIDTH, BF16), (DSA_WIDTH, BF16),
                  (IDX_HEADS * IDX_HD, BF16), (LANE, BF16), (LANE, F32)]
    per_head = (3, 4)
    heads_spec = pl.BlockSpec((tm, DSA_HEADS, DSA_HD), lambda i: (i, 0, 0))
    return pl.pallas_call(
        _proj_kernel, grid=(n // tm,),
        in_specs=[row(d)] + [_resident(c.shape) for c in consts] + [tab] * 4,
        out_specs=[heads_spec if i in per_head else row(w) for i, (w, _) in enumerate(out_widths)],
        out_shape=[jax.ShapeDtypeStruct((n, DSA_HEADS, DSA_HD) if i in per_head else (n, w), dt)
                   for i, (w, dt) in enumerate(out_widths)],
        compiler_params=pltpu.CompilerParams(dimension_semantics=("parallel",), vmem_limit_bytes=VMEM_LIMIT),
        name="proj")(h, *consts, ca, sa, cb, sb)


def _mla_kv_kernel(ckv_ref, kslab_ref, wn_ref, wv_ref, gk_ref, ka_ref, va_ref):
    c = ckv_ref[...].astype(BF16)
    kn = _dot(c, wn_ref[...])
    kslab = kslab_ref[...]
    for h in range(MLA_HEADS):
        sl = slice(h * LANE, (h + 1) * LANE)
        ka_ref[:, sl] = _head96_norm(kn[:, sl] + kslab, gk_ref[...]).astype(BF16)
    va_ref[...] = _dot(c, wv_ref[...]).astype(BF16)


def _mla_kv(ckv, kslab, lw):
    m = ckv.shape[0]
    tm = _row_tile(m, 512)
    row = lambda w: pl.BlockSpec((tm, w), lambda i: (i, 0))
    consts = [lw["w_ukv_nope"], lw["w_ukv_v"], lw["mla_k_gain"]]
    return pl.pallas_call(
        _mla_kv_kernel, grid=(m // tm,),
        in_specs=[row(MLA_KV_LORA), row(LANE)] + [_resident(c.shape) for c in consts],
        out_specs=[row(MLA_HEADS * LANE), row(MLA_WIDTH)],
        out_shape=[jax.ShapeDtypeStruct((m, MLA_HEADS * LANE), BF16), jax.ShapeDtypeStruct((m, MLA_WIDTH), BF16)],
        compiler_params=pltpu.CompilerParams(dimension_semantics=("parallel",), vmem_limit_bytes=VMEM_LIMIT),
        name="mla_kv")(ckv, kslab, *consts)


SUBLANES = 8
FOLD_CHAINS = 4


def _visible_t(q0, k0, tk, tq, s_real):
    q_chunk = (q0 + lax.broadcasted_iota(I32, (tk, tq), 1)) >> CHUNK_SHIFT
    k_idx = k0 + lax.broadcasted_iota(I32, (tk, tq), 0)
    return ((k_idx >> CHUNK_SHIFT) <= q_chunk) & (k_idx < s_real)


def _fold_rows(x, reduce):
    groups = x.shape[0] // SUBLANES
    if groups % FOLD_CHAINS == 0 and groups > FOLD_CHAINS:
        x = reduce(x.reshape(groups // FOLD_CHAINS, FOLD_CHAINS * SUBLANES, x.shape[1]), axis=0)
        groups = FOLD_CHAINS
    return reduce(x.reshape(groups, SUBLANES, x.shape[1]), axis=0)


ONES_ROWS = 16
PAIR_ROWS = LANE + ONES_ROWS


def _softmax_update_t(s, v_t, m, acc):
    m_new = jnp.maximum(m, jnp.max(_fold_rows(s, jnp.max), axis=0, keepdims=True))
    p = jnp.exp2(s - m_new)
    return m_new, jnp.exp2(m - m_new) * acc + _dot(v_t, p.astype(BF16))


def _write_pairs_t(o_ref, accs):
    upper = lax.broadcasted_iota(I32, (LANE, accs[0].shape[1]), 0) < LANE // 2
    out = [a[:LANE] / a[LANE:LANE + 1] for a in accs]
    for p in range(len(accs) // 2):
        o_ref[0, p * LANE:(p + 1) * LANE, :] = jnp.where(upper, out[2 * p], out[2 * p + 1]).astype(o_ref.dtype)


def _pad_queries(arrays, tq):
    t = arrays[0].shape[1]
    t_pad = pl.cdiv(t, tq) * tq
    return [jnp.pad(a, ((0, 0), (0, t_pad - t), (0, 0))) for a in arrays] if t_pad != t else list(arrays)


def _visible_blocks(q0, tq, tk, s_real):
    n_full = jnp.minimum(((q0 >> CHUNK_SHIFT) + 1) << CHUNK_SHIFT, s_real) // tk
    vis_end = jnp.minimum((((q0 + tq - 1) >> CHUNK_SHIFT) + 1) << CHUNK_SHIFT, s_real)
    return n_full, (vis_end + tk - 1) // tk


def _mla_attn_kernel(q_ref, k_ref, vt_ref, o_ref, *, tq, tk, q_off, s_real):
    q0 = q_off + pl.program_id(1) * tq
    n_full, n_vis = _visible_blocks(q0, tq, tk, s_real)
    heads = range(MLA_HEADS)

    def attend_block(kb, carry, masked):
        ms, accs = carry
        k0 = pl.multiple_of(kb * tk, tk)
        scores = [_nt_dot(k_ref[0, pl.ds(k0, tk), h * LANE:(h + 1) * LANE], q_ref[0, :, h * LANE:(h + 1) * LANE])
                  for h in heads]
        if masked:
            cap = jnp.where(_visible_t(q0, k0, tk, tq, s_real), F32_MAX, NEG)
            scores = [jnp.minimum(s, cap) for s in scores]
        v_pairs = [_with_ones(vt_ref[0, kb, p * LANE:(p + 1) * LANE, :]) for p in range(MLA_HEADS // 2)]
        new = [_softmax_update_t(scores[h], v_pairs[h // 2], ms[h], accs[h]) for h in heads]
        return tuple(n[0] for n in new), tuple(n[1] for n in new)

    carry = _softmax_init(len(heads), tq)
    carry = lax.fori_loop(0, n_full, functools.partial(attend_block, masked=False), carry)
    _, accs = lax.fori_loop(n_full, n_vis, functools.partial(attend_block, masked=True), carry)
    _write_pairs_t(o_ref, accs)


def _softmax_init(n_heads, tq):
    return (tuple(jnp.full((1, tq), NEG, F32) for _ in range(n_heads)),
            tuple(jnp.zeros((PAIR_ROWS, tq), F32) for _ in range(n_heads)))


def _key_blocks_t(v, tk):
    b, s_pad, w = v.shape
    return jnp.swapaxes(v.reshape(b, s_pad // tk, tk, w), 2, 3)


def _with_ones(v_t):
    return jnp.concatenate([v_t, jnp.ones((ONES_ROWS, v_t.shape[1]), v_t.dtype)], axis=0)


def _mla_attn(qa, ka, va, *, q_off, s_real, tq, tk):
    b, t, _ = qa.shape
    s_pad = ka.shape[1]
    (qa,) = _pad_queries([qa], tq)
    t_pad = qa.shape[1]
    kern = functools.partial(_mla_attn_kernel, tq=tq, tk=tk, q_off=q_off, s_real=s_real)
    o_t = pl.pallas_call(
        kern, grid=(b, t_pad // tq),
        in_specs=[pl.BlockSpec((1, tq, MLA_HEADS * LANE), lambda bi, qi: (bi, qi, 0)),
                  pl.BlockSpec((1, s_pad, MLA_HEADS * LANE), lambda bi, qi: (bi, 0, 0), pipeline_mode=pl.Buffered(1)),
                  pl.BlockSpec((1, s_pad // tk, MLA_WIDTH, tk), lambda bi, qi: (bi, 0, 0, 0),
                               pipeline_mode=pl.Buffered(1))],
        out_specs=pl.BlockSpec((1, MLA_WIDTH, tq), lambda bi, qi: (bi, 0, qi)),
        out_shape=jax.ShapeDtypeStruct((b, MLA_WIDTH, t_pad), BF16),
        compiler_params=pltpu.CompilerParams(dimension_semantics=("parallel", "arbitrary"),
                                             vmem_limit_bytes=VMEM_LIMIT),
        name="mla_attn")(qa, ka, _key_blocks_t(va, tk))
    return jnp.swapaxes(o_t, 1, 2)[:, :t]


FALSE_POSITION_PROBES = 24
MAX_PROBES = 64


def _sortable(bits):
    return bits ^ ((bits >> 31) & INT_MAX)


def _dsa_kernel(qb_ref, qi_ref, wi_ref, kb_ref, vt_ref, ik2_ref, tri_ref, o_ref,
                key_ref, qm_ref, qim_ref, *, tq, tk, q_off, s_real, topk):
    q0 = q_off + pl.program_id(1) * tq
    n_full, n_vis = _visible_blocks(q0, tq, tk, s_real)
    lane = _lane_iota((tq, LANE))
    lo = lane < LANE // 2

    for h in range(DSA_HEADS):
        pair = qb_ref[0, :, (h // 2) * LANE:(h // 2 + 1) * LANE]
        qm_ref[h] = jnp.where(lo if h % 2 == 0 else ~lo, pair, jnp.zeros_like(pair))
    for h in range(IDX_HEADS):
        pair = qi_ref[0, :, (h // 2) * LANE:(h // 2 + 1) * LANE]
        qim_ref[h] = jnp.where(lo if h % 2 == 0 else ~lo, pair, jnp.zeros_like(pair))
    w_rows = [wi_ref[0, h:h + 1, :] for h in range(IDX_HEADS)]

    def score_block(kb, carry, masked):
        q_max, q_min = carry
        k0 = pl.multiple_of(kb * tk, tk)
        ik = ik2_ref[0, pl.ds(k0, tk), :]
        score = jnp.zeros((tk, tq), F32)
        for h in range(IDX_HEADS):
            score = score + jnp.maximum(_nt_dot(ik, qim_ref[h]), 0.0) * w_rows[h]
        score = jnp.where(score == 0.0, 0.0, score)
        below = above = score
        if masked:
            vis = _visible_t(q0, k0, tk, tq, s_real)
            below, above = jnp.where(vis, score, -jnp.inf), jnp.where(vis, score, jnp.inf)
        key_ref[kb] = _sortable(pltpu.bitcast(below, I32))
        return (jnp.maximum(q_max, _fold_rows(below, jnp.max)), jnp.minimum(q_min, _fold_rows(above, jnp.min)))

    carry = (jnp.full((SUBLANES, tq), -jnp.inf, F32), jnp.full((SUBLANES, tq), jnp.inf, F32))
    carry = lax.fori_loop(0, n_full, functools.partial(score_block, masked=False), carry)
    q_max, q_min = lax.fori_loop(n_full, n_vis, functools.partial(score_block, masked=True), carry)
    row_max = jnp.max(q_max, axis=0, keepdims=True)
    row_min = jnp.min(q_min, axis=0, keepdims=True)

    def count_ge(t):
        def body(kb, acc):
            return acc + _fold_rows(jnp.where(key_ref[kb] >= t, 1.0, 0.0), jnp.sum)
        return jnp.sum(lax.fori_loop(0, n_vis, body, jnp.zeros((SUBLANES, tq), F32)), axis=0, keepdims=True)

    kf = float(topk)
    q_pos = q0 + lax.broadcasted_iota(I32, (1, tq), 1)
    n_row = jnp.minimum(((q_pos >> CHUNK_SHIFT) + 1) << CHUNK_SHIFT, s_real).astype(F32)
    few = n_row < kf
    lo0 = _sortable(lax.bitcast_convert_type(row_min, I32))
    hi0 = _sortable(lax.bitcast_convert_type(row_max, I32)) + 1

    def finished(lo_k, hi_k, c_lo):
        return few | (c_lo == kf) | (hi_k == lo_k + 1)

    def probe_step(carry):
        it, _, lo_k, hi_k, c_lo, c_hi, g_lo, g_hi, last = carry
        v_lo = lax.bitcast_convert_type(_sortable(lo_k), F32)
        v_hi = lax.bitcast_convert_type(_sortable(hi_k), F32)
        a = jnp.log(c_lo * (1.0 / kf)) * g_lo
        b = jnp.log(kf / jnp.maximum(c_hi, 0.5)) * g_hi
        p = _sortable(lax.bitcast_convert_type(v_lo + (v_hi - v_lo) * (a / (a + b)), I32))
        p = jnp.where(it >= FALSE_POSITION_PROBES, (lo_k >> 1) + (hi_k >> 1) + (lo_k & hi_k & 1), p)
        p = jnp.where((it == 0) & (lo_k < 0) & (hi_k > 0), 0, p)
        p = jnp.where(lo_k == 0, 1, p)
        p = jnp.minimum(jnp.maximum(p, lo_k + 1), hi_k - 1)
        c = count_ge(p)
        open_ = ~finished(lo_k, hi_k, c_lo)
        up = open_ & (c >= kf)
        down = open_ & (c < kf)
        lo_k, c_lo = jnp.where(up, p, lo_k), jnp.where(up, c, c_lo)
        hi_k, c_hi = jnp.where(down, p, hi_k), jnp.where(down, c, c_hi)
        g_lo = jnp.where(down, jnp.where(last < 0.0, 0.5 * g_lo, 1.0), jnp.where(up, 1.0, g_lo))
        g_hi = jnp.where(up, jnp.where(last > 0.0, 0.5 * g_hi, 1.0), jnp.where(down, 1.0, g_hi))
        last = jnp.where(up, 1.0, jnp.where(down, -1.0, last))
        n_open = jnp.max(jnp.where(finished(lo_k, hi_k, c_lo), 0, 1))
        return it + 1, n_open, lo_k, hi_k, c_lo, c_hi, g_lo, g_hi, last

    ones = jnp.ones((1, tq), F32)
    init = (jnp.int32(0), jnp.max(jnp.where(finished(lo0, hi0, n_row), 0, 1)), lo0, hi0, n_row, 0.0 * ones,
            ones, ones, 0.0 * ones)
    final = lax.while_loop(lambda c: (c[1] > 0) & (c[0] < MAX_PROBES), probe_step, init)
    t, c_lo, c_hi = final[2], final[4], final[5]

    t = jnp.where(few, KEY_NEG_INF, t)
    need = jnp.where(few, 0.0, jnp.where(c_lo == kf, kf, kf - c_hi))

    heads = range(DSA_HEADS)

    def attend_block(kb, carry):
        tied_before, ms, accs = carry
        k0 = pl.multiple_of(kb * tk, tk)
        blk = key_ref[kb]
        tied = jnp.where(blk == t, 1.0, 0.0)
        tied_rank = (tied_before + _dot(tri_ref[...], tied.astype(BF16))) * tied
        cap = jnp.where((blk >= t) & (tied_rank <= need), F32_MAX, NEG)
        k_pairs = [kb_ref[0, pl.ds(k0, tk), p * LANE:(p + 1) * LANE] for p in range(DSA_HEADS // 2)]
        v_pairs = [_with_ones(vt_ref[0, kb, p * LANE:(p + 1) * LANE, :]) for p in range(DSA_HEADS // 2)]
        scores = [_nt_dot(k_pairs[h // 2], qm_ref[h]) for h in heads]
        new = [_softmax_update_t(jnp.minimum(scores[h], cap), v_pairs[h // 2], ms[h], accs[h]) for h in heads]
        return (tied_before + jnp.sum(_fold_rows(tied, jnp.sum), axis=0, keepdims=True),
                tuple(n[0] for n in new), tuple(n[1] for n in new))

    _, _, accs = lax.fori_loop(0, n_vis, attend_block, (jnp.zeros((1, tq), F32),) + _softmax_init(len(heads), tq))
    _write_pairs_t(o_ref, accs)


def _dsa_attn(qb, qi, wi, kb, vb, ik2, *, q_off, s_real, tq, tk):
    b, t, _ = qb.shape
    s_pad = kb.shape[1]
    n_kb = s_pad // tk
    topk = min(TOPK_MAX, s_real // 4)
    qb, qi, wi = _pad_queries([qb, qi, wi], tq)
    t_pad = qb.shape[1]
    wi_t = jnp.swapaxes(wi[:, :, :SUBLANES], 1, 2)
    v_t = _key_blocks_t(vb, tk)
    tri = (lax.broadcasted_iota(I32, (tk, tk), 0) >= lax.broadcasted_iota(I32, (tk, tk), 1)).astype(BF16)
    qspec = lambda w: pl.BlockSpec((1, tq, w), lambda bi, i: (bi, i, 0))
    kspec = lambda w: pl.BlockSpec((1, s_pad, w), lambda bi, i: (bi, 0, 0), pipeline_mode=pl.Buffered(1))
    kern = functools.partial(_dsa_kernel, tq=tq, tk=tk, q_off=q_off, s_real=s_real, topk=topk)
    o_t = pl.pallas_call(
        kern, grid=(b, t_pad // tq),
        in_specs=[qspec(DSA_WIDTH), qspec(IDX_HEADS * IDX_HD),
                  pl.BlockSpec((1, SUBLANES, tq), lambda bi, i: (bi, 0, i)),
                  kspec(DSA_WIDTH),
                  pl.BlockSpec((1, n_kb, DSA_WIDTH, tk), lambda bi, i: (bi, 0, 0, 0), pipeline_mode=pl.Buffered(1)),
                  kspec(LANE), _resident((tk, tk))],
        out_specs=pl.BlockSpec((1, DSA_WIDTH, tq), lambda bi, i: (bi, 0, i)),
        out_shape=jax.ShapeDtypeStruct((b, DSA_WIDTH, t_pad), BF16),
        scratch_shapes=[pltpu.VMEM((n_kb, tk, tq), I32),
                        pltpu.VMEM((DSA_HEADS, tq, LANE), BF16), pltpu.VMEM((IDX_HEADS, tq, LANE), BF16)],
        compiler_params=pltpu.CompilerParams(dimension_semantics=("parallel", "arbitrary"),
                                             vmem_limit_bytes=VMEM_LIMIT),
        name="dsa_attn")(qb, qi, wi_t, kb, v_t, ik2, tri)
    return jnp.swapaxes(o_t, 1, 2)[:, :t]


def _pad_cols(w, width):
    return jnp.pad(w, ((0, 0), (0, width - w.shape[1])))


def _layer_weights(p, l):
    w_in = p["w_in"][l]
    off, pieces = 0, []
    for n in (MLA_Q_LORA, MLA_KV_LORA, MLA_ROPE, DSA_WIDTH, DSA_WIDTH, DSA_WIDTH, IDX_HEADS * IDX_HD, IDX_HD, IDX_HEADS):
        pieces.append(w_in[:, off:off + n])
        off += n
    c_q, c_kv, k_r, q_b, k_b, v_b, q_i, k_i, w_i = pieces
    k_r = jnp.pad(k_r, ((0, 0), (MLA_NOPE, LANE - MLA_QK)))
    w_in_p = jnp.concatenate([c_q, c_kv, k_r, q_b, k_b, v_b, q_i, k_i, k_i, _pad_cols(w_i, LANE)], axis=1)
    assert w_in_p.shape[1] == C_END

    d_lora = p["mla_w_uq"].shape[1]
    w_uq = p["mla_w_uq"][l].reshape(d_lora, MLA_HEADS, MLA_QK)
    w_uq = jnp.pad(w_uq, ((0, 0), (0, 0), (0, LANE - MLA_QK))).reshape(d_lora, MLA_HEADS * LANE)
    w_ukv = p["mla_w_ukv"][l].reshape(MLA_KV_LORA, MLA_HEADS, MLA_NOPE + MLA_V)
    w_nope = jnp.pad(w_ukv[:, :, :MLA_NOPE], ((0, 0), (0, 0), (0, LANE - MLA_NOPE))).reshape(MLA_KV_LORA, MLA_HEADS * LANE)
    w_v = w_ukv[:, :, MLA_NOPE:].reshape(MLA_KV_LORA, MLA_WIDTH)
    w_out = p["w_out"][l]

    row = lambda g: g[l][None, :].astype(F32)
    pad96 = lambda g: jnp.pad(g[l].astype(F32), (0, LANE - MLA_QK))[None, :]
    twice = lambda g: jnp.tile(g[l].astype(F32), 2)[None, :]
    lw = {
        "w_in": w_in_p.astype(BF16), "w_uq": w_uq.astype(BF16), "w_ukv_nope": w_nope.astype(BF16),
        "w_ukv_v": w_v.astype(BF16), "w_out_a": w_out[:MLA_WIDTH].astype(BF16), "w_out_b": w_out[MLA_WIDTH:].astype(BF16),
        "mix_norm": row(p["mix_norm"]), "mla_q_norm": row(p["mla_q_norm"]), "mla_kv_norm": row(p["mla_kv_norm"]),
        "mla_q_gain": pad96(p["mla_q_gain"]), "mla_k_gain": pad96(p["mla_k_gain"]),
        "dsa_q_gain": twice(p["dsa_q_gain"]), "dsa_k_gain": twice(p["dsa_k_gain"]),
    }
    for f in ("ffn1", "ffn2"):
        lw[f + "_norm"] = row(p[f + "_norm"])
        for w in ("w_gate", "w_up", "w_down"):
            lw[f + "_" + w] = p[f + "_" + w][l].astype(BF16)
    return lw


def _rope_tables(pos, rows):
    def cs(rot):
        inv = 1.0 / (ROPE_THETA ** (jnp.arange(0, rot, 2, dtype=F32) / rot))
        ang = pos.astype(F32)[:, None] * inv[None, :]
        return jnp.cos(ang), jnp.sin(ang)

    t = pos.shape[0]
    cos_a, sin_a = cs(MLA_ROPE)
    ones = lambda w: jnp.ones((t, w), F32)
    zeros = lambda w: jnp.zeros((t, w), F32)
    ca = jnp.concatenate([ones(MLA_NOPE), cos_a, cos_a, ones(LANE - MLA_QK)], axis=1)
    sa = jnp.concatenate([zeros(MLA_NOPE), -sin_a, sin_a, zeros(LANE - MLA_QK)], axis=1)
    cos_b, sin_b = cs(DSA_ROT)
    cb = jnp.tile(jnp.concatenate([cos_b, cos_b, ones(DSA_HD - DSA_ROT)], axis=1), (1, 2))
    sb = jnp.tile(jnp.concatenate([-sin_b, sin_b, zeros(DSA_HD - DSA_ROT)], axis=1), (1, 2))
    reps = max(1, rows // t)
    return tuple(jnp.tile(x, (reps, 1)) for x in (ca, sa, cb, sb))


def _pad_keys(x, s_pad):
    return jnp.pad(x, ((0, 0), (0, s_pad - x.shape[1]), (0, 0)))


def _trunk_layer(x, lw, tables, past, *, b, t, q_off, tq_mla, tk_mla, tq_dsa, tk_dsa):
    h = _ffn(x, lw["ffn1_norm"], lw["ffn1_w_gate"], lw["ffn1_w_up"], lw["ffn1_w_down"])
    (ckv, krope, kslab, kb, vb, ki, qa, qb, kb16, vb16, qi, ik2, wi) = _proj(h, lw, tables, t)
    per_batch = lambda a: a.reshape(b, t, a.shape[-1])
    if past is None:
        ckv_all, kslab_all = ckv, kslab
        kb_all, vb_all, ik2_all = per_batch(kb16), per_batch(vb16), per_batch(ik2)
        s_real = t
    else:
        p_ckv, p_krope, p_kb, p_vb, p_ki = past
        s_real = p_ckv.shape[1] + t
        cat = lambda old, new: jnp.concatenate([old, per_batch(new)], axis=1)
        ckv_all = cat(p_ckv, ckv).reshape(b * s_real, MLA_KV_LORA)
        p_kslab = jnp.pad(p_krope, ((0, 0), (0, 0), (MLA_NOPE, LANE - MLA_QK)))
        kslab_all = cat(p_kslab, kslab).reshape(b * s_real, LANE)
        kb_all = cat(p_kb.reshape(b, -1, DSA_WIDTH).astype(BF16), kb16)
        vb_all = cat(p_vb.reshape(b, -1, DSA_WIDTH).astype(BF16), vb16)
        ik2_all = cat(jnp.tile(p_ki, (1, 1, 2)).astype(BF16), ik2)
    ka, va = _mla_kv(ckv_all, kslab_all, lw)
    s_pad_a = pl.cdiv(s_real, tk_mla) * tk_mla
    ka = _pad_keys(ka.reshape(b, s_real, -1), s_pad_a)
    va = _pad_keys(va.reshape(b, s_real, -1), s_pad_a)
    oa = _mla_attn(per_batch(qa), ka, va, q_off=q_off, s_real=s_real, tq=tq_mla, tk=tk_mla)
    s_pad_b = pl.cdiv(s_real, tk_dsa) * tk_dsa
    ob = _dsa_attn(per_batch(qb), per_batch(qi), per_batch(wi), _pad_keys(kb_all, s_pad_b), _pad_keys(vb_all, s_pad_b),
                   _pad_keys(ik2_all, s_pad_b), q_off=q_off, s_real=s_real, tq=tq_dsa, tk=tk_dsa)
    y = _ffn(h, lw["ffn2_norm"], lw["ffn2_w_gate"], lw["ffn2_w_up"], lw["ffn2_w_down"],
             attn=(oa.reshape(b * t, MLA_WIDTH), ob.reshape(b * t, DSA_WIDTH), lw["w_out_a"], lw["w_out_b"]))
    rows = (per_batch(ckv), per_batch(krope), kb.reshape(b, t, DSA_HEADS, DSA_HD),
            vb.reshape(b, t, DSA_HEADS, DSA_HD), per_batch(ki))
    return y, rows


def kernel(x_prompt, x_sample, cache_mla_ckv, cache_mla_krope, cache_dsa_k, cache_dsa_v, cache_idx_k,
           ffn1_norm, ffn1_w_gate, ffn1_w_up, ffn1_w_down, mix_norm, w_in,
           mla_q_norm, mla_w_uq, mla_kv_norm, mla_w_ukv, mla_q_gain, mla_k_gain,
           dsa_q_gain, dsa_k_gain, w_out, ffn2_norm, ffn2_w_gate, ffn2_w_up, ffn2_w_down):
    params = dict(ffn1_norm=ffn1_norm, ffn1_w_gate=ffn1_w_gate, ffn1_w_up=ffn1_w_up, ffn1_w_down=ffn1_w_down,
                  mix_norm=mix_norm, w_in=w_in, mla_q_norm=mla_q_norm, mla_w_uq=mla_w_uq, mla_kv_norm=mla_kv_norm,
                  mla_w_ukv=mla_w_ukv, mla_q_gain=mla_q_gain, mla_k_gain=mla_k_gain, dsa_q_gain=dsa_q_gain,
                  dsa_k_gain=dsa_k_gain, w_out=w_out, ffn2_norm=ffn2_norm, ffn2_w_gate=ffn2_w_gate,
                  ffn2_w_up=ffn2_w_up, ffn2_w_down=ffn2_w_down)
    depth = w_in.shape[0]
    d_model = x_prompt.shape[-1]
    weights = [_layer_weights(params, l) for l in range(depth)]

    b_p, t_p = x_prompt.shape[:2]
    n_p = b_p * t_p
    tabs_p = _rope_tables(jnp.arange(t_p, dtype=I32), _row_tile(n_p, 512))
    tile_p = dict(tq_mla=min(t_p, 512), tk_mla=min(t_p, 512), tq_dsa=min(t_p, 512), tk_dsa=min(t_p, 512))
    h_p = x_prompt.reshape(n_p, d_model)
    p_rows = []
    for l in range(depth):
        h_p, rows = _trunk_layer(h_p, weights[l], tabs_p, None, b=b_p, t=t_p, q_off=0, **tile_p)
        p_rows.append(rows)

    b_s, t_s = x_sample.shape[:2]
    n_s = b_s * t_s
    past_len = cache_mla_ckv.shape[2]
    tabs_s = _rope_tables(past_len + jnp.arange(t_s, dtype=I32), _row_tile(n_s, 512))
    tile_s = dict(tq_mla=LANE, tk_mla=3 * LANE, tq_dsa=LANE, tk_dsa=3 * LANE)
    h_s = x_sample.reshape(n_s, d_model)
    s_rows = []
    for l in range(depth):
        past = (cache_mla_ckv[l], cache_mla_krope[l], cache_dsa_k[l], cache_dsa_v[l], cache_idx_k[l])
        h_s, rows = _trunk_layer(h_s, weights[l], tabs_s, past, b=b_s, t=t_s, q_off=past_len, **tile_s)
        s_rows.append(rows)

    stack = lambda rows_by_layer, i: jnp.stack([r[i] for r in rows_by_layer])
    return (h_p.reshape(b_p, t_p, d_model), h_s.reshape(b_s, t_s, d_model),
            *[stack(p_rows, i) for i in range(5)], *[stack(s_rows, i) for i in range(5)])
```

```python
import functools

import jax
import jax.numpy as jnp
from jax import lax
from jax.experimental import pallas as pl
from jax.experimental.pallas import tpu as pltpu

F32 = jnp.float32
BF16 = jnp.bfloat16
I32 = jnp.int32

CHUNK_SHIFT = 6
ROPE_THETA = 500000.0
EPS = 1e-6
MLA_HEADS = 8
MLA_NOPE = 64
MLA_ROPE = 32
MLA_QK = MLA_NOPE + MLA_ROPE
MLA_V = 64
MLA_Q_LORA = 256
MLA_KV_LORA = 128
DSA_HEADS = 8
DSA_HD = 64
DSA_ROT = 16
IDX_HEADS = 4
IDX_HD = 64
IDX_W_SCALE = (IDX_HD * IDX_HEADS) ** -0.5
TOPK_MAX = 256
DSA_WIDTH = DSA_HEADS * DSA_HD
MLA_WIDTH = MLA_HEADS * MLA_V

LANE = 128
VMEM_LIMIT = 56 * 1024 * 1024

NEG = -1e30
F32_MAX = 3.4028234e38
LOG2E = 1.4426950408889634
INT_MIN = -(2 ** 31)
INT_MAX = 2 ** 31 - 1
KEY_NEG_INF = INT_MIN + 0x7FFFFF


def _nt_dot(a, b):
    return lax.dot_general(a, b, (((1,), (1,)), ((), ())), preferred_element_type=F32)


def _dot(a, b):
    return jnp.dot(a, b, preferred_element_type=F32)


def _rms(x, g):
    return x * lax.rsqrt(jnp.mean(x * x, axis=-1, keepdims=True) + EPS) * g


def _lane_iota(shape):
    return lax.broadcasted_iota(I32, shape, len(shape) - 1)


FFN_CHUNK = 256


def _ffn_body(x, g_ref, wg_ref, wu_ref, wd_ref, o_ref):
    xb = _rms(x, g_ref[...]).astype(BF16)
    d_ff = wg_ref.shape[1]
    acc = jnp.zeros(x.shape, F32)
    for c in range(d_ff // FFN_CHUNK):
        sl = slice(c * FFN_CHUNK, (c + 1) * FFN_CHUNK)
        gate = _dot(xb, wg_ref[:, sl])
        up = _dot(xb, wu_ref[:, sl])
        act = (gate * jax.nn.sigmoid(gate) * up).astype(BF16)
        acc = acc + _dot(act, wd_ref[sl, :])
    o_ref[...] = x + 0.5 * acc


def _ffn_kernel(x_ref, g_ref, wg_ref, wu_ref, wd_ref, o_ref):
    _ffn_body(x_ref[...], g_ref, wg_ref, wu_ref, wd_ref, o_ref)


def _out_ffn_kernel(h_ref, oa_ref, ob_ref, woa_ref, wob_ref, g_ref, wg_ref, wu_ref, wd_ref, o_ref):
    x = h_ref[...] + _dot(oa_ref[...], woa_ref[...]) + _dot(ob_ref[...], wob_ref[...])
    _ffn_body(x, g_ref, wg_ref, wu_ref, wd_ref, o_ref)


def _resident(shape):
    nd = len(shape)
    return pl.BlockSpec(shape, lambda *_: (0,) * nd, pipeline_mode=pl.Buffered(1))


def _row_tile(n, pref):
    for t in range(min(n, pref), 0, -16):
        if n % t == 0:
            return t
    raise ValueError(f"no row tile for {n} rows")


def _ffn(x, g, wg, wu, wd, attn=None):
    n, d = x.shape
    tm = _row_tile(n, 512)
    row = lambda w: pl.BlockSpec((tm, w), lambda i: (i, 0))
    w_specs = [_resident(g.shape), _resident(wg.shape), _resident(wu.shape), _resident(wd.shape)]
    params = pltpu.CompilerParams(dimension_semantics=("parallel",), vmem_limit_bytes=VMEM_LIMIT)
    out_shape = jax.ShapeDtypeStruct((n, d), F32)
    if attn is None:
        return pl.pallas_call(_ffn_kernel, grid=(n // tm,), in_specs=[row(d)] + w_specs, out_specs=row(d),
                              out_shape=out_shape, compiler_params=params, name="ffn")(x, g, wg, wu, wd)
    oa, ob, woa, wob = attn
    return pl.pallas_call(
        _out_ffn_kernel, grid=(n // tm,),
        in_specs=[row(d), row(oa.shape[1]), row(ob.shape[1]), _resident(woa.shape), _resident(wob.shape)] + w_specs,
        out_specs=row(d), out_shape=out_shape, compiler_params=params, name="out_ffn")(x, oa, ob, woa, wob, g, wg, wu, wd)


C_CQ = 0
C_CKV = C_CQ + MLA_Q_LORA
C_KR = C_CKV + MLA_KV_LORA
C_QB = C_KR + LANE
C_KB = C_QB + DSA_WIDTH
C_VB = C_KB + DSA_WIDTH
C_QI = C_VB + DSA_WIDTH
C_KI = C_QI + IDX_HEADS * IDX_HD
C_WI = C_KI + LANE
C_END = C_WI + LANE


def _rope_a(x, c, s):
    lane = _lane_iota(x.shape)
    partner = jnp.where(lane < MLA_NOPE + MLA_ROPE // 2, pltpu.roll(x, LANE - MLA_ROPE // 2, 1),
                        pltpu.roll(x, MLA_ROPE // 2, 1))
    return x * c + partner * s


def _rope_b(x, c, s):
    lane = _lane_iota(x.shape)
    half = DSA_ROT // 2
    partner = jnp.where((lane & (DSA_HD - 1)) < half, pltpu.roll(x, LANE - half, 1), pltpu.roll(x, half, 1))
    return x * c + partner * s


def _head96_norm(x, g):
    ms = jnp.sum(x * x, axis=-1, keepdims=True) * (1.0 / MLA_QK)
    return x * lax.rsqrt(ms + EPS) * g


def _head64_norm(x, g2):
    lane = _lane_iota(x.shape)
    lo = lane < DSA_HD
    sq = x * x
    s_lo = jnp.sum(jnp.where(lo, sq, 0.0), axis=-1, keepdims=True)
    s_hi = jnp.sum(jnp.where(lo, 0.0, sq), axis=-1, keepdims=True)
    ms = jnp.where(lo, s_lo, s_hi) * (1.0 / DSA_HD)
    return x * lax.rsqrt(ms + EPS) * g2


def _proj_kernel(h_ref, gmix_ref, win_ref, gq_ref, wuq_ref, gkv_ref, gqa_ref, gqb_ref, gkb_ref,
                 ca_ref, sa_ref, cb_ref, sb_ref,
                 ckv_ref, krope_ref, kslab_ref, kb_ref, vb_ref, ki_ref,
                 qa_ref, qb_ref, kb16_ref, vb16_ref, qi_ref, ik2_ref, wi_ref):
    u = _rms(h_ref[...], gmix_ref[...]).astype(BF16)
    ca, sa, cb, sb = ca_ref[...], sa_ref[...], cb_ref[...], sb_ref[...]

    def cols(start, width):
        return _dot(u, win_ref[:, start:start + width])

    cq = _rms(cols(C_CQ, MLA_Q_LORA), gq_ref[...]).astype(BF16)
    qa = _dot(cq, wuq_ref[...])
    qa_scale = MLA_QK ** -0.5 * LOG2E
    for h in range(MLA_HEADS):
        sl = slice(h * LANE, (h + 1) * LANE)
        qa_ref[:, sl] = (_head96_norm(_rope_a(qa[:, sl], ca, sa), gqa_ref[...]) * qa_scale).astype(BF16)

    ckv_ref[...] = _rms(cols(C_CKV, MLA_KV_LORA), gkv_ref[...])
    kslab = _rope_a(cols(C_KR, LANE), ca, sa)
    kslab_ref[...] = kslab
    krope_ref[...] = kslab[:, MLA_NOPE:MLA_NOPE + MLA_ROPE]

    qb = cols(C_QB, DSA_WIDTH)
    kb = cols(C_KB, DSA_WIDTH)
    qb_scale = DSA_HD ** -0.5 * LOG2E
    for p in range(DSA_WIDTH // LANE):
        sl = slice(p * LANE, (p + 1) * LANE)
        qb_ref[:, sl] = (_rope_b(_head64_norm(qb[:, sl], gqb_ref[...]), cb, sb) * qb_scale).astype(BF16)
        kp = _rope_b(_head64_norm(kb[:, sl], gkb_ref[...]), cb, sb)
        kb_ref[:, 2 * p:2 * p + 2, :] = kp.reshape(kp.shape[0], 2, DSA_HD)
        kb16_ref[:, sl] = kp.astype(BF16)
    vb = cols(C_VB, DSA_WIDTH)
    vb_ref[...] = vb.reshape(vb.shape[0], DSA_HEADS, DSA_HD)
    vb16_ref[...] = vb.astype(BF16)

    qi = cols(C_QI, IDX_HEADS * IDX_HD)
    for p in range(IDX_HEADS * IDX_HD // LANE):
        sl = slice(p * LANE, (p + 1) * LANE)
        qi_ref[:, sl] = _rope_b(qi[:, sl], cb, sb).astype(BF16)
    ik2 = _rope_b(cols(C_KI, LANE), cb, sb)
    ki_ref[...] = ik2[:, :IDX_HD]
    ik2_ref[...] = ik2.astype(BF16)
    wi_ref[...] = cols(C_WI, LANE) * IDX_W_SCALE


def _proj(h, lw, tables, t_seq):
    n, d = h.shape
    tm = _row_tile(n, 512)
    ca, sa, cb, sb = tables
    n_tab = ca.shape[0] // tm
    row = lambda w: pl.BlockSpec((tm, w), lambda i: (i, 0))
    tab = pl.BlockSpec((tm, LANE), lambda i: (i % n_tab, 0))
    consts = [lw["mix_norm"], lw["w_in"], lw["mla_q_norm"], lw["w_uq"], lw["mla_kv_norm"],
              lw["mla_q_gain"], lw["dsa_q_gain"], lw["dsa_k_gain"]]
    out_widths = [(MLA_KV_LORA, F32), (MLA_ROPE, F32), (LANE, F32), (DSA_WIDTH, F32), (DSA_WIDTH, F32), (IDX_HD, F32),
                  (MLA_HEADS * LANE, BF16), (DSA_WIDTH, BF16), (DSA_WIDTH, BF16), (DSA_WIDTH, BF16),
                  (IDX_HEADS * IDX_HD, BF16), (LANE, BF16), (LANE, F32)]
    per_head = (3, 4)
    heads_spec = pl.BlockSpec((tm, DSA_HEADS, DSA_HD), lambda i: (i, 0, 0))
    return pl.pallas_call(
        _proj_kernel, grid=(n // tm,),
        in_specs=[row(d)] + [_resident(c.shape) for c in consts] + [tab] * 4,
        out_specs=[heads_spec if i in per_head else row(w) for i, (w, _) in enumerate(out_widths)],
        out_shape=[jax.ShapeDtypeStruct((n, DSA_HEADS, DSA_HD) if i in per_head else (n, w), dt)
                   for i, (w, dt) in enumerate(out_widths)],
        compiler_params=pltpu.CompilerParams(dimension_semantics=("parallel",), vmem_limit_bytes=VMEM_LIMIT),
        name="proj")(h, *consts, ca, sa, cb, sb)


def _mla_kv_kernel(ckv_ref, kslab_ref, wn_ref, wv_ref, gk_ref, ka_ref, va_ref):
    c = ckv_ref[...].astype(BF16)
    kn = _dot(c, wn_ref[...])
    kslab = kslab_ref[...]
    for h in range(MLA_HEADS):
        sl = slice(h * LANE, (h + 1) * LANE)
        ka_ref[:, sl] = _head96_norm(kn[:, sl] + kslab, gk_ref[...]).astype(BF16)
    va_ref[...] = _dot(c, wv_ref[...]).astype(BF16)


def _mla_kv(ckv, kslab, lw):
    m = ckv.shape[0]
    tm = _row_tile(m, 512)
    row = lambda w: pl.BlockSpec((tm, w), lambda i: (i, 0))
    consts = [lw["w_ukv_nope"], lw["w_ukv_v"], lw["mla_k_gain"]]
    return pl.pallas_call(
        _mla_kv_kernel, grid=(m // tm,),
        in_specs=[row(MLA_KV_LORA), row(LANE)] + [_resident(c.shape) for c in consts],
        out_specs=[row(MLA_HEADS * LANE), row(MLA_WIDTH)],
        out_shape=[jax.ShapeDtypeStruct((m, MLA_HEADS * LANE), BF16), jax.ShapeDtypeStruct((m, MLA_WIDTH), BF16)],
        compiler_params=pltpu.CompilerParams(dimension_semantics=("parallel",), vmem_limit_bytes=VMEM_LIMIT),
        name="mla_kv")(ckv, kslab, *consts)


SUBLANES = 8
FOLD_CHAINS = 4


def _visible_t(q0, k0, tk, tq, s_real):
    q_chunk = (q0 + lax.broadcasted_iota(I32, (tk, tq), 1)) >> CHUNK_SHIFT
    k_idx = k0 + lax.broadcasted_iota(I32, (tk, tq), 0)
    return ((k_idx >> CHUNK_SHIFT) <= q_chunk) & (k_idx < s_real)


def _fold_rows(x, reduce):
    groups = x.shape[0] // SUBLANES
    if groups % FOLD_CHAINS == 0 and groups > FOLD_CHAINS:
        x = reduce(x.reshape(groups // FOLD_CHAINS, FOLD_CHAINS * SUBLANES, x.shape[1]), axis=0)
        groups = FOLD_CHAINS
    return reduce(x.reshape(groups, SUBLANES, x.shape[1]), axis=0)


HEAD_V = 64
ONES_ROWS = 16
HEAD_ROWS = HEAD_V + ONES_ROWS
HEAD_GROUPS = (range(0, 4), range(4, 8))


def _head_values(vt_ref, kb, h):
    v_t = vt_ref[0, kb, h * HEAD_V:(h + 1) * HEAD_V, :]
    return jnp.concatenate([v_t, jnp.ones((ONES_ROWS, v_t.shape[1]), v_t.dtype)], axis=0)


def _softmax_update_t(s, v_t, m, acc):
    m_new = jnp.maximum(m, jnp.max(_fold_rows(s, jnp.max), axis=0, keepdims=True))
    p = jnp.exp2(s - m_new)
    return m_new, jnp.exp2(m - m_new) * acc + _dot(v_t, p.astype(BF16))


def _softmax_init(n_heads, tq):
    return (tuple(jnp.full((1, tq), NEG, F32) for _ in range(n_heads)),
            tuple(jnp.zeros((HEAD_ROWS, tq), F32) for _ in range(n_heads)))


def _write_heads_t(o_ref, accs):
    for h, a in enumerate(accs):
        o_ref[0, h * HEAD_V:(h + 1) * HEAD_V, :] = (a[:HEAD_V] / a[HEAD_V:HEAD_V + 1]).astype(o_ref.dtype)


def _pad_queries(arrays, tq):
    t = arrays[0].shape[1]
    t_pad = pl.cdiv(t, tq) * tq
    return [jnp.pad(a, ((0, 0), (0, t_pad - t), (0, 0))) for a in arrays] if t_pad != t else list(arrays)


def _visible_blocks(q0, tq, tk, s_real):
    n_full = jnp.minimum(((q0 >> CHUNK_SHIFT) + 1) << CHUNK_SHIFT, s_real) // tk
    vis_end = jnp.minimum((((q0 + tq - 1) >> CHUNK_SHIFT) + 1) << CHUNK_SHIFT, s_real)
    return n_full, (vis_end + tk - 1) // tk


def _mla_attn_kernel(q_ref, k_ref, vt_ref, o_ref, *, tq, tk, q_off, s_real):
    q0 = q_off + pl.program_id(1) * tq
    n_full, n_vis = _visible_blocks(q0, tq, tk, s_real)

    def attend_block(kb, carry, masked):
        ms, accs = carry
        k0 = pl.multiple_of(kb * tk, tk)
        if masked:
            cap = jnp.where(_visible_t(q0, k0, tk, tq, s_real), F32_MAX, NEG)
        new = []
        for group in HEAD_GROUPS:
            scores = {h: _nt_dot(k_ref[0, pl.ds(k0, tk), h * LANE:(h + 1) * LANE], q_ref[0, :, h * LANE:(h + 1) * LANE])
                      for h in group}
            if masked:
                scores = {h: jnp.minimum(s, cap) for h, s in scores.items()}
            new += [_softmax_update_t(scores[h], _head_values(vt_ref, kb, h), ms[h], accs[h]) for h in group]
        return tuple(n[0] for n in new), tuple(n[1] for n in new)

    carry = _softmax_init(MLA_HEADS, tq)
    carry = lax.fori_loop(0, n_full, functools.partial(attend_block, masked=False), carry)
    _, accs = lax.fori_loop(n_full, n_vis, functools.partial(attend_block, masked=True), carry)
    _write_heads_t(o_ref, accs)


def _key_blocks_t(v, tk):
    b, s_pad, w = v.shape
    return jnp.swapaxes(v.reshape(b, s_pad // tk, tk, w), 2, 3)


def _mla_attn(qa, ka, va, *, q_off, s_real, tq, tk):
    b, t, _ = qa.shape
    s_pad = ka.shape[1]
    (qa,) = _pad_queries([qa], tq)
    t_pad = qa.shape[1]
    kern = functools.partial(_mla_attn_kernel, tq=tq, tk=tk, q_off=q_off, s_real=s_real)
    o_t = pl.pallas_call(
        kern, grid=(b, t_pad // tq),
        in_specs=[pl.BlockSpec((1, tq, MLA_HEADS * LANE), lambda bi, qi: (bi, qi, 0)),
                  pl.BlockSpec((1, s_pad, MLA_HEADS * LANE), lambda bi, qi: (bi, 0, 0), pipeline_mode=pl.Buffered(1)),
                  pl.BlockSpec((1, s_pad // tk, MLA_WIDTH, tk), lambda bi, qi: (bi, 0, 0, 0),
                               pipeline_mode=pl.Buffered(1))],
        out_specs=pl.BlockSpec((1, MLA_WIDTH, tq), lambda bi, qi: (bi, 0, qi)),
        out_shape=jax.ShapeDtypeStruct((b, MLA_WIDTH, t_pad), BF16),
        compiler_params=pltpu.CompilerParams(dimension_semantics=("parallel", "arbitrary"),
                                             vmem_limit_bytes=VMEM_LIMIT),
        name="mla_attn")(qa, ka, _key_blocks_t(va, tk))
    return jnp.swapaxes(o_t, 1, 2)[:, :t]


FALSE_POSITION_PROBES = 24
MAX_PROBES = 64


def _sortable(bits):
    return bits ^ ((bits >> 31) & INT_MAX)


def _dsa_kernel(qb_ref, qi_ref, wi_ref, kb_ref, vt_ref, ik2_ref, tri_ref, o_ref,
                key_ref, qm_ref, qim_ref, *, tq, tk, q_off, s_real, topk):
    q0 = q_off + pl.program_id(1) * tq
    n_full, n_vis = _visible_blocks(q0, tq, tk, s_real)
    lane = _lane_iota((tq, LANE))
    lo = lane < LANE // 2

    for h in range(DSA_HEADS):
        pair = qb_ref[0, :, (h // 2) * LANE:(h // 2 + 1) * LANE]
        qm_ref[h] = jnp.where(lo if h % 2 == 0 else ~lo, pair, jnp.zeros_like(pair))
    for h in range(IDX_HEADS):
        pair = qi_ref[0, :, (h // 2) * LANE:(h // 2 + 1) * LANE]
        qim_ref[h] = jnp.where(lo if h % 2 == 0 else ~lo, pair, jnp.zeros_like(pair))
    w_rows = [wi_ref[0, h:h + 1, :] for h in range(IDX_HEADS)]

    def score_block(kb, carry, masked):
        q_max, q_min = carry
        k0 = pl.multiple_of(kb * tk, tk)
        ik = ik2_ref[0, pl.ds(k0, tk), :]
        score = jnp.zeros((tk, tq), F32)
        for h in range(IDX_HEADS):
            score = score + jnp.maximum(_nt_dot(ik, qim_ref[h]), 0.0) * w_rows[h]
        score = jnp.where(score == 0.0, 0.0, score)
        below = above = score
        if masked:
            vis = _visible_t(q0, k0, tk, tq, s_real)
            below, above = jnp.where(vis, score, -jnp.inf), jnp.where(vis, score, jnp.inf)
        key_ref[kb] = _sortable(pltpu.bitcast(below, I32))
        return (jnp.maximum(q_max, _fold_rows(below, jnp.max)), jnp.minimum(q_min, _fold_rows(above, jnp.min)))

    carry = (jnp.full((SUBLANES, tq), -jnp.inf, F32), jnp.full((SUBLANES, tq), jnp.inf, F32))
    carry = lax.fori_loop(0, n_full, functools.partial(score_block, masked=False), carry)
    q_max, q_min = lax.fori_loop(n_full, n_vis, functools.partial(score_block, masked=True), carry)
    row_max = jnp.max(q_max, axis=0, keepdims=True)
    row_min = jnp.min(q_min, axis=0, keepdims=True)

    def count_ge(t):
        def body(kb, acc):
            return acc + _fold_rows(jnp.where(key_ref[kb] >= t, 1.0, 0.0), jnp.sum)
        return jnp.sum(lax.fori_loop(0, n_vis, body, jnp.zeros((SUBLANES, tq), F32)), axis=0, keepdims=True)

    kf = float(topk)
    q_pos = q0 + lax.broadcasted_iota(I32, (1, tq), 1)
    n_row = jnp.minimum(((q_pos >> CHUNK_SHIFT) + 1) << CHUNK_SHIFT, s_real).astype(F32)
    few = n_row < kf
    lo0 = _sortable(lax.bitcast_convert_type(row_min, I32))
    hi0 = _sortable(lax.bitcast_convert_type(row_max, I32)) + 1

    def finished(lo_k, hi_k, c_lo):
        return few | (c_lo == kf) | (hi_k == lo_k + 1)

    def probe_step(carry):
        it, _, lo_k, hi_k, c_lo, c_hi, g_lo, g_hi, last = carry
        v_lo = lax.bitcast_convert_type(_sortable(lo_k), F32)
        v_hi = lax.bitcast_convert_type(_sortable(hi_k), F32)
        a = jnp.log(c_lo * (1.0 / kf)) * g_lo
        b = jnp.log(kf / jnp.maximum(c_hi, 0.5)) * g_hi
        p = _sortable(lax.bitcast_convert_type(v_lo + (v_hi - v_lo) * (a / (a + b)), I32))
        p = jnp.where(it >= FALSE_POSITION_PROBES, (lo_k >> 1) + (hi_k >> 1) + (lo_k & hi_k & 1), p)
        p = jnp.where((it == 0) & (lo_k < 0) & (hi_k > 0), 0, p)
        p = jnp.where(lo_k == 0, 1, p)
        p = jnp.minimum(jnp.maximum(p, lo_k + 1), hi_k - 1)
        c = count_ge(p)
        open_ = ~finished(lo_k, hi_k, c_lo)
        up = open_ & (c >= kf)
        down = open_ & (c < kf)
        lo_k, c_lo = jnp.where(up, p, lo_k), jnp.where(up, c, c_lo)
        hi_k, c_hi = jnp.where(down, p, hi_k), jnp.where(down, c, c_hi)
        g_lo = jnp.where(down, jnp.where(last < 0.0, 0.5 * g_lo, 1.0), jnp.where(up, 1.0, g_lo))
        g_hi = jnp.where(up, jnp.where(last > 0.0, 0.5 * g_hi, 1.0), jnp.where(down, 1.0, g_hi))
        last = jnp.where(up, 1.0, jnp.where(down, -1.0, last))
        n_open = jnp.max(jnp.where(finished(lo_k, hi_k, c_lo), 0, 1))
        return it + 1, n_open, lo_k, hi_k, c_lo, c_hi, g_lo, g_hi, last

    ones = jnp.ones((1, tq), F32)
    init = (jnp.int32(0), jnp.max(jnp.where(finished(lo0, hi0, n_row), 0, 1)), lo0, hi0, n_row, 0.0 * ones,
            ones, ones, 0.0 * ones)
    final = lax.while_loop(lambda c: (c[1] > 0) & (c[0] < MAX_PROBES), probe_step, init)
    t, c_lo, c_hi = final[2], final[4], final[5]

    t = jnp.where(few, KEY_NEG_INF, t)
    need = jnp.where(few, 0.0, jnp.where(c_lo == kf, kf, kf - c_hi))

    def attend_block(kb, carry):
        tied_before, ms, accs = carry
        k0 = pl.multiple_of(kb * tk, tk)
        blk = key_ref[kb]
        tied = jnp.where(blk == t, 1.0, 0.0)
        tied_rank = (tied_before + _dot(tri_ref[...], tied.astype(BF16))) * tied
        cap = jnp.where((blk >= t) & (tied_rank <= need), F32_MAX, NEG)
        k_pairs = [kb_ref[0, pl.ds(k0, tk), p * LANE:(p + 1) * LANE] for p in range(DSA_HEADS // 2)]
        new = []
        for group in HEAD_GROUPS:
            scores = {h: _nt_dot(k_pairs[h // 2], qm_ref[h]) for h in group}
            new += [_softmax_update_t(jnp.minimum(scores[h], cap), _head_values(vt_ref, kb, h), ms[h], accs[h]) for h in group]
        return (tied_before + jnp.sum(_fold_rows(tied, jnp.sum), axis=0, keepdims=True),
                tuple(n[0] for n in new), tuple(n[1] for n in new))

    _, _, accs = lax.fori_loop(0, n_vis, attend_block, (jnp.zeros((1, tq), F32),) + _softmax_init(DSA_HEADS, tq))
    _write_heads_t(o_ref, accs)


def _dsa_attn(qb, qi, wi, kb, vb, ik2, *, q_off, s_real, tq, tk):
    b, t, _ = qb.shape
    s_pad = kb.shape[1]
    n_kb = s_pad // tk
    topk = min(TOPK_MAX, s_real // 4)
    qb, qi, wi = _pad_queries([qb, qi, wi], tq)
    t_pad = qb.shape[1]
    wi_t = jnp.swapaxes(wi[:, :, :SUBLANES], 1, 2)
    v_t = _key_blocks_t(vb, tk)
    tri = (lax.broadcasted_iota(I32, (tk, tk), 0) >= lax.broadcasted_iota(I32, (tk, tk), 1)).astype(BF16)
    qspec = lambda w: pl.BlockSpec((1, tq, w), lambda bi, i: (bi, i, 0))
    kspec = lambda w: pl.BlockSpec((1, s_pad, w), lambda bi, i: (bi, 0, 0), pipeline_mode=pl.Buffered(1))
    kern = functools.partial(_dsa_kernel, tq=tq, tk=tk, q_off=q_off, s_real=s_real, topk=topk)
    o_t = pl.pallas_call(
        kern, grid=(b, t_pad // tq),
        in_specs=[qspec(DSA_WIDTH), qspec(IDX_HEADS * IDX_HD),
                  pl.BlockSpec((1, SUBLANES, tq), lambda bi, i: (bi, 0, i)),
                  kspec(DSA_WIDTH),
                  pl.BlockSpec((1, n_kb, DSA_WIDTH, tk), lambda bi, i: (bi, 0, 0, 0), pipeline_mode=pl.Buffered(1)),
                  kspec(LANE), _resident((tk, tk))],
        out_specs=pl.BlockSpec((1, DSA_WIDTH, tq), lambda bi, i: (bi, 0, i)),
        out_shape=jax.ShapeDtypeStruct((b, DSA_WIDTH, t_pad), BF16),
        scratch_shapes=[pltpu.VMEM((n_kb, tk, tq), I32),
                        pltpu.VMEM((DSA_HEADS, tq, LANE), BF16), pltpu.VMEM((IDX_HEADS, tq, LANE), BF16)],
        compiler_params=pltpu.CompilerParams(dimension_semantics=("parallel", "arbitrary"),
                                             vmem_limit_bytes=VMEM_LIMIT),
        name="dsa_attn")(qb, qi, wi_t, kb, v_t, ik2, tri)
    return jnp.swapaxes(o_t, 1, 2)[:, :t]


def _pad_cols(w, width):
    return jnp.pad(w, ((0, 0), (0, width - w.shape[1])))


def _layer_weights(p, l):
    w_in = p["w_in"][l]
    off, pieces = 0, []
    for n in (MLA_Q_LORA, MLA_KV_LORA, MLA_ROPE, DSA_WIDTH, DSA_WIDTH, DSA_WIDTH, IDX_HEADS * IDX_HD, IDX_HD, IDX_HEADS):
        pieces.append(w_in[:, off:off + n])
        off += n
    c_q, c_kv, k_r, q_b, k_b, v_b, q_i, k_i, w_i = pieces
    k_r = jnp.pad(k_r, ((0, 0), (MLA_NOPE, LANE - MLA_QK)))
    w_in_p = jnp.concatenate([c_q, c_kv, k_r, q_b, k_b, v_b, q_i, k_i, k_i, _pad_cols(w_i, LANE)], axis=1)
    assert w_in_p.shape[1] == C_END

    d_lora = p["mla_w_uq"].shape[1]
    w_uq = p["mla_w_uq"][l].reshape(d_lora, MLA_HEADS, MLA_QK)
    w_uq = jnp.pad(w_uq, ((0, 0), (0, 0), (0, LANE - MLA_QK))).reshape(d_lora, MLA_HEADS * LANE)
    w_ukv = p["mla_w_ukv"][l].reshape(MLA_KV_LORA, MLA_HEADS, MLA_NOPE + MLA_V)
    w_nope = jnp.pad(w_ukv[:, :, :MLA_NOPE], ((0, 0), (0, 0), (0, LANE - MLA_NOPE))).reshape(MLA_KV_LORA, MLA_HEADS * LANE)
    w_v = w_ukv[:, :, MLA_NOPE:].reshape(MLA_KV_LORA, MLA_WIDTH)
    w_out = p["w_out"][l]

    row = lambda g: g[l][None, :].astype(F32)
    pad96 = lambda g: jnp.pad(g[l].astype(F32), (0, LANE - MLA_QK))[None, :]
    twice = lambda g: jnp.tile(g[l].astype(F32), 2)[None, :]
    lw = {
        "w_in": w_in_p.astype(BF16), "w_uq": w_uq.astype(BF16), "w_ukv_nope": w_nope.astype(BF16),
        "w_ukv_v": w_v.astype(BF16), "w_out_a": w_out[:MLA_WIDTH].astype(BF16), "w_out_b": w_out[MLA_WIDTH:].astype(BF16),
        "mix_norm": row(p["mix_norm"]), "mla_q_norm": row(p["mla_q_norm"]), "mla_kv_norm": row(p["mla_kv_norm"]),
        "mla_q_gain": pad96(p["mla_q_gain"]), "mla_k_gain": pad96(p["mla_k_gain"]),
        "dsa_q_gain": twice(p["dsa_q_gain"]), "dsa_k_gain": twice(p["dsa_k_gain"]),
    }
    for f in ("ffn1", "ffn2"):
        lw[f + "_norm"] = row(p[f + "_norm"])
        for w in ("w_gate", "w_up", "w_down"):
            lw[f + "_" + w] = p[f + "_" + w][l].astype(BF16)
    return lw


def _rope_tables(pos, rows):
    def cs(rot):
        inv = 1.0 / (ROPE_THETA ** (jnp.arange(0, rot, 2, dtype=F32) / rot))
        ang = pos.astype(F32)[:, None] * inv[None, :]
        return jnp.cos(ang), jnp.sin(ang)

    t = pos.shape[0]
    cos_a, sin_a = cs(MLA_ROPE)
    ones = lambda w: jnp.ones((t, w), F32)
    zeros = lambda w: jnp.zeros((t, w), F32)
    ca = jnp.concatenate([ones(MLA_NOPE), cos_a, cos_a, ones(LANE - MLA_QK)], axis=1)
    sa = jnp.concatenate([zeros(MLA_NOPE), -sin_a, sin_a, zeros(LANE - MLA_QK)], axis=1)
    cos_b, sin_b = cs(DSA_ROT)
    cb = jnp.tile(jnp.concatenate([cos_b, cos_b, ones(DSA_HD - DSA_ROT)], axis=1), (1, 2))
    sb = jnp.tile(jnp.concatenate([-sin_b, sin_b, zeros(DSA_HD - DSA_ROT)], axis=1), (1, 2))
    reps = max(1, rows // t)
    return tuple(jnp.tile(x, (reps, 1)) for x in (ca, sa, cb, sb))


def _pad_keys(x, s_pad):
    return jnp.pad(x, ((0, 0), (0, s_pad - x.shape[1]), (0, 0)))


def _trunk_layer(x, lw, tables, past, *, b, t, q_off, tq_mla, tk_mla, tq_dsa, tk_dsa):
    h = _ffn(x, lw["ffn1_norm"], lw["ffn1_w_gate"], lw["ffn1_w_up"], lw["ffn1_w_down"])
    (ckv, krope, kslab, kb, vb, ki, qa, qb, kb16, vb16, qi, ik2, wi) = _proj(h, lw, tables, t)
    per_batch = lambda a: a.reshape(b, t, a.shape[-1])
    if past is None:
        ckv_all, kslab_all = ckv, kslab
        kb_all, vb_all, ik2_all = per_batch(kb16), per_batch(vb16), per_batch(ik2)
        s_real = t
    else:
        p_ckv, p_krope, p_kb, p_vb, p_ki = past
        s_real = p_ckv.shape[1] + t
        cat = lambda old, new: jnp.concatenate([old, per_batch(new)], axis=1)
        ckv_all = cat(p_ckv, ckv).reshape(b * s_real, MLA_KV_LORA)
        p_kslab = jnp.pad(p_krope, ((0, 0), (0, 0), (MLA_NOPE, LANE - MLA_QK)))
        kslab_all = cat(p_kslab, kslab).reshape(b * s_real, LANE)
        kb_all = cat(p_kb.reshape(b, -1, DSA_WIDTH).astype(BF16), kb16)
        vb_all = cat(p_vb.reshape(b, -1, DSA_WIDTH).astype(BF16), vb16)
        ik2_all = cat(jnp.tile(p_ki, (1, 1, 2)).astype(BF16), ik2)
    ka, va = _mla_kv(ckv_all, kslab_all, lw)
    s_pad_a = pl.cdiv(s_real, tk_mla) * tk_mla
    ka = _pad_keys(ka.reshape(b, s_real, -1), s_pad_a)
    va = _pad_keys(va.reshape(b, s_real, -1), s_pad_a)
    oa = _mla_attn(per_batch(qa), ka, va, q_off=q_off, s_real=s_real, tq=tq_mla, tk=tk_mla)
    s_pad_b = pl.cdiv(s_real, tk_dsa) * tk_dsa
    ob = _dsa_attn(per_batch(qb), per_batch(qi), per_batch(wi), _pad_keys(kb_all, s_pad_b), _pad_keys(vb_all, s_pad_b),
                   _pad_keys(ik2_all, s_pad_b), q_off=q_off, s_real=s_real, tq=tq_dsa, tk=tk_dsa)
    y = _ffn(h, lw["ffn2_norm"], lw["ffn2_w_gate"], lw["ffn2_w_up"], lw["ffn2_w_down"],
             attn=(oa.reshape(b * t, MLA_WIDTH), ob.reshape(b * t, DSA_WIDTH), lw["w_out_a"], lw["w_out_b"]))
    rows = (per_batch(ckv), per_batch(krope), kb.reshape(b, t, DSA_HEADS, DSA_HD),
            vb.reshape(b, t, DSA_HEADS, DSA_HD), per_batch(ki))
    return y, rows


def kernel(x_prompt, x_sample, cache_mla_ckv, cache_mla_krope, cache_dsa_k, cache_dsa_v, cache_idx_k,
           ffn1_norm, ffn1_w_gate, ffn1_w_up, ffn1_w_down, mix_norm, w_in,
           mla_q_norm, mla_w_uq, mla_kv_norm, mla_w_ukv, mla_q_gain, mla_k_gain,
           dsa_q_gain, dsa_k_gain, w_out, ffn2_norm, ffn2_w_gate, ffn2_w_up, ffn2_w_down):
    params = dict(ffn1_norm=ffn1_norm, ffn1_w_gate=ffn1_w_gate, ffn1_w_up=ffn1_w_up, ffn1_w_down=ffn1_w_down,
                  mix_norm=mix_norm, w_in=w_in, mla_q_norm=mla_q_norm, mla_w_uq=mla_w_uq, mla_kv_norm=mla_kv_norm,
                  mla_w_ukv=mla_w_ukv, mla_q_gain=mla_q_gain, mla_k_gain=mla_k_gain, dsa_q_gain=dsa_q_gain,
                  dsa_k_gain=dsa_k_gain, w_out=w_out, ffn2_norm=ffn2_norm, ffn2_w_gate=ffn2_w_gate,
                  ffn2_w_up=ffn2_w_up, ffn2_w_down=ffn2_w_down)
    depth = w_in.shape[0]
    d_model = x_prompt.shape[-1]
    weights = [_layer_weights(params, l) for l in range(depth)]

    b_p, t_p = x_prompt.shape[:2]
    n_p = b_p * t_p
    tabs_p = _rope_tables(jnp.arange(t_p, dtype=I32), _row_tile(n_p, 512))
    tile_p = dict(tq_mla=min(t_p, 512), tk_mla=min(t_p, 512), tq_dsa=min(t_p, 512), tk_dsa=min(t_p, 512))
    h_p = x_prompt.reshape(n_p, d_model)
    p_rows = []
    for l in range(depth):
        h_p, rows = _trunk_layer(h_p, weights[l], tabs_p, None, b=b_p, t=t_p, q_off=0, **tile_p)
        p_rows.append(rows)

    b_s, t_s = x_sample.shape[:2]
    n_s = b_s * t_s
    past_len = cache_mla_ckv.shape[2]
    tabs_s = _rope_tables(past_len + jnp.arange(t_s, dtype=I32), _row_tile(n_s, 512))
    tile_s = dict(tq_mla=LANE, tk_mla=3 * LANE, tq_dsa=LANE, tk_dsa=3 * LANE)
    h_s = x_sample.reshape(n_s, d_model)
    s_rows = []
    for l in range(depth):
        past = (cache_mla_ckv[l], cache_mla_krope[l], cache_dsa_k[l], cache_dsa_v[l], cache_idx_k[l])
        h_s, rows = _trunk_layer(h_s, weights[l], tabs_s, past, b=b_s, t=t_s, q_off=past_len, **tile_s)
        s_rows.append(rows)

    stack = lambda rows_by_layer, i: jnp.stack([r[i] for r in rows_by_layer])
    return (h_p.reshape(b_p, t_p, d_model), h_s.reshape(b_s, t_s, d_model),
            *[stack(p_rows, i) for i in range(5)], *[stack(s_rows, i) for i in range(5)])
```

```python
import functools

import jax
import jax.numpy as jnp
from jax import lax
from jax.experimental import pallas as pl
from jax.experimental.pallas import tpu as pltpu

F32 = jnp.float32
BF16 = jnp.bfloat16
I32 = jnp.int32

CHUNK_SHIFT = 6
ROPE_THETA = 500000.0
EPS = 1e-6
MLA_HEADS = 8
MLA_NOPE = 64
MLA_ROPE = 32
MLA_QK = MLA_NOPE + MLA_ROPE
MLA_V = 64
MLA_Q_LORA = 256
MLA_KV_LORA = 128
DSA_HEADS = 8
DSA_HD = 64
DSA_ROT = 16
IDX_HEADS = 4
IDX_HD = 64
IDX_W_SCALE = (IDX_HD * IDX_HEADS) ** -0.5
TOPK_MAX = 256
DSA_WIDTH = DSA_HEADS * DSA_HD
MLA_WIDTH = MLA_HEADS * MLA_V

LANE = 128
VMEM_LIMIT = 56 * 1024 * 1024

NEG = -1e30
F32_MAX = 3.4028234e38
LOG2E = 1.4426950408889634
INT_MIN = -(2 ** 31)
INT_MAX = 2 ** 31 - 1
KEY_NEG_INF = INT_MIN + 0x7FFFFF


def _nt_dot(a, b):
    return lax.dot_general(a, b, (((1,), (1,)), ((), ())), preferred_element_type=F32)


def _dot(a, b):
    return jnp.dot(a, b, preferred_element_type=F32)


def _rms(x, g):
    return x * lax.rsqrt(jnp.mean(x * x, axis=-1, keepdims=True) + EPS) * g


def _lane_iota(shape):
    return lax.broadcasted_iota(I32, shape, len(shape) - 1)


FFN_CHUNK = 256


def _ffn_body(x, g_ref, wg_ref, wu_ref, wd_ref, o_ref):
    xb = _rms(x, g_ref[...]).astype(BF16)
    d_ff = wg_ref.shape[1]
    acc = jnp.zeros(x.shape, F32)
    for c in range(d_ff // FFN_CHUNK):
        sl = slice(c * FFN_CHUNK, (c + 1) * FFN_CHUNK)
        gate = _dot(xb, wg_ref[:, sl])
        up = _dot(xb, wu_ref[:, sl])
        act = (gate * jax.nn.sigmoid(gate) * up).astype(BF16)
        acc = acc + _dot(act, wd_ref[sl, :])
    o_ref[...] = x + 0.5 * acc


def _ffn_kernel(x_ref, g_ref, wg_ref, wu_ref, wd_ref, o_ref):
    _ffn_body(x_ref[...], g_ref, wg_ref, wu_ref, wd_ref, o_ref)


def _out_ffn_kernel(h_ref, oa_ref, ob_ref, woa_ref, wob_ref, g_ref, wg_ref, wu_ref, wd_ref, o_ref):
    x = h_ref[...] + _dot(oa_ref[...], woa_ref[...]) + _dot(ob_ref[...], wob_ref[...])
    _ffn_body(x, g_ref, wg_ref, wu_ref, wd_ref, o_ref)


def _resident(shape):
    nd = len(shape)
    return pl.BlockSpec(shape, lambda *_: (0,) * nd, pipeline_mode=pl.Buffered(1))


def _row_tile(n, pref):
    for t in range(min(n, pref), 0, -16):
        if n % t == 0:
            return t
    raise ValueError(f"no row tile for {n} rows")


def _ffn(x, g, wg, wu, wd, attn=None):
    n, d = x.shape
    tm = _row_tile(n, 512)
    row = lambda w: pl.BlockSpec((tm, w), lambda i: (i, 0))
    w_specs = [_resident(g.shape), _resident(wg.shape), _resident(wu.shape), _resident(wd.shape)]
    params = pltpu.CompilerParams(dimension_semantics=("parallel",), vmem_limit_bytes=VMEM_LIMIT)
    out_shape = jax.ShapeDtypeStruct((n, d), F32)
    if attn is None:
        return pl.pallas_call(_ffn_kernel, grid=(n // tm,), in_specs=[row(d)] + w_specs, out_specs=row(d),
                              out_shape=out_shape, compiler_params=params, name="ffn")(x, g, wg, wu, wd)
    oa, ob, woa, wob = attn
    return pl.pallas_call(
        _out_ffn_kernel, grid=(n // tm,),
        in_specs=[row(d), row(oa.shape[1]), row(ob.shape[1]), _resident(woa.shape), _resident(wob.shape)] + w_specs,
        out_specs=row(d), out_shape=out_shape, compiler_params=params, name="out_ffn")(x, oa, ob, woa, wob, g, wg, wu, wd)


C_CQ = 0
C_CKV = C_CQ + MLA_Q_LORA
C_KR = C_CKV + MLA_KV_LORA
C_QB = C_KR + LANE
C_KB = C_QB + DSA_WIDTH
C_VB = C_KB + DSA_WIDTH
C_QI = C_VB + DSA_WIDTH
C_KI = C_QI + IDX_HEADS * IDX_HD
C_WI = C_KI + LANE
C_END = C_WI + LANE


def _rope_a(x, c, s):
    lane = _lane_iota(x.shape)
    partner = jnp.where(lane < MLA_NOPE + MLA_ROPE // 2, pltpu.roll(x, LANE - MLA_ROPE // 2, 1),
                        pltpu.roll(x, MLA_ROPE // 2, 1))
    return x * c + partner * s


def _rope_b(x, c, s):
    lane = _lane_iota(x.shape)
    half = DSA_ROT // 2
    partner = jnp.where((lane & (DSA_HD - 1)) < half, pltpu.roll(x, LANE - half, 1), pltpu.roll(x, half, 1))
    return x * c + partner * s


def _head96_norm(x, g):
    ms = jnp.sum(x * x, axis=-1, keepdims=True) * (1.0 / MLA_QK)
    return x * lax.rsqrt(ms + EPS) * g


def _head64_norm(x, g2):
    lane = _lane_iota(x.shape)
    lo = lane < DSA_HD
    sq = x * x
    s_lo = jnp.sum(jnp.where(lo, sq, 0.0), axis=-1, keepdims=True)
    s_hi = jnp.sum(jnp.where(lo, 0.0, sq), axis=-1, keepdims=True)
    ms = jnp.where(lo, s_lo, s_hi) * (1.0 / DSA_HD)
    return x * lax.rsqrt(ms + EPS) * g2


def _proj_kernel(h_ref, gmix_ref, win_ref, gq_ref, wuq_ref, gkv_ref, gqa_ref, gqb_ref, gkb_ref,
                 ca_ref, sa_ref, cb_ref, sb_ref,
                 ckv_ref, krope_ref, kslab_ref, kb_ref, vb_ref, ki_ref,
                 qa_ref, qb_ref, kb16_ref, vb16_ref, qi_ref, ik2_ref, wi_ref):
    u = _rms(h_ref[...], gmix_ref[...]).astype(BF16)
    ca, sa, cb, sb = ca_ref[...], sa_ref[...], cb_ref[...], sb_ref[...]

    def cols(start, width):
        return _dot(u, win_ref[:, start:start + width])

    cq = _rms(cols(C_CQ, MLA_Q_LORA), gq_ref[...]).astype(BF16)
    qa = _dot(cq, wuq_ref[...])
    qa_scale = MLA_QK ** -0.5 * LOG2E
    for h in range(MLA_HEADS):
        sl = slice(h * LANE, (h + 1) * LANE)
        qa_ref[:, sl] = (_head96_norm(_rope_a(qa[:, sl], ca, sa), gqa_ref[...]) * qa_scale).astype(BF16)

    ckv_ref[...] = _rms(cols(C_CKV, MLA_KV_LORA), gkv_ref[...])
    kslab = _rope_a(cols(C_KR, LANE), ca, sa)
    kslab_ref[...] = kslab
    krope_ref[...] = kslab[:, MLA_NOPE:MLA_NOPE + MLA_ROPE]

    qb = cols(C_QB, DSA_WIDTH)
    kb = cols(C_KB, DSA_WIDTH)
    qb_scale = DSA_HD ** -0.5 * LOG2E
    for p in range(DSA_WIDTH // LANE):
        sl = slice(p * LANE, (p + 1) * LANE)
        qb_ref[:, sl] = (_rope_b(_head64_norm(qb[:, sl], gqb_ref[...]), cb, sb) * qb_scale).astype(BF16)
        kp = _rope_b(_head64_norm(kb[:, sl], gkb_ref[...]), cb, sb)
        kb_ref[:, 2 * p:2 * p + 2, :] = kp.reshape(kp.shape[0], 2, DSA_HD)
        kb16_ref[:, sl] = kp.astype(BF16)
    vb = cols(C_VB, DSA_WIDTH)
    vb_ref[...] = vb.reshape(vb.shape[0], DSA_HEADS, DSA_HD)
    vb16_ref[...] = vb.astype(BF16)

    qi = cols(C_QI, IDX_HEADS * IDX_HD)
    for p in range(IDX_HEADS * IDX_HD // LANE):
        sl = slice(p * LANE, (p + 1) * LANE)
        qi_ref[:, sl] = _rope_b(qi[:, sl], cb, sb).astype(BF16)
    ik2 = _rope_b(cols(C_KI, LANE), cb, sb)
    ki_ref[...] = ik2[:, :IDX_HD]
    ik2_ref[...] = ik2.astype(BF16)
    wi_ref[...] = cols(C_WI, LANE) * IDX_W_SCALE


def _proj(h, lw, tables, t_seq):
    n, d = h.shape
    tm = _row_tile(n, 512)
    ca, sa, cb, sb = tables
    n_tab = ca.shape[0] // tm
    row = lambda w: pl.BlockSpec((tm, w), lambda i: (i, 0))
    tab = pl.BlockSpec((tm, LANE), lambda i: (i % n_tab, 0))
    consts = [lw["mix_norm"], lw["w_in"], lw["mla_q_norm"], lw["w_uq"], lw["mla_kv_norm"],
              lw["mla_q_gain"], lw["dsa_q_gain"], lw["dsa_k_gain"]]
    out_widths = [(MLA_KV_LORA, F32), (MLA_ROPE, F32), (LANE, F32), (DSA_WIDTH, F32), (DSA_WIDTH, F32), (IDX_HD, F32),
                  (MLA_HEADS * LANE, BF16), (DSA_WIDTH, BF16), (DSA_WIDTH, BF16), (DSA_WIDTH, BF16),
                  (IDX_HEADS * IDX_HD, BF16), (LANE, BF16), (LANE, F32)]
    per_head = (3, 4)
    heads_spec = pl.BlockSpec((tm, DSA_HEADS, DSA_HD), lambda i: (i, 0, 0))
    return pl.pallas_call(
        _proj_kernel, grid=(n // tm,),
        in_specs=[row(d)] + [_resident(c.shape) for c in consts] + [tab] * 4,
        out_specs=[heads_spec if i in per_head else row(w) for i, (w, _) in enumerate(out_widths)],
        out_shape=[jax.ShapeDtypeStruct((n, DSA_HEADS, DSA_HD) if i in per_head else (n, w), dt)
                   for i, (w, dt) in enumerate(out_widths)],
        compiler_params=pltpu.CompilerParams(dimension_semantics=("parallel",), vmem_limit_bytes=VMEM_LIMIT),
        name="proj")(h, *consts, ca, sa, cb, sb)


def _mla_kv_kernel(ckv_ref, kslab_ref, wn_ref, wv_ref, gk_ref, ka_ref, va_ref):
    c = ckv_ref[...].astype(BF16)
    kn = _dot(c, wn_ref[...])
    kslab = kslab_ref[...]
    for h in range(MLA_HEADS):
        ka_ref[h] = _head96_norm(kn[:, h * LANE:(h + 1) * LANE] + kslab, gk_ref[...]).astype(BF16)
    va_ref[...] = _dot(c, wv_ref[...]).astype(BF16)


def _mla_kv(ckv, kslab, lw):
    m = ckv.shape[0]
    tm = _row_tile(m, 512)
    row = lambda w: pl.BlockSpec((tm, w), lambda i: (i, 0))
    consts = [lw["w_ukv_nope"], lw["w_ukv_v"], lw["mla_k_gain"]]
    return pl.pallas_call(
        _mla_kv_kernel, grid=(m // tm,),
        in_specs=[row(MLA_KV_LORA), row(LANE)] + [_resident(c.shape) for c in consts],
        out_specs=[pl.BlockSpec((MLA_HEADS, tm, LANE), lambda i: (0, i, 0)), row(MLA_WIDTH)],
        out_shape=[jax.ShapeDtypeStruct((MLA_HEADS, m, LANE), BF16), jax.ShapeDtypeStruct((m, MLA_WIDTH), BF16)],
        compiler_params=pltpu.CompilerParams(dimension_semantics=("parallel",), vmem_limit_bytes=VMEM_LIMIT),
        name="mla_kv")(ckv, kslab, *consts)


SUBLANES = 8
FOLD_CHAINS = 4


def _visible_t(q0, k0, tk, tq, s_real):
    q_chunk = (q0 + lax.broadcasted_iota(I32, (tk, tq), 1)) >> CHUNK_SHIFT
    k_idx = k0 + lax.broadcasted_iota(I32, (tk, tq), 0)
    return ((k_idx >> CHUNK_SHIFT) <= q_chunk) & (k_idx < s_real)


def _fold_rows(x, reduce):
    groups = x.shape[0] // SUBLANES
    if groups % FOLD_CHAINS == 0 and groups > FOLD_CHAINS:
        x = reduce(x.reshape(groups // FOLD_CHAINS, FOLD_CHAINS * SUBLANES, x.shape[1]), axis=0)
        groups = FOLD_CHAINS
    return reduce(x.reshape(groups, SUBLANES, x.shape[1]), axis=0)


HEAD_V = 64
ONES_ROWS = 16
HEAD_ROWS = HEAD_V + ONES_ROWS
HEAD_GROUPS = (range(0, 4), range(4, 8))


def _head_values(vt_ref, kb, h):
    v_t = vt_ref[0, kb, h * HEAD_V:(h + 1) * HEAD_V, :]
    return jnp.concatenate([v_t, jnp.ones((ONES_ROWS, v_t.shape[1]), v_t.dtype)], axis=0)


def _softmax_update_t(s, v_t, m, acc):
    m_new = jnp.maximum(m, jnp.max(_fold_rows(s, jnp.max), axis=0, keepdims=True))
    p = jnp.exp2(s - m_new)
    return m_new, jnp.exp2(m - m_new) * acc + _dot(v_t, p.astype(BF16))


def _softmax_init(n_heads, tq):
    return (tuple(jnp.full((1, tq), NEG, F32) for _ in range(n_heads)),
            tuple(jnp.zeros((HEAD_ROWS, tq), F32) for _ in range(n_heads)))


def _write_heads_t(o_ref, accs):
    for h, a in enumerate(accs):
        o_ref[0, h * HEAD_V:(h + 1) * HEAD_V, :] = (a[:HEAD_V] / a[HEAD_V:HEAD_V + 1]).astype(o_ref.dtype)


def _pad_queries(arrays, tq):
    t = arrays[0].shape[1]
    t_pad = pl.cdiv(t, tq) * tq
    return [jnp.pad(a, ((0, 0), (0, t_pad - t), (0, 0))) for a in arrays] if t_pad != t else list(arrays)


def _visible_blocks(q0, tq, tk, s_real):
    n_full = jnp.minimum(((q0 >> CHUNK_SHIFT) + 1) << CHUNK_SHIFT, s_real) // tk
    vis_end = jnp.minimum((((q0 + tq - 1) >> CHUNK_SHIFT) + 1) << CHUNK_SHIFT, s_real)
    return n_full, (vis_end + tk - 1) // tk


def _mla_attn_kernel(q_ref, k_ref, vt_ref, o_ref, *, tq, tk, q_off, s_real):
    q0 = q_off + pl.program_id(1) * tq
    n_full, n_vis = _visible_blocks(q0, tq, tk, s_real)

    def attend_block(kb, carry, masked):
        ms, accs = carry
        k0 = pl.multiple_of(kb * tk, tk)
        if masked:
            cap = jnp.where(_visible_t(q0, k0, tk, tq, s_real), F32_MAX, NEG)
        new = []
        for group in HEAD_GROUPS:
            scores = {h: _nt_dot(k_ref[h, 0, pl.ds(k0, tk), :], q_ref[0, :, h * LANE:(h + 1) * LANE])
                      for h in group}
            if masked:
                scores = {h: jnp.minimum(s, cap) for h, s in scores.items()}
            new += [_softmax_update_t(scores[h], _head_values(vt_ref, kb, h), ms[h], accs[h]) for h in group]
        return tuple(n[0] for n in new), tuple(n[1] for n in new)

    carry = _softmax_init(MLA_HEADS, tq)
    carry = lax.fori_loop(0, n_full, functools.partial(attend_block, masked=False), carry)
    _, accs = lax.fori_loop(n_full, n_vis, functools.partial(attend_block, masked=True), carry)
    _write_heads_t(o_ref, accs)


def _key_blocks_t(v, tk):
    b, s_pad, w = v.shape
    return jnp.swapaxes(v.reshape(b, s_pad // tk, tk, w), 2, 3)


def _mla_attn(qa, ka, va, *, q_off, s_real, tq, tk):
    b, t, _ = qa.shape
    s_pad = ka.shape[2]
    (qa,) = _pad_queries([qa], tq)
    t_pad = qa.shape[1]
    kern = functools.partial(_mla_attn_kernel, tq=tq, tk=tk, q_off=q_off, s_real=s_real)
    o_t = pl.pallas_call(
        kern, grid=(b, t_pad // tq),
        in_specs=[pl.BlockSpec((1, tq, MLA_HEADS * LANE), lambda bi, qi: (bi, qi, 0)),
                  pl.BlockSpec((MLA_HEADS, 1, s_pad, LANE), lambda bi, qi: (0, bi, 0, 0), pipeline_mode=pl.Buffered(1)),
                  pl.BlockSpec((1, s_pad // tk, MLA_WIDTH, tk), lambda bi, qi: (bi, 0, 0, 0),
                               pipeline_mode=pl.Buffered(1))],
        out_specs=pl.BlockSpec((1, MLA_WIDTH, tq), lambda bi, qi: (bi, 0, qi)),
        out_shape=jax.ShapeDtypeStruct((b, MLA_WIDTH, t_pad), BF16),
        compiler_params=pltpu.CompilerParams(dimension_semantics=("parallel", "arbitrary"),
                                             vmem_limit_bytes=VMEM_LIMIT),
        name="mla_attn")(qa, ka, _key_blocks_t(va, tk))
    return jnp.swapaxes(o_t, 1, 2)[:, :t]


FALSE_POSITION_PROBES = 24
MAX_PROBES = 64


def _sortable(bits):
    return bits ^ ((bits >> 31) & INT_MAX)


def _dsa_kernel(qb_ref, qi_ref, wi_ref, kb_ref, vt_ref, ik2_ref, tri_ref, o_ref,
                key_ref, qm_ref, qim_ref, *, tq, tk, q_off, s_real, topk):
    q0 = q_off + pl.program_id(1) * tq
    n_full, n_vis = _visible_blocks(q0, tq, tk, s_real)
    lane = _lane_iota((tq, LANE))
    lo = lane < LANE // 2

    for h in range(DSA_HEADS):
        pair = qb_ref[0, :, (h // 2) * LANE:(h // 2 + 1) * LANE]
        qm_ref[h] = jnp.where(lo if h % 2 == 0 else ~lo, pair, jnp.zeros_like(pair))
    for h in range(IDX_HEADS):
        pair = qi_ref[0, :, (h // 2) * LANE:(h // 2 + 1) * LANE]
        qim_ref[h] = jnp.where(lo if h % 2 == 0 else ~lo, pair, jnp.zeros_like(pair))
    w_rows = [wi_ref[0, h:h + 1, :] for h in range(IDX_HEADS)]

    def score_block(kb, carry, masked):
        q_max, q_min = carry
        k0 = pl.multiple_of(kb * tk, tk)
        ik = ik2_ref[0, pl.ds(k0, tk), :]
        score = jnp.zeros((tk, tq), F32)
        for h in range(IDX_HEADS):
            score = score + jnp.maximum(_nt_dot(ik, qim_ref[h]), 0.0) * w_rows[h]
        score = jnp.where(score == 0.0, 0.0, score)
        below = above = score
        if masked:
            vis = _visible_t(q0, k0, tk, tq, s_real)
            below, above = jnp.where(vis, score, -jnp.inf), jnp.where(vis, score, jnp.inf)
        key_ref[kb] = _sortable(pltpu.bitcast(below, I32))
        return (jnp.maximum(q_max, _fold_rows(below, jnp.max)), jnp.minimum(q_min, _fold_rows(above, jnp.min)))

    carry = (jnp.full((SUBLANES, tq), -jnp.inf, F32), jnp.full((SUBLANES, tq), jnp.inf, F32))
    carry = lax.fori_loop(0, n_full, functools.partial(score_block, masked=False), carry)
    q_max, q_min = lax.fori_loop(n_full, n_vis, functools.partial(score_block, masked=True), carry)
    row_max = jnp.max(q_max, axis=0, keepdims=True)
    row_min = jnp.min(q_min, axis=0, keepdims=True)

    def count_ge(t):
        def body(kb, acc):
            return acc + _fold_rows(jnp.where(key_ref[kb] >= t, 1.0, 0.0), jnp.sum)
        return jnp.sum(lax.fori_loop(0, n_vis, body, jnp.zeros((SUBLANES, tq), F32)), axis=0, keepdims=True)

    kf = float(topk)
    q_pos = q0 + lax.broadcasted_iota(I32, (1, tq), 1)
    n_row = jnp.minimum(((q_pos >> CHUNK_SHIFT) + 1) << CHUNK_SHIFT, s_real).astype(F32)
    few = n_row < kf
    lo0 = _sortable(lax.bitcast_convert_type(row_min, I32))
    hi0 = _sortable(lax.bitcast_convert_type(row_max, I32)) + 1

    def finished(lo_k, hi_k, c_lo):
        return few | (c_lo == kf) | (hi_k == lo_k + 1)

    def probe_step(carry):
        it, _, lo_k, hi_k, c_lo, c_hi, g_lo, g_hi, last = carry
        v_lo = lax.bitcast_convert_type(_sortable(lo_k), F32)
        v_hi = lax.bitcast_convert_type(_sortable(hi_k), F32)
        a = jnp.log(c_lo * (1.0 / kf)) * g_lo
        b = jnp.log(kf / jnp.maximum(c_hi, 0.5)) * g_hi
        p = _sortable(lax.bitcast_convert_type(v_lo + (v_hi - v_lo) * (a / (a + b)), I32))
        p = jnp.where(it >= FALSE_POSITION_PROBES, (lo_k >> 1) + (hi_k >> 1) + (lo_k & hi_k & 1), p)
        p = jnp.where((it == 0) & (lo_k < 0) & (hi_k > 0), 0, p)
        p = jnp.where(lo_k == 0, 1, p)
        p = jnp.minimum(jnp.maximum(p, lo_k + 1), hi_k - 1)
        c = count_ge(p)
        open_ = ~finished(lo_k, hi_k, c_lo)
        up = open_ & (c >= kf)
        down = open_ & (c < kf)
        lo_k, c_lo = jnp.where(up, p, lo_k), jnp.where(up, c, c_lo)
        hi_k, c_hi = jnp.where(down, p, hi_k), jnp.where(down, c, c_hi)
        g_lo = jnp.where(down, jnp.where(last < 0.0, 0.5 * g_lo, 1.0), jnp.where(up, 1.0, g_lo))
        g_hi = jnp.where(up, jnp.where(last > 0.0, 0.5 * g_hi, 1.0), jnp.where(down, 1.0, g_hi))
        last = jnp.where(up, 1.0, jnp.where(down, -1.0, last))
        n_open = jnp.max(jnp.where(finished(lo_k, hi_k, c_lo), 0, 1))
        return it + 1, n_open, lo_k, hi_k, c_lo, c_hi, g_lo, g_hi, last

    ones = jnp.ones((1, tq), F32)
    init = (jnp.int32(0), jnp.max(jnp.where(finished(lo0, hi0, n_row), 0, 1)), lo0, hi0, n_row, 0.0 * ones,
            ones, ones, 0.0 * ones)
    final = lax.while_loop(lambda c: (c[1] > 0) & (c[0] < MAX_PROBES), probe_step, init)
    t, c_lo, c_hi = final[2], final[4], final[5]

    t = jnp.where(few, KEY_NEG_INF, t)
    need = jnp.where(few, 0.0, jnp.where(c_lo == kf, kf, kf - c_hi))

    def attend_block(kb, carry):
        tied_before, ms, accs = carry
        k0 = pl.multiple_of(kb * tk, tk)
        blk = key_ref[kb]
        tied = jnp.where(blk == t, 1.0, 0.0)
        tied_rank = (tied_before + _dot(tri_ref[...], tied.astype(BF16))) * tied
        cap = jnp.where((blk >= t) & (tied_rank <= need), F32_MAX, NEG)
        k_pairs = [kb_ref[0, pl.ds(k0, tk), p * LANE:(p + 1) * LANE] for p in range(DSA_HEADS // 2)]
        new = []
        for group in HEAD_GROUPS:
            scores = {h: _nt_dot(k_pairs[h // 2], qm_ref[h]) for h in group}
            new += [_softmax_update_t(jnp.minimum(scores[h], cap), _head_values(vt_ref, kb, h), ms[h], accs[h]) for h in group]
        return (tied_before + jnp.sum(_fold_rows(tied, jnp.sum), axis=0, keepdims=True),
                tuple(n[0] for n in new), tuple(n[1] for n in new))

    _, _, accs = lax.fori_loop(0, n_vis, attend_block, (jnp.zeros((1, tq), F32),) + _softmax_init(DSA_HEADS, tq))
    _write_heads_t(o_ref, accs)


def _dsa_attn(qb, qi, wi, kb, vb, ik2, *, q_off, s_real, tq, tk):
    b, t, _ = qb.shape
    s_pad = kb.shape[1]
    n_kb = s_pad // tk
    topk = min(TOPK_MAX, s_real // 4)
    qb, qi, wi = _pad_queries([qb, qi, wi], tq)
    t_pad = qb.shape[1]
    wi_t = jnp.swapaxes(wi[:, :, :SUBLANES], 1, 2)
    v_t = _key_blocks_t(vb, tk)
    tri = (lax.broadcasted_iota(I32, (tk, tk), 0) >= lax.broadcasted_iota(I32, (tk, tk), 1)).astype(BF16)
    qspec = lambda w: pl.BlockSpec((1, tq, w), lambda bi, i: (bi, i, 0))
    kspec = lambda w: pl.BlockSpec((1, s_pad, w), lambda bi, i: (bi, 0, 0), pipeline_mode=pl.Buffered(1))
    kern = functools.partial(_dsa_kernel, tq=tq, tk=tk, q_off=q_off, s_real=s_real, topk=topk)
    o_t = pl.pallas_call(
        kern, grid=(b, t_pad // tq),
        in_specs=[qspec(DSA_WIDTH), qspec(IDX_HEADS * IDX_HD),
                  pl.BlockSpec((1, SUBLANES, tq), lambda bi, i: (bi, 0, i)),
                  kspec(DSA_WIDTH),
                  pl.BlockSpec((1, n_kb, DSA_WIDTH, tk), lambda bi, i: (bi, 0, 0, 0), pipeline_mode=pl.Buffered(1)),
                  kspec(LANE), _resident((tk, tk))],
        out_specs=pl.BlockSpec((1, DSA_WIDTH, tq), lambda bi, i: (bi, 0, i)),
        out_shape=jax.ShapeDtypeStruct((b, DSA_WIDTH, t_pad), BF16),
        scratch_shapes=[pltpu.VMEM((n_kb, tk, tq), I32),
                        pltpu.VMEM((DSA_HEADS, tq, LANE), BF16), pltpu.VMEM((IDX_HEADS, tq, LANE), BF16)],
        compiler_params=pltpu.CompilerParams(dimension_semantics=("parallel", "arbitrary"),
                                             vmem_limit_bytes=VMEM_LIMIT),
        name="dsa_attn")(qb, qi, wi_t, kb, v_t, ik2, tri)
    return jnp.swapaxes(o_t, 1, 2)[:, :t]


def _pad_cols(w, width):
    return jnp.pad(w, ((0, 0), (0, width - w.shape[1])))


def _layer_weights(p, l):
    w_in = p["w_in"][l]
    off, pieces = 0, []
    for n in (MLA_Q_LORA, MLA_KV_LORA, MLA_ROPE, DSA_WIDTH, DSA_WIDTH, DSA_WIDTH, IDX_HEADS * IDX_HD, IDX_HD, IDX_HEADS):
        pieces.append(w_in[:, off:off + n])
        off += n
    c_q, c_kv, k_r, q_b, k_b, v_b, q_i, k_i, w_i = pieces
    k_r = jnp.pad(k_r, ((0, 0), (MLA_NOPE, LANE - MLA_QK)))
    w_in_p = jnp.concatenate([c_q, c_kv, k_r, q_b, k_b, v_b, q_i, k_i, k_i, _pad_cols(w_i, LANE)], axis=1)
    assert w_in_p.shape[1] == C_END

    d_lora = p["mla_w_uq"].shape[1]
    w_uq = p["mla_w_uq"][l].reshape(d_lora, MLA_HEADS, MLA_QK)
    w_uq = jnp.pad(w_uq, ((0, 0), (0, 0), (0, LANE - MLA_QK))).reshape(d_lora, MLA_HEADS * LANE)
    w_ukv = p["mla_w_ukv"][l].reshape(MLA_KV_LORA, MLA_HEADS, MLA_NOPE + MLA_V)
    w_nope = jnp.pad(w_ukv[:, :, :MLA_NOPE], ((0, 0), (0, 0), (0, LANE - MLA_NOPE))).reshape(MLA_KV_LORA, MLA_HEADS * LANE)
    w_v = w_ukv[:, :, MLA_NOPE:].reshape(MLA_KV_LORA, MLA_WIDTH)
    w_out = p["w_out"][l]

    row = lambda g: g[l][None, :].astype(F32)
    pad96 = lambda g: jnp.pad(g[l].astype(F32), (0, LANE - MLA_QK))[None, :]
    twice = lambda g: jnp.tile(g[l].astype(F32), 2)[None, :]
    lw = {
        "w_in": w_in_p.astype(BF16), "w_uq": w_uq.astype(BF16), "w_ukv_nope": w_nope.astype(BF16),
        "w_ukv_v": w_v.astype(BF16), "w_out_a": w_out[:MLA_WIDTH].astype(BF16), "w_out_b": w_out[MLA_WIDTH:].astype(BF16),
        "mix_norm": row(p["mix_norm"]), "mla_q_norm": row(p["mla_q_norm"]), "mla_kv_norm": row(p["mla_kv_norm"]),
        "mla_q_gain": pad96(p["mla_q_gain"]), "mla_k_gain": pad96(p["mla_k_gain"]),
        "dsa_q_gain": twice(p["dsa_q_gain"]), "dsa_k_gain": twice(p["dsa_k_gain"]),
    }
    for f in ("ffn1", "ffn2"):
        lw[f + "_norm"] = row(p[f + "_norm"])
        for w in ("w_gate", "w_up", "w_down"):
            lw[f + "_" + w] = p[f + "_" + w][l].astype(BF16)
    return lw


def _rope_tables(pos, rows):
    def cs(rot):
        inv = 1.0 / (ROPE_THETA ** (jnp.arange(0, rot, 2, dtype=F32) / rot))
        ang = pos.astype(F32)[:, None] * inv[None, :]
        return jnp.cos(ang), jnp.sin(ang)

    t = pos.shape[0]
    cos_a, sin_a = cs(MLA_ROPE)
    ones = lambda w: jnp.ones((t, w), F32)
    zeros = lambda w: jnp.zeros((t, w), F32)
    ca = jnp.concatenate([ones(MLA_NOPE), cos_a, cos_a, ones(LANE - MLA_QK)], axis=1)
    sa = jnp.concatenate([zeros(MLA_NOPE), -sin_a, sin_a, zeros(LANE - MLA_QK)], axis=1)
    cos_b, sin_b = cs(DSA_ROT)
    cb = jnp.tile(jnp.concatenate([cos_b, cos_b, ones(DSA_HD - DSA_ROT)], axis=1), (1, 2))
    sb = jnp.tile(jnp.concatenate([-sin_b, sin_b, zeros(DSA_HD - DSA_ROT)], axis=1), (1, 2))
    reps = max(1, rows // t)
    return tuple(jnp.tile(x, (reps, 1)) for x in (ca, sa, cb, sb))


def _pad_keys(x, s_pad):
    return jnp.pad(x, ((0, 0), (0, s_pad - x.shape[1]), (0, 0)))


def _trunk_layer(x, lw, tables, past, *, b, t, q_off, tq_mla, tk_mla, tq_dsa, tk_dsa):
    h = _ffn(x, lw["ffn1_norm"], lw["ffn1_w_gate"], lw["ffn1_w_up"], lw["ffn1_w_down"])
    (ckv, krope, kslab, kb, vb, ki, qa, qb, kb16, vb16, qi, ik2, wi) = _proj(h, lw, tables, t)
    per_batch = lambda a: a.reshape(b, t, a.shape[-1])
    if past is None:
        ckv_all, kslab_all = ckv, kslab
        kb_all, vb_all, ik2_all = per_batch(kb16), per_batch(vb16), per_batch(ik2)
        s_real = t
    else:
        p_ckv, p_krope, p_kb, p_vb, p_ki = past
        s_real = p_ckv.shape[1] + t
        cat = lambda old, new: jnp.concatenate([old, per_batch(new)], axis=1)
        ckv_all = cat(p_ckv, ckv).reshape(b * s_real, MLA_KV_LORA)
        p_kslab = jnp.pad(p_krope, ((0, 0), (0, 0), (MLA_NOPE, LANE - MLA_QK)))
        kslab_all = cat(p_kslab, kslab).reshape(b * s_real, LANE)
        kb_all = cat(p_kb.reshape(b, -1, DSA_WIDTH).astype(BF16), kb16)
        vb_all = cat(p_vb.reshape(b, -1, DSA_WIDTH).astype(BF16), vb16)
        ik2_all = cat(jnp.tile(p_ki, (1, 1, 2)).astype(BF16), ik2)
    ka, va = _mla_kv(ckv_all, kslab_all, lw)
    s_pad_a = pl.cdiv(s_real, tk_mla) * tk_mla
    ka = jnp.pad(ka.reshape(MLA_HEADS, b, s_real, LANE), ((0, 0), (0, 0), (0, s_pad_a - s_real), (0, 0)))
    va = _pad_keys(va.reshape(b, s_real, -1), s_pad_a)
    oa = _mla_attn(per_batch(qa), ka, va, q_off=q_off, s_real=s_real, tq=tq_mla, tk=tk_mla)
    s_pad_b = pl.cdiv(s_real, tk_dsa) * tk_dsa
    ob = _dsa_attn(per_batch(qb), per_batch(qi), per_batch(wi), _pad_keys(kb_all, s_pad_b), _pad_keys(vb_all, s_pad_b),
                   _pad_keys(ik2_all, s_pad_b), q_off=q_off, s_real=s_real, tq=tq_dsa, tk=tk_dsa)
    y = _ffn(h, lw["ffn2_norm"], lw["ffn2_w_gate"], lw["ffn2_w_up"], lw["ffn2_w_down"],
             attn=(oa.reshape(b * t, MLA_WIDTH), ob.reshape(b * t, DSA_WIDTH), lw["w_out_a"], lw["w_out_b"]))
    rows = (per_batch(ckv), per_batch(krope), kb.reshape(b, t, DSA_HEADS, DSA_HD),
            vb.reshape(b, t, DSA_HEADS, DSA_HD), per_batch(ki))
    return y, rows


def kernel(x_prompt, x_sample, cache_mla_ckv, cache_mla_krope, cache_dsa_k, cache_dsa_v, cache_idx_k,
           ffn1_norm, ffn1_w_gate, ffn1_w_up, ffn1_w_down, mix_norm, w_in,
           mla_q_norm, mla_w_uq, mla_kv_norm, mla_w_ukv, mla_q_gain, mla_k_gain,
           dsa_q_gain, dsa_k_gain, w_out, ffn2_norm, ffn2_w_gate, ffn2_w_up, ffn2_w_down):
    params = dict(ffn1_norm=ffn1_norm, ffn1_w_gate=ffn1_w_gate, ffn1_w_up=ffn1_w_up, ffn1_w_down=ffn1_w_down,
                  mix_norm=mix_norm, w_in=w_in, mla_q_norm=mla_q_norm, mla_w_uq=mla_w_uq, mla_kv_norm=mla_kv_norm,
                  mla_w_ukv=mla_w_ukv, mla_q_gain=mla_q_gain, mla_k_gain=mla_k_gain, dsa_q_gain=dsa_q_gain,
                  dsa_k_gain=dsa_k_gain, w_out=w_out, ffn2_norm=ffn2_norm, ffn2_w_gate=ffn2_w_gate,
                  ffn2_w_up=ffn2_w_up, ffn2_w_down=ffn2_w_down)
    depth = w_in.shape[0]
    d_model = x_prompt.shape[-1]
    weights = [_layer_weights(params, l) for l in range(depth)]

    b_p, t_p = x_prompt.shape[:2]
    n_p = b_p * t_p
    tabs_p = _rope_tables(jnp.arange(t_p, dtype=I32), _row_tile(n_p, 512))
    tile_p = dict(tq_mla=min(t_p, 512), tk_mla=min(t_p, 512), tq_dsa=min(t_p, 512), tk_dsa=min(t_p, 512))
    h_p = x_prompt.reshape(n_p, d_model)
    p_rows = []
    for l in range(depth):
        h_p, rows = _trunk_layer(h_p, weights[l], tabs_p, None, b=b_p, t=t_p, q_off=0, **tile_p)
        p_rows.append(rows)

    b_s, t_s = x_sample.shape[:2]
    n_s = b_s * t_s
    past_len = cache_mla_ckv.shape[2]
    tabs_s = _rope_tables(past_len + jnp.arange(t_s, dtype=I32), _row_tile(n_s, 512))
    tile_s = dict(tq_mla=LANE, tk_mla=3 * LANE, tq_dsa=LANE, tk_dsa=3 * LANE)
    h_s = x_sample.reshape(n_s, d_model)
    s_rows = []
    for l in range(depth):
        past = (cache_mla_ckv[l], cache_mla_krope[l], cache_dsa_k[l], cache_dsa_v[l], cache_idx_k[l])
        h_s, rows = _trunk_layer(h_s, weights[l], tabs_s, past, b=b_s, t=t_s, q_off=past_len, **tile_s)
        s_rows.append(rows)

    stack = lambda rows_by_layer, i: jnp.stack([r[i] for r in rows_by_layer])
    return (h_p.reshape(b_p, t_p, d_model), h_s.reshape(b_s, t_s, d_model),
            *[stack(p_rows, i) for i in range(5)], *[stack(s_rows, i) for i in range(5)])
```

```python
import functools

import jax
import jax.numpy as jnp
from jax import lax
from jax.experimental import pallas as pl
from jax.experimental.pallas import tpu as pltpu

F32 = jnp.float32
BF16 = jnp.bfloat16
I32 = jnp.int32

CHUNK_SHIFT = 6
ROPE_THETA = 500000.0
EPS = 1e-6
MLA_HEADS = 8
MLA_NOPE = 64
MLA_ROPE = 32
MLA_QK = MLA_NOPE + MLA_ROPE
MLA_V = 64
MLA_Q_LORA = 256
MLA_KV_LORA = 128
DSA_HEADS = 8
DSA_HD = 64
DSA_ROT = 16
IDX_HEADS = 4
IDX_HD = 64
IDX_W_SCALE = (IDX_HD * IDX_HEADS) ** -0.5
TOPK_MAX = 256
DSA_WIDTH = DSA_HEADS * DSA_HD
MLA_WIDTH = MLA_HEADS * MLA_V

LANE = 128
VMEM_LIMIT = 56 * 1024 * 1024

NEG = -1e30
F32_MAX = 3.4028234e38
LOG2E = 1.4426950408889634
INT_MIN = -(2 ** 31)
INT_MAX = 2 ** 31 - 1
KEY_NEG_INF = INT_MIN + 0x7FFFFF


def _nt_dot(a, b):
    return lax.dot_general(a, b, (((1,), (1,)), ((), ())), preferred_element_type=F32)


def _dot(a, b):
    return jnp.dot(a, b, preferred_element_type=F32)


def _rms(x, g):
    return x * lax.rsqrt(jnp.mean(x * x, axis=-1, keepdims=True) + EPS) * g


def _lane_iota(shape):
    return lax.broadcasted_iota(I32, shape, len(shape) - 1)


FFN_CHUNK = 256


def _ffn_body(x, g_ref, wg_ref, wu_ref, wd_ref, o_ref):
    xb = _rms(x, g_ref[...]).astype(BF16)
    d_ff = wg_ref.shape[1]
    acc = jnp.zeros(x.shape, F32)
    for c in range(d_ff // FFN_CHUNK):
        sl = slice(c * FFN_CHUNK, (c + 1) * FFN_CHUNK)
        gate = _dot(xb, wg_ref[:, sl])
        up = _dot(xb, wu_ref[:, sl])
        act = (gate * jax.nn.sigmoid(gate) * up).astype(BF16)
        acc = acc + _dot(act, wd_ref[sl, :])
    o_ref[...] = x + 0.5 * acc


def _ffn_kernel(x_ref, g_ref, wg_ref, wu_ref, wd_ref, o_ref):
    _ffn_body(x_ref[...], g_ref, wg_ref, wu_ref, wd_ref, o_ref)


def _out_ffn_kernel(h_ref, oa_ref, ob_ref, woa_ref, wob_ref, g_ref, wg_ref, wu_ref, wd_ref, o_ref):
    x = h_ref[...] + _dot(oa_ref[...], woa_ref[...]) + _dot(ob_ref[...], wob_ref[...])
    _ffn_body(x, g_ref, wg_ref, wu_ref, wd_ref, o_ref)


def _resident(shape):
    nd = len(shape)
    return pl.BlockSpec(shape, lambda *_: (0,) * nd, pipeline_mode=pl.Buffered(1))


def _row_tile(n, pref):
    for t in range(min(n, pref), 0, -16):
        if n % t == 0:
            return t
    raise ValueError(f"no row tile for {n} rows")


def _ffn(x, g, wg, wu, wd, attn=None):
    n, d = x.shape
    tm = _row_tile(n, 512)
    row = lambda w: pl.BlockSpec((tm, w), lambda i: (i, 0))
    w_specs = [_resident(g.shape), _resident(wg.shape), _resident(wu.shape), _resident(wd.shape)]
    params = pltpu.CompilerParams(dimension_semantics=("parallel",), vmem_limit_bytes=VMEM_LIMIT)
    out_shape = jax.ShapeDtypeStruct((n, d), F32)
    if attn is None:
        return pl.pallas_call(_ffn_kernel, grid=(n // tm,), in_specs=[row(d)] + w_specs, out_specs=row(d),
                              out_shape=out_shape, compiler_params=params, name="ffn")(x, g, wg, wu, wd)
    oa, ob, woa, wob = attn
    return pl.pallas_call(
        _out_ffn_kernel, grid=(n // tm,),
        in_specs=[row(d), row(oa.shape[1]), row(ob.shape[1]), _resident(woa.shape), _resident(wob.shape)] + w_specs,
        out_specs=row(d), out_shape=out_shape, compiler_params=params, name="out_ffn")(x, oa, ob, woa, wob, g, wg, wu, wd)


C_CQ = 0
C_CKV = C_CQ + MLA_Q_LORA
C_KR = C_CKV + MLA_KV_LORA
C_QB = C_KR + LANE
C_KB = C_QB + DSA_WIDTH
C_VB = C_KB + DSA_WIDTH
C_QI = C_VB + DSA_WIDTH
C_KI = C_QI + IDX_HEADS * IDX_HD
C_WI = C_KI + LANE
C_END = C_WI + LANE


def _rope_a(x, c, s):
    lane = _lane_iota(x.shape)
    partner = jnp.where(lane < MLA_NOPE + MLA_ROPE // 2, pltpu.roll(x, LANE - MLA_ROPE // 2, 1),
                        pltpu.roll(x, MLA_ROPE // 2, 1))
    return x * c + partner * s


def _rope_b(x, c, s):
    lane = _lane_iota(x.shape)
    half = DSA_ROT // 2
    partner = jnp.where((lane & (DSA_HD - 1)) < half, pltpu.roll(x, LANE - half, 1), pltpu.roll(x, half, 1))
    return x * c + partner * s


def _head96_norm(x, g):
    ms = jnp.sum(x * x, axis=-1, keepdims=True) * (1.0 / MLA_QK)
    return x * lax.rsqrt(ms + EPS) * g


def _head64_norm(x, g2):
    lane = _lane_iota(x.shape)
    lo = lane < DSA_HD
    sq = x * x
    s_lo = jnp.sum(jnp.where(lo, sq, 0.0), axis=-1, keepdims=True)
    s_hi = jnp.sum(jnp.where(lo, 0.0, sq), axis=-1, keepdims=True)
    ms = jnp.where(lo, s_lo, s_hi) * (1.0 / DSA_HD)
    return x * lax.rsqrt(ms + EPS) * g2


def _proj_kernel(h_ref, gmix_ref, win_ref, gq_ref, wuq_ref, gkv_ref, gqa_ref, gqb_ref, gkb_ref,
                 ca_ref, sa_ref, cb_ref, sb_ref,
                 ckv_ref, krope_ref, kslab_ref, kb_ref, vb_ref, ki_ref,
                 qa_ref, qb_ref, kb16_ref, vb16_ref, qi_ref, ik2_ref, wi_ref):
    u = _rms(h_ref[...], gmix_ref[...]).astype(BF16)
    ca, sa, cb, sb = ca_ref[...], sa_ref[...], cb_ref[...], sb_ref[...]

    def cols(start, width):
        return _dot(u, win_ref[:, start:start + width])

    cq = _rms(cols(C_CQ, MLA_Q_LORA), gq_ref[...]).astype(BF16)
    qa = _dot(cq, wuq_ref[...])
    qa_scale = MLA_QK ** -0.5 * LOG2E
    for h in range(MLA_HEADS):
        sl = slice(h * LANE, (h + 1) * LANE)
        qa_ref[:, sl] = (_head96_norm(_rope_a(qa[:, sl], ca, sa), gqa_ref[...]) * qa_scale).astype(BF16)

    ckv_ref[...] = _rms(cols(C_CKV, MLA_KV_LORA), gkv_ref[...])
    kslab = _rope_a(cols(C_KR, LANE), ca, sa)
    kslab_ref[...] = kslab
    krope_ref[...] = kslab[:, MLA_NOPE:MLA_NOPE + MLA_ROPE]

    qb = cols(C_QB, DSA_WIDTH)
    kb = cols(C_KB, DSA_WIDTH)
    qb_scale = DSA_HD ** -0.5 * LOG2E
    for p in range(DSA_WIDTH // LANE):
        sl = slice(p * LANE, (p + 1) * LANE)
        qb_ref[:, sl] = (_rope_b(_head64_norm(qb[:, sl], gqb_ref[...]), cb, sb) * qb_scale).astype(BF16)
        kp = _rope_b(_head64_norm(kb[:, sl], gkb_ref[...]), cb, sb)
        kb_ref[:, 2 * p:2 * p + 2, :] = kp.reshape(kp.shape[0], 2, DSA_HD)
        kb16_ref[:, sl] = kp.astype(BF16)
    vb = cols(C_VB, DSA_WIDTH)
    vb_ref[...] = vb.reshape(vb.shape[0], DSA_HEADS, DSA_HD)
    vb16_ref[...] = vb.astype(BF16)

    qi = cols(C_QI, IDX_HEADS * IDX_HD)
    for p in range(IDX_HEADS * IDX_HD // LANE):
        sl = slice(p * LANE, (p + 1) * LANE)
        qi_ref[:, sl] = _rope_b(qi[:, sl], cb, sb).astype(BF16)
    ik2 = _rope_b(cols(C_KI, LANE), cb, sb)
    ki_ref[...] = ik2[:, :IDX_HD]
    ik2_ref[...] = ik2.astype(BF16)
    wi_ref[...] = cols(C_WI, LANE) * IDX_W_SCALE


def _proj(h, lw, tables, t_seq):
    n, d = h.shape
    tm = _row_tile(n, 512)
    ca, sa, cb, sb = tables
    n_tab = ca.shape[0] // tm
    row = lambda w: pl.BlockSpec((tm, w), lambda i: (i, 0))
    tab = pl.BlockSpec((tm, LANE), lambda i: (i % n_tab, 0))
    consts = [lw["mix_norm"], lw["w_in"], lw["mla_q_norm"], lw["w_uq"], lw["mla_kv_norm"],
              lw["mla_q_gain"], lw["dsa_q_gain"], lw["dsa_k_gain"]]
    out_widths = [(MLA_KV_LORA, F32), (MLA_ROPE, F32), (LANE, F32), (DSA_WIDTH, F32), (DSA_WIDTH, F32), (IDX_HD, F32),
                  (MLA_HEADS * LANE, BF16), (DSA_WIDTH, BF16), (DSA_WIDTH, BF16), (DSA_WIDTH, BF16),
                  (IDX_HEADS * IDX_HD, BF16), (LANE, BF16), (LANE, F32)]
    per_head = (3, 4)
    heads_spec = pl.BlockSpec((tm, DSA_HEADS, DSA_HD), lambda i: (i, 0, 0))
    return pl.pallas_call(
        _proj_kernel, grid=(n // tm,),
        in_specs=[row(d)] + [_resident(c.shape) for c in consts] + [tab] * 4,
        out_specs=[heads_spec if i in per_head else row(w) for i, (w, _) in enumerate(out_widths)],
        out_shape=[jax.ShapeDtypeStruct((n, DSA_HEADS, DSA_HD) if i in per_head else (n, w), dt)
                   for i, (w, dt) in enumerate(out_widths)],
        compiler_params=pltpu.CompilerParams(dimension_semantics=("parallel",), vmem_limit_bytes=VMEM_LIMIT),
        name="proj")(h, *consts, ca, sa, cb, sb)


def _mla_kv_kernel(ckv_ref, kslab_ref, wn_ref, wv_ref, gk_ref, ka_ref, va_ref):
    c = ckv_ref[...].astype(BF16)
    kn = _dot(c, wn_ref[...])
    kslab = kslab_ref[...]
    for h in range(MLA_HEADS):
        sl = slice(h * LANE, (h + 1) * LANE)
        ka_ref[:, sl] = _head96_norm(kn[:, sl] + kslab, gk_ref[...]).astype(BF16)
    va_ref[...] = _dot(c, wv_ref[...]).astype(BF16)


def _mla_kv(ckv, kslab, lw):
    m = ckv.shape[0]
    tm = _row_tile(m, 512)
    row = lambda w: pl.BlockSpec((tm, w), lambda i: (i, 0))
    consts = [lw["w_ukv_nope"], lw["w_ukv_v"], lw["mla_k_gain"]]
    return pl.pallas_call(
        _mla_kv_kernel, grid=(m // tm,),
        in_specs=[row(MLA_KV_LORA), row(LANE)] + [_resident(c.shape) for c in consts],
        out_specs=[row(MLA_HEADS * LANE), row(MLA_WIDTH)],
        out_shape=[jax.ShapeDtypeStruct((m, MLA_HEADS * LANE), BF16), jax.ShapeDtypeStruct((m, MLA_WIDTH), BF16)],
        compiler_params=pltpu.CompilerParams(dimension_semantics=("parallel",), vmem_limit_bytes=VMEM_LIMIT),
        name="mla_kv")(ckv, kslab, *consts)


SUBLANES = 8
FOLD_CHAINS = 4


def _visible_t(q0, k0, tk, tq, s_real):
    q_chunk = (q0 + lax.broadcasted_iota(I32, (tk, tq), 1)) >> CHUNK_SHIFT
    k_idx = k0 + lax.broadcasted_iota(I32, (tk, tq), 0)
    return ((k_idx >> CHUNK_SHIFT) <= q_chunk) & (k_idx < s_real)


def _fold_rows(x, reduce):
    groups = x.shape[0] // SUBLANES
    if groups % FOLD_CHAINS == 0 and groups > FOLD_CHAINS:
        x = reduce(x.reshape(groups // FOLD_CHAINS, FOLD_CHAINS * SUBLANES, x.shape[1]), axis=0)
        groups = FOLD_CHAINS
    return reduce(x.reshape(groups, SUBLANES, x.shape[1]), axis=0)


HEAD_V = 64
ONES_ROWS = 16
HEAD_ROWS = HEAD_V + ONES_ROWS
HEAD_GROUPS = (range(0, 4), range(4, 8))


def _head_values(vt_ref, kb, h):
    v_t = vt_ref[0, kb, h * HEAD_V:(h + 1) * HEAD_V, :]
    return jnp.concatenate([v_t, jnp.ones((ONES_ROWS, v_t.shape[1]), v_t.dtype)], axis=0)


def _softmax_update_t(s, v_t, m, acc):
    m_new = jnp.maximum(m, jnp.max(_fold_rows(s, jnp.max), axis=0, keepdims=True))
    p = jnp.exp2(s - m_new)
    return m_new, jnp.exp2(m - m_new) * acc + _dot(v_t, p.astype(BF16))


def _softmax_init(n_heads, tq):
    return (tuple(jnp.full((1, tq), NEG, F32) for _ in range(n_heads)),
            tuple(jnp.zeros((HEAD_ROWS, tq), F32) for _ in range(n_heads)))


def _write_heads_t(o_ref, accs):
    for h, a in enumerate(accs):
        o_ref[0, h * HEAD_V:(h + 1) * HEAD_V, :] = (a[:HEAD_V] / a[HEAD_V:HEAD_V + 1]).astype(o_ref.dtype)


def _pad_queries(arrays, tq):
    t = arrays[0].shape[1]
    t_pad = pl.cdiv(t, tq) * tq
    return [jnp.pad(a, ((0, 0), (0, t_pad - t), (0, 0))) for a in arrays] if t_pad != t else list(arrays)


def _visible_blocks(q0, tq, tk, s_real):
    n_full = jnp.minimum(((q0 >> CHUNK_SHIFT) + 1) << CHUNK_SHIFT, s_real) // tk
    vis_end = jnp.minimum((((q0 + tq - 1) >> CHUNK_SHIFT) + 1) << CHUNK_SHIFT, s_real)
    return n_full, (vis_end + tk - 1) // tk


def _mla_attn_kernel(q_ref, k_ref, vt_ref, o_ref, *, tq, tk, q_off, s_real):
    q0 = q_off + pl.program_id(1) * tq
    n_full, n_vis = _visible_blocks(q0, tq, tk, s_real)

    def attend_block(kb, carry, masked):
        ms, accs = carry
        k0 = pl.multiple_of(kb * tk, tk)
        if masked:
            cap = jnp.where(_visible_t(q0, k0, tk, tq, s_real), F32_MAX, NEG)
        new = []
        for group in HEAD_GROUPS:
            scores = {h: _nt_dot(k_ref[0, pl.ds(k0, tk), h * LANE:(h + 1) * LANE], q_ref[0, :, h * LANE:(h + 1) * LANE])
                      for h in group}
            if masked:
                scores = {h: jnp.minimum(s, cap) for h, s in scores.items()}
            new += [_softmax_update_t(scores[h], _head_values(vt_ref, kb, h), ms[h], accs[h]) for h in group]
        return tuple(n[0] for n in new), tuple(n[1] for n in new)

    carry = _softmax_init(MLA_HEADS, tq)
    carry = lax.fori_loop(0, n_full, functools.partial(attend_block, masked=False), carry)
    _, accs = lax.fori_loop(n_full, n_vis, functools.partial(attend_block, masked=True), carry)
    _write_heads_t(o_ref, accs)


def _key_blocks_t(v, tk):
    b, s_pad, w = v.shape
    return jnp.swapaxes(v.reshape(b, s_pad // tk, tk, w), 2, 3)


def _mla_attn(qa, ka, va, *, q_off, s_real, tq, tk):
    b, t, _ = qa.shape
    s_pad = ka.shape[1]
    (qa,) = _pad_queries([qa], tq)
    t_pad = qa.shape[1]
    kern = functools.partial(_mla_attn_kernel, tq=tq, tk=tk, q_off=q_off, s_real=s_real)
    o_t = pl.pallas_call(
        kern, grid=(b, t_pad // tq),
        in_specs=[pl.BlockSpec((1, tq, MLA_HEADS * LANE), lambda bi, qi: (bi, qi, 0)),
                  pl.BlockSpec((1, s_pad, MLA_HEADS * LANE), lambda bi, qi: (bi, 0, 0), pipeline_mode=pl.Buffered(1)),
                  pl.BlockSpec((1, s_pad // tk, MLA_WIDTH, tk), lambda bi, qi: (bi, 0, 0, 0),
                               pipeline_mode=pl.Buffered(1))],
        out_specs=pl.BlockSpec((1, MLA_WIDTH, tq), lambda bi, qi: (bi, 0, qi)),
        out_shape=jax.ShapeDtypeStruct((b, MLA_WIDTH, t_pad), BF16),
        compiler_params=pltpu.CompilerParams(dimension_semantics=("parallel", "arbitrary"),
                                             vmem_limit_bytes=VMEM_LIMIT),
        name="mla_attn")(qa, ka, _key_blocks_t(va, tk))
    return jnp.swapaxes(o_t, 1, 2)[:, :t]


FALSE_POSITION_PROBES = 24
MAX_PROBES = 64
COUNT_ROWS = 128


def _sortable(bits):
    return bits ^ ((bits >> 31) & INT_MAX)


def _dsa_kernel(qb_ref, qi_ref, wi_ref, kb_ref, vt_ref, ik2_ref, tri_ref, o_ref,
                key_ref, qm_ref, qim_ref, *, tq, tk, q_off, s_real, topk):
    q0 = q_off + pl.program_id(1) * tq
    n_full, n_vis = _visible_blocks(q0, tq, tk, s_real)
    lane = _lane_iota((tq, LANE))
    lo = lane < LANE // 2

    for h in range(DSA_HEADS):
        pair = qb_ref[0, :, (h // 2) * LANE:(h // 2 + 1) * LANE]
        qm_ref[h] = jnp.where(lo if h % 2 == 0 else ~lo, pair, jnp.zeros_like(pair))
    for h in range(IDX_HEADS):
        pair = qi_ref[0, :, (h // 2) * LANE:(h // 2 + 1) * LANE]
        qim_ref[h] = jnp.where(lo if h % 2 == 0 else ~lo, pair, jnp.zeros_like(pair))
    w_rows = [wi_ref[0, h:h + 1, :] for h in range(IDX_HEADS)]

    def score_block(kb, carry, masked):
        q_max, q_min = carry
        k0 = pl.multiple_of(kb * tk, tk)
        ik = ik2_ref[0, pl.ds(k0, tk), :]
        score = jnp.zeros((tk, tq), F32)
        for h in range(IDX_HEADS):
            score = score + jnp.maximum(_nt_dot(ik, qim_ref[h]), 0.0) * w_rows[h]
        score = jnp.where(score == 0.0, 0.0, score)
        below = above = score
        if masked:
            vis = _visible_t(q0, k0, tk, tq, s_real)
            below, above = jnp.where(vis, score, -jnp.inf), jnp.where(vis, score, jnp.inf)
        key_ref[kb] = _sortable(pltpu.bitcast(below, I32))
        return (jnp.maximum(q_max, _fold_rows(below, jnp.max)), jnp.minimum(q_min, _fold_rows(above, jnp.min)))

    carry = (jnp.full((SUBLANES, tq), -jnp.inf, F32), jnp.full((SUBLANES, tq), jnp.inf, F32))
    carry = lax.fori_loop(0, n_full, functools.partial(score_block, masked=False), carry)
    q_max, q_min = lax.fori_loop(n_full, n_vis, functools.partial(score_block, masked=True), carry)
    row_max = jnp.max(q_max, axis=0, keepdims=True)
    row_min = jnp.min(q_min, axis=0, keepdims=True)

    def count_ge(t):
        def body(kb, acc):
            for r in range(0, tk, COUNT_ROWS):
                acc = acc + _fold_rows(jnp.where(key_ref[kb, r:r + COUNT_ROWS, :] >= t, 1.0, 0.0), jnp.sum)
            return acc
        return jnp.sum(lax.fori_loop(0, n_vis, body, jnp.zeros((SUBLANES, tq), F32)), axis=0, keepdims=True)

    kf = float(topk)
    q_pos = q0 + lax.broadcasted_iota(I32, (1, tq), 1)
    n_row = jnp.minimum(((q_pos >> CHUNK_SHIFT) + 1) << CHUNK_SHIFT, s_real).astype(F32)
    few = n_row < kf
    lo0 = _sortable(lax.bitcast_convert_type(row_min, I32))
    hi0 = _sortable(lax.bitcast_convert_type(row_max, I32)) + 1

    def finished(lo_k, hi_k, c_lo):
        return few | (c_lo == kf) | (hi_k == lo_k + 1)

    def probe_step(carry):
        it, _, lo_k, hi_k, c_lo, c_hi, g_lo, g_hi, last = carry
        v_lo = lax.bitcast_convert_type(_sortable(lo_k), F32)
        v_hi = lax.bitcast_convert_type(_sortable(hi_k), F32)
        a = jnp.log(c_lo * (1.0 / kf)) * g_lo
        b = jnp.log(kf / jnp.maximum(c_hi, 0.5)) * g_hi
        p = _sortable(lax.bitcast_convert_type(v_lo + (v_hi - v_lo) * (a / (a + b)), I32))
        p = jnp.where(it >= FALSE_POSITION_PROBES, (lo_k >> 1) + (hi_k >> 1) + (lo_k & hi_k & 1), p)
        p = jnp.where((it == 0) & (lo_k < 0) & (hi_k > 0), 0, p)
        p = jnp.where(lo_k == 0, 1, p)
        p = jnp.minimum(jnp.maximum(p, lo_k + 1), hi_k - 1)
        c = count_ge(p)
        open_ = ~finished(lo_k, hi_k, c_lo)
        up = open_ & (c >= kf)
        down = open_ & (c < kf)
        lo_k, c_lo = jnp.where(up, p, lo_k), jnp.where(up, c, c_lo)
        hi_k, c_hi = jnp.where(down, p, hi_k), jnp.where(down, c, c_hi)
        g_lo = jnp.where(down, jnp.where(last < 0.0, 0.5 * g_lo, 1.0), jnp.where(up, 1.0, g_lo))
        g_hi = jnp.where(up, jnp.where(last > 0.0, 0.5 * g_hi, 1.0), jnp.where(down, 1.0, g_hi))
        last = jnp.where(up, 1.0, jnp.where(down, -1.0, last))
        n_open = jnp.max(jnp.where(finished(lo_k, hi_k, c_lo), 0, 1))
        return it + 1, n_open, lo_k, hi_k, c_lo, c_hi, g_lo, g_hi, last

    ones = jnp.ones((1, tq), F32)
    init = (jnp.int32(0), jnp.max(jnp.where(finished(lo0, hi0, n_row), 0, 1)), lo0, hi0, n_row, 0.0 * ones,
            ones, ones, 0.0 * ones)
    final = lax.while_loop(lambda c: (c[1] > 0) & (c[0] < MAX_PROBES), probe_step, init)
    t, c_lo, c_hi = final[2], final[4], final[5]

    t = jnp.where(few, KEY_NEG_INF, t)
    need = jnp.where(few, 0.0, jnp.where(c_lo == kf, kf, kf - c_hi))

    def attend_block(kb, carry):
        tied_before, ms, accs = carry
        k0 = pl.multiple_of(kb * tk, tk)
        blk = key_ref[kb]
        tied = jnp.where(blk == t, 1.0, 0.0)
        tied_rank = (tied_before + _dot(tri_ref[...], tied.astype(BF16))) * tied
        cap = jnp.where((blk >= t) & (tied_rank <= need), F32_MAX, NEG)
        k_pairs = [kb_ref[0, pl.ds(k0, tk), p * LANE:(p + 1) * LANE] for p in range(DSA_HEADS // 2)]
        new = []
        for group in HEAD_GROUPS:
            scores = {h: _nt_dot(k_pairs[h // 2], qm_ref[h]) for h in group}
            new += [_softmax_update_t(jnp.minimum(scores[h], cap), _head_values(vt_ref, kb, h), ms[h], accs[h]) for h in group]
        return (tied_before + jnp.sum(_fold_rows(tied, jnp.sum), axis=0, keepdims=True),
                tuple(n[0] for n in new), tuple(n[1] for n in new))

    _, _, accs = lax.fori_loop(0, n_vis, attend_block, (jnp.zeros((1, tq), F32),) + _softmax_init(DSA_HEADS, tq))
    _write_heads_t(o_ref, accs)


def _dsa_attn(qb, qi, wi, kb, vb, ik2, *, q_off, s_real, tq, tk):
    b, t, _ = qb.shape
    s_pad = kb.shape[1]
    n_kb = s_pad // tk
    topk = min(TOPK_MAX, s_real // 4)
    qb, qi, wi = _pad_queries([qb, qi, wi], tq)
    t_pad = qb.shape[1]
    wi_t = jnp.swapaxes(wi[:, :, :SUBLANES], 1, 2)
    v_t = _key_blocks_t(vb, tk)
    tri = (lax.broadcasted_iota(I32, (tk, tk), 0) >= lax.broadcasted_iota(I32, (tk, tk), 1)).astype(BF16)
    qspec = lambda w: pl.BlockSpec((1, tq, w), lambda bi, i: (bi, i, 0))
    kspec = lambda w: pl.BlockSpec((1, s_pad, w), lambda bi, i: (bi, 0, 0), pipeline_mode=pl.Buffered(1))
    kern = functools.partial(_dsa_kernel, tq=tq, tk=tk, q_off=q_off, s_real=s_real, topk=topk)
    o_t = pl.pallas_call(
        kern, grid=(b, t_pad // tq),
        in_specs=[qspec(DSA_WIDTH), qspec(IDX_HEADS * IDX_HD),
                  pl.BlockSpec((1, SUBLANES, tq), lambda bi, i: (bi, 0, i)),
                  kspec(DSA_WIDTH),
                  pl.BlockSpec((1, n_kb, DSA_WIDTH, tk), lambda bi, i: (bi, 0, 0, 0), pipeline_mode=pl.Buffered(1)),
                  kspec(LANE), _resident((tk, tk))],
        out_specs=pl.BlockSpec((1, DSA_WIDTH, tq), lambda bi, i: (bi, 0, i)),
        out_shape=jax.ShapeDtypeStruct((b, DSA_WIDTH, t_pad), BF16),
        scratch_shapes=[pltpu.VMEM((n_kb, tk, tq), I32),
                        pltpu.VMEM((DSA_HEADS, tq, LANE), BF16), pltpu.VMEM((IDX_HEADS, tq, LANE), BF16)],
        compiler_params=pltpu.CompilerParams(dimension_semantics=("parallel", "arbitrary"),
                                             vmem_limit_bytes=VMEM_LIMIT),
        name="dsa_attn")(qb, qi, wi_t, kb, v_t, ik2, tri)
    return jnp.swapaxes(o_t, 1, 2)[:, :t]


def _pad_cols(w, width):
    return jnp.pad(w, ((0, 0), (0, width - w.shape[1])))


def _layer_weights(p, l):
    w_in = p["w_in"][l]
    off, pieces = 0, []
    for n in (MLA_Q_LORA, MLA_KV_LORA, MLA_ROPE, DSA_WIDTH, DSA_WIDTH, DSA_WIDTH, IDX_HEADS * IDX_HD, IDX_HD, IDX_HEADS):
        pieces.append(w_in[:, off:off + n])
        off += n
    c_q, c_kv, k_r, q_b, k_b, v_b, q_i, k_i, w_i = pieces
    k_r = jnp.pad(k_r, ((0, 0), (MLA_NOPE, LANE - MLA_QK)))
    w_in_p = jnp.concatenate([c_q, c_kv, k_r, q_b, k_b, v_b, q_i, k_i, k_i, _pad_cols(w_i, LANE)], axis=1)
    assert w_in_p.shape[1] == C_END

    d_lora = p["mla_w_uq"].shape[1]
    w_uq = p["mla_w_uq"][l].reshape(d_lora, MLA_HEADS, MLA_QK)
    w_uq = jnp.pad(w_uq, ((0, 0), (0, 0), (0, LANE - MLA_QK))).reshape(d_lora, MLA_HEADS * LANE)
    w_ukv = p["mla_w_ukv"][l].reshape(MLA_KV_LORA, MLA_HEADS, MLA_NOPE + MLA_V)
    w_nope = jnp.pad(w_ukv[:, :, :MLA_NOPE], ((0, 0), (0, 0), (0, LANE - MLA_NOPE))).reshape(MLA_KV_LORA, MLA_HEADS * LANE)
    w_v = w_ukv[:, :, MLA_NOPE:].reshape(MLA_KV_LORA, MLA_WIDTH)
    w_out = p["w_out"][l]

    row = lambda g: g[l][None, :].astype(F32)
    pad96 = lambda g: jnp.pad(g[l].astype(F32), (0, LANE - MLA_QK))[None, :]
    twice = lambda g: jnp.tile(g[l].astype(F32), 2)[None, :]
    lw = {
        "w_in": w_in_p.astype(BF16), "w_uq": w_uq.astype(BF16), "w_ukv_nope": w_nope.astype(BF16),
        "w_ukv_v": w_v.astype(BF16), "w_out_a": w_out[:MLA_WIDTH].astype(BF16), "w_out_b": w_out[MLA_WIDTH:].astype(BF16),
        "mix_norm": row(p["mix_norm"]), "mla_q_norm": row(p["mla_q_norm"]), "mla_kv_norm": row(p["mla_kv_norm"]),
        "mla_q_gain": pad96(p["mla_q_gain"]), "mla_k_gain": pad96(p["mla_k_gain"]),
        "dsa_q_gain": twice(p["dsa_q_gain"]), "dsa_k_gain": twice(p["dsa_k_gain"]),
    }
    for f in ("ffn1", "ffn2"):
        lw[f + "_norm"] = row(p[f + "_norm"])
        for w in ("w_gate", "w_up", "w_down"):
            lw[f + "_" + w] = p[f + "_" + w][l].astype(BF16)
    return lw


def _rope_tables(pos, rows):
    def cs(rot):
        inv = 1.0 / (ROPE_THETA ** (jnp.arange(0, rot, 2, dtype=F32) / rot))
        ang = pos.astype(F32)[:, None] * inv[None, :]
        return jnp.cos(ang), jnp.sin(ang)

    t = pos.shape[0]
    cos_a, sin_a = cs(MLA_ROPE)
    ones = lambda w: jnp.ones((t, w), F32)
    zeros = lambda w: jnp.zeros((t, w), F32)
    ca = jnp.concatenate([ones(MLA_NOPE), cos_a, cos_a, ones(LANE - MLA_QK)], axis=1)
    sa = jnp.concatenate([zeros(MLA_NOPE), -sin_a, sin_a, zeros(LANE - MLA_QK)], axis=1)
    cos_b, sin_b = cs(DSA_ROT)
    cb = jnp.tile(jnp.concatenate([cos_b, cos_b, ones(DSA_HD - DSA_ROT)], axis=1), (1, 2))
    sb = jnp.tile(jnp.concatenate([-sin_b, sin_b, zeros(DSA_HD - DSA_ROT)], axis=1), (1, 2))
    reps = max(1, rows // t)
    return tuple(jnp.tile(x, (reps, 1)) for x in (ca, sa, cb, sb))


def _pad_keys(x, s_pad):
    return jnp.pad(x, ((0, 0), (0, s_pad - x.shape[1]), (0, 0)))


def _trunk_layer(x, lw, tables, past, *, b, t, q_off, tq_mla, tk_mla, tq_dsa, tk_dsa):
    h = _ffn(x, lw["ffn1_norm"], lw["ffn1_w_gate"], lw["ffn1_w_up"], lw["ffn1_w_down"])
    (ckv, krope, kslab, kb, vb, ki, qa, qb, kb16, vb16, qi, ik2, wi) = _proj(h, lw, tables, t)
    per_batch = lambda a: a.reshape(b, t, a.shape[-1])
    if past is None:
        ckv_all, kslab_all = ckv, kslab
        kb_all, vb_all, ik2_all = per_batch(kb16), per_batch(vb16), per_batch(ik2)
        s_real = t
    else:
        p_ckv, p_krope, p_kb, p_vb, p_ki = past
        s_real = p_ckv.shape[1] + t
        cat = lambda old, new: jnp.concatenate([old, per_batch(new)], axis=1)
        ckv_all = cat(p_ckv, ckv).reshape(b * s_real, MLA_KV_LORA)
        p_kslab = jnp.pad(p_krope, ((0, 0), (0, 0), (MLA_NOPE, LANE - MLA_QK)))
        kslab_all = cat(p_kslab, kslab).reshape(b * s_real, LANE)
        kb_all = cat(p_kb.reshape(b, -1, DSA_WIDTH).astype(BF16), kb16)
        vb_all = cat(p_vb.reshape(b, -1, DSA_WIDTH).astype(BF16), vb16)
        ik2_all = cat(jnp.tile(p_ki, (1, 1, 2)).astype(BF16), ik2)
    ka, va = _mla_kv(ckv_all, kslab_all, lw)
    s_pad_a = pl.cdiv(s_real, tk_mla) * tk_mla
    ka = _pad_keys(ka.reshape(b, s_real, -1), s_pad_a)
    va = _pad_keys(va.reshape(b, s_real, -1), s_pad_a)
    oa = _mla_attn(per_batch(qa), ka, va, q_off=q_off, s_real=s_real, tq=tq_mla, tk=tk_mla)
    s_pad_b = pl.cdiv(s_real, tk_dsa) * tk_dsa
    ob = _dsa_attn(per_batch(qb), per_batch(qi), per_batch(wi), _pad_keys(kb_all, s_pad_b), _pad_keys(vb_all, s_pad_b),
                   _pad_keys(ik2_all, s_pad_b), q_off=q_off, s_real=s_real, tq=tq_dsa, tk=tk_dsa)
    y = _ffn(h, lw["ffn2_norm"], lw["ffn2_w_gate"], lw["ffn2_w_up"], lw["ffn2_w_down"],
             attn=(oa.reshape(b * t, MLA_WIDTH), ob.reshape(b * t, DSA_WIDTH), lw["w_out_a"], lw["w_out_b"]))
    rows = (per_batch(ckv), per_batch(krope), kb.reshape(b, t, DSA_HEADS, DSA_HD),
            vb.reshape(b, t, DSA_HEADS, DSA_HD), per_batch(ki))
    return y, rows


def kernel(x_prompt, x_sample, cache_mla_ckv, cache_mla_krope, cache_dsa_k, cache_dsa_v, cache_idx_k,
           ffn1_norm, ffn1_w_gate, ffn1_w_up, ffn1_w_down, mix_norm, w_in,
           mla_q_norm, mla_w_uq, mla_kv_norm, mla_w_ukv, mla_q_gain, mla_k_gain,
           dsa_q_gain, dsa_k_gain, w_out, ffn2_norm, ffn2_w_gate, ffn2_w_up, ffn2_w_down):
    params = dict(ffn1_norm=ffn1_norm, ffn1_w_gate=ffn1_w_gate, ffn1_w_up=ffn1_w_up, ffn1_w_down=ffn1_w_down,
                  mix_norm=mix_norm, w_in=w_in, mla_q_norm=mla_q_norm, mla_w_uq=mla_w_uq, mla_kv_norm=mla_kv_norm,
                  mla_w_ukv=mla_w_ukv, mla_q_gain=mla_q_gain, mla_k_gain=mla_k_gain, dsa_q_gain=dsa_q_gain,
                  dsa_k_gain=dsa_k_gain, w_out=w_out, ffn2_norm=ffn2_norm, ffn2_w_gate=ffn2_w_gate,
                  ffn2_w_up=ffn2_w_up, ffn2_w_down=ffn2_w_down)
    depth = w_in.shape[0]
    d_model = x_prompt.shape[-1]
    weights = [_layer_weights(params, l) for l in range(depth)]

    b_p, t_p = x_prompt.shape[:2]
    n_p = b_p * t_p
    tabs_p = _rope_tables(jnp.arange(t_p, dtype=I32), _row_tile(n_p, 512))
    tile_p = dict(tq_mla=min(t_p, 512), tk_mla=min(t_p, 512), tq_dsa=min(t_p, 512), tk_dsa=min(t_p, 512))
    h_p = x_prompt.reshape(n_p, d_model)
    p_rows = []
    for l in range(depth):
        h_p, rows = _trunk_layer(h_p, weights[l], tabs_p, None, b=b_p, t=t_p, q_off=0, **tile_p)
        p_rows.append(rows)

    b_s, t_s = x_sample.shape[:2]
    n_s = b_s * t_s
    past_len = cache_mla_ckv.shape[2]
    tabs_s = _rope_tables(past_len + jnp.arange(t_s, dtype=I32), _row_tile(n_s, 512))
    tile_s = dict(tq_mla=LANE, tk_mla=3 * LANE, tq_dsa=LANE, tk_dsa=3 * LANE)
    h_s = x_sample.reshape(n_s, d_model)
    s_rows = []
    for l in range(depth):
        past = (cache_mla_ckv[l], cache_mla_krope[l], cache_dsa_k[l], cache_dsa_v[l], cache_idx_k[l])
        h_s, rows = _trunk_layer(h_s, weights[l], tabs_s, past, b=b_s, t=t_s, q_off=past_len, **tile_s)
        s_rows.append(rows)

    stack = lambda rows_by_layer, i: jnp.stack([r[i] for r in rows_by_layer])
    return (h_p.reshape(b_p, t_p, d_model), h_s.reshape(b_s, t_s, d_model),
            *[stack(p_rows, i) for i in range(5)], *[stack(s_rows, i) for i in range(5)])
```

```python
import functools

import jax
import jax.numpy as jnp
from jax import lax
from jax.experimental import pallas as pl
from jax.experimental.pallas import tpu as pltpu

F32 = jnp.float32
BF16 = jnp.bfloat16
I32 = jnp.int32

CHUNK_SHIFT = 6
ROPE_THETA = 500000.0
EPS = 1e-6
MLA_HEADS = 8
MLA_NOPE = 64
MLA_ROPE = 32
MLA_QK = MLA_NOPE + MLA_ROPE
MLA_V = 64
MLA_Q_LORA = 256
MLA_KV_LORA = 128
DSA_HEADS = 8
DSA_HD = 64
DSA_ROT = 16
IDX_HEADS = 4
IDX_HD = 64
IDX_W_SCALE = (IDX_HD * IDX_HEADS) ** -0.5
TOPK_MAX = 256
DSA_WIDTH = DSA_HEADS * DSA_HD
MLA_WIDTH = MLA_HEADS * MLA_V

LANE = 128
VMEM_LIMIT = 56 * 1024 * 1024

NEG = -1e30
F32_MAX = 3.4028234e38
LOG2E = 1.4426950408889634
INT_MIN = -(2 ** 31)
INT_MAX = 2 ** 31 - 1
KEY_NEG_INF = INT_MIN + 0x7FFFFF


def _nt_dot(a, b):
    return lax.dot_general(a, b, (((1,), (1,)), ((), ())), preferred_element_type=F32)


def _tn_dot(a, b):
    return lax.dot_general(a, b, (((0,), (0,)), ((), ())), preferred_element_type=F32)


def _dot(a, b):
    return jnp.dot(a, b, preferred_element_type=F32)


def _rms(x, g):
    return x * lax.rsqrt(jnp.mean(x * x, axis=-1, keepdims=True) + EPS) * g


def _lane_iota(shape):
    return lax.broadcasted_iota(I32, shape, len(shape) - 1)


FFN_CHUNK = 256


def _ffn_body(x, g_ref, wg_ref, wu_ref, wd_ref, o_ref):
    xb = _rms(x, g_ref[...]).astype(BF16)
    d_ff = wg_ref.shape[1]
    acc = jnp.zeros(x.shape, F32)
    for c in range(d_ff // FFN_CHUNK):
        sl = slice(c * FFN_CHUNK, (c + 1) * FFN_CHUNK)
        gate = _dot(xb, wg_ref[:, sl])
        up = _dot(xb, wu_ref[:, sl])
        act = (gate * jax.nn.sigmoid(gate) * up).astype(BF16)
        acc = acc + _dot(act, wd_ref[sl, :])
    o_ref[...] = x + 0.5 * acc


def _ffn_kernel(x_ref, g_ref, wg_ref, wu_ref, wd_ref, o_ref):
    _ffn_body(x_ref[...], g_ref, wg_ref, wu_ref, wd_ref, o_ref)


def _out_ffn_kernel(h_ref, oa_ref, ob_ref, woa_ref, wob_ref, g_ref, wg_ref, wu_ref, wd_ref, o_ref):
    x = h_ref[...] + _tn_dot(oa_ref[0], woa_ref[...]) + _tn_dot(ob_ref[0], wob_ref[...])
    _ffn_body(x, g_ref, wg_ref, wu_ref, wd_ref, o_ref)


def _resident(shape):
    nd = len(shape)
    return pl.BlockSpec(shape, lambda *_: (0,) * nd, pipeline_mode=pl.Buffered(1))


def _row_tile(n, pref):
    for t in range(min(n, pref), 0, -16):
        if n % t == 0:
            return t
    raise ValueError(f"no row tile for {n} rows")


def _ffn(x, g, wg, wu, wd, attn=None):
    n, d = x.shape
    tm = _row_tile(n, 512)
    row = lambda w: pl.BlockSpec((tm, w), lambda i: (i, 0))
    w_specs = [_resident(g.shape), _resident(wg.shape), _resident(wu.shape), _resident(wd.shape)]
    params = pltpu.CompilerParams(dimension_semantics=("parallel",), vmem_limit_bytes=VMEM_LIMIT)
    out_shape = jax.ShapeDtypeStruct((n, d), F32)
    if attn is None:
        return pl.pallas_call(_ffn_kernel, grid=(n // tm,), in_specs=[row(d)] + w_specs, out_specs=row(d),
                              out_shape=out_shape, compiler_params=params, name="ffn")(x, g, wg, wu, wd)
    oa, ob, woa, wob = attn
    t = oa.shape[2]
    if t % tm:
        oa, ob = (jnp.swapaxes(o, 0, 1).reshape(1, o.shape[1], n) for o in (oa, ob))
        t = n
    cols = lambda w: pl.BlockSpec((1, w, tm), lambda i: (i // (t // tm), 0, i % (t // tm)))
    return pl.pallas_call(
        _out_ffn_kernel, grid=(n // tm,),
        in_specs=[row(d), cols(oa.shape[1]), cols(ob.shape[1]), _resident(woa.shape), _resident(wob.shape)] + w_specs,
        out_specs=row(d), out_shape=out_shape, compiler_params=params, name="out_ffn")(x, oa, ob, woa, wob, g, wg, wu, wd)


C_CQ = 0
C_CKV = C_CQ + MLA_Q_LORA
C_KR = C_CKV + MLA_KV_LORA
C_QB = C_KR + LANE
C_KB = C_QB + DSA_WIDTH
C_VB = C_KB + DSA_WIDTH
C_QI = C_VB + DSA_WIDTH
C_KI = C_QI + IDX_HEADS * IDX_HD
C_WI = C_KI + LANE
C_END = C_WI + LANE


def _rope_a(x, c, s):
    lane = _lane_iota(x.shape)
    partner = jnp.where(lane < MLA_NOPE + MLA_ROPE // 2, pltpu.roll(x, LANE - MLA_ROPE // 2, 1),
                        pltpu.roll(x, MLA_ROPE // 2, 1))
    return x * c + partner * s


def _rope_b(x, c, s):
    lane = _lane_iota(x.shape)
    half = DSA_ROT // 2
    partner = jnp.where((lane & (DSA_HD - 1)) < half, pltpu.roll(x, LANE - half, 1), pltpu.roll(x, half, 1))
    return x * c + partner * s


def _head96_norm(x, g):
    ms = jnp.sum(x * x, axis=-1, keepdims=True) * (1.0 / MLA_QK)
    return x * lax.rsqrt(ms + EPS) * g


def _head64_norm(x, g2):
    lane = _lane_iota(x.shape)
    lo = lane < DSA_HD
    sq = x * x
    s_lo = jnp.sum(jnp.where(lo, sq, 0.0), axis=-1, keepdims=True)
    s_hi = jnp.sum(jnp.where(lo, 0.0, sq), axis=-1, keepdims=True)
    ms = jnp.where(lo, s_lo, s_hi) * (1.0 / DSA_HD)
    return x * lax.rsqrt(ms + EPS) * g2


def _proj_kernel(h_ref, gmix_ref, win_ref, gq_ref, wuq_ref, gkv_ref, gqa_ref, gqb_ref, gkb_ref,
                 ca_ref, sa_ref, cb_ref, sb_ref,
                 ckv_ref, krope_ref, kslab_ref, kb_ref, vb_ref, ki_ref,
                 qa_ref, qb_ref, kb16_ref, vb16_ref, qi_ref, ik2_ref, wi_ref):
    u = _rms(h_ref[...], gmix_ref[...]).astype(BF16)
    ca, sa, cb, sb = ca_ref[...], sa_ref[...], cb_ref[...], sb_ref[...]

    def cols(start, width):
        return _dot(u, win_ref[:, start:start + width])

    cq = _rms(cols(C_CQ, MLA_Q_LORA), gq_ref[...]).astype(BF16)
    qa = _dot(cq, wuq_ref[...])
    qa_scale = MLA_QK ** -0.5 * LOG2E
    for h in range(MLA_HEADS):
        sl = slice(h * LANE, (h + 1) * LANE)
        qa_ref[:, sl] = (_head96_norm(_rope_a(qa[:, sl], ca, sa), gqa_ref[...]) * qa_scale).astype(BF16)

    ckv_ref[...] = _rms(cols(C_CKV, MLA_KV_LORA), gkv_ref[...])
    kslab = _rope_a(cols(C_KR, LANE), ca, sa)
    kslab_ref[...] = kslab
    krope_ref[...] = kslab[:, MLA_NOPE:MLA_NOPE + MLA_ROPE]

    qb = cols(C_QB, DSA_WIDTH)
    kb = cols(C_KB, DSA_WIDTH)
    qb_scale = DSA_HD ** -0.5 * LOG2E
    for p in range(DSA_WIDTH // LANE):
        sl = slice(p * LANE, (p + 1) * LANE)
        qb_ref[:, sl] = (_rope_b(_head64_norm(qb[:, sl], gqb_ref[...]), cb, sb) * qb_scale).astype(BF16)
        kp = _rope_b(_head64_norm(kb[:, sl], gkb_ref[...]), cb, sb)
        kb_ref[:, 2 * p:2 * p + 2, :] = kp.reshape(kp.shape[0], 2, DSA_HD)
        kb16_ref[:, sl] = kp.astype(BF16)
    vb = cols(C_VB, DSA_WIDTH)
    vb_ref[...] = vb.reshape(vb.shape[0], DSA_HEADS, DSA_HD)
    vb16_ref[...] = vb.astype(BF16)

    qi = cols(C_QI, IDX_HEADS * IDX_HD)
    for p in range(IDX_HEADS * IDX_HD // LANE):
        sl = slice(p * LANE, (p + 1) * LANE)
        qi_ref[:, sl] = _rope_b(qi[:, sl], cb, sb).astype(BF16)
    ik2 = _rope_b(cols(C_KI, LANE), cb, sb)
    ki_ref[...] = ik2[:, :IDX_HD]
    ik2_ref[...] = ik2.astype(BF16)
    wi_ref[...] = cols(C_WI, LANE) * IDX_W_SCALE


def _proj(h, lw, tables, t_seq):
    n, d = h.shape
    tm = _row_tile(n, 512)
    ca, sa, cb, sb = tables
    n_tab = ca.shape[0] // tm
    row = lambda w: pl.BlockSpec((tm, w), lambda i: (i, 0))
    tab = pl.BlockSpec((tm, LANE), lambda i: (i % n_tab, 0))
    consts = [lw["mix_norm"], lw["w_in"], lw["mla_q_norm"], lw["w_uq"], lw["mla_kv_norm"],
              lw["mla_q_gain"], lw["dsa_q_gain"], lw["dsa_k_gain"]]
    out_widths = [(MLA_KV_LORA, F32), (MLA_ROPE, F32), (LANE, F32), (DSA_WIDTH, F32), (DSA_WIDTH, F32), (IDX_HD, F32),
                  (MLA_HEADS * LANE, BF16), (DSA_WIDTH, BF16), (DSA_WIDTH, BF16), (DSA_WIDTH, BF16),
                  (IDX_HEADS * IDX_HD, BF16), (LANE, BF16), (LANE, F32)]
    per_head = (3, 4)
    heads_spec = pl.BlockSpec((tm, DSA_HEADS, DSA_HD), lambda i: (i, 0, 0))
    return pl.pallas_call(
        _proj_kernel, grid=(n // tm,),
        in_specs=[row(d)] + [_resident(c.shape) for c in consts] + [tab] * 4,
        out_specs=[heads_spec if i in per_head else row(w) for i, (w, _) in enumerate(out_widths)],
        out_shape=[jax.ShapeDtypeStruct((n, DSA_HEADS, DSA_HD) if i in per_head else (n, w), dt)
                   for i, (w, dt) in enumerate(out_widths)],
        compiler_params=pltpu.CompilerParams(dimension_semantics=("parallel",), vmem_limit_bytes=VMEM_LIMIT),
        name="proj")(h, *consts, ca, sa, cb, sb)


def _mla_kv_kernel(ckv_ref, kslab_ref, wn_ref, wv_ref, gk_ref, ka_ref, va_ref):
    c = ckv_ref[...].astype(BF16)
    kn = _dot(c, wn_ref[...])
    kslab = kslab_ref[...]
    for h in range(MLA_HEADS):
        sl = slice(h * LANE, (h + 1) * LANE)
        ka_ref[:, sl] = _head96_norm(kn[:, sl] + kslab, gk_ref[...]).astype(BF16)
    va_ref[...] = _dot(c, wv_ref[...]).astype(BF16)


def _mla_kv(ckv, kslab, lw):
    m = ckv.shape[0]
    tm = _row_tile(m, 512)
    row = lambda w: pl.BlockSpec((tm, w), lambda i: (i, 0))
    consts = [lw["w_ukv_nope"], lw["w_ukv_v"], lw["mla_k_gain"]]
    return pl.pallas_call(
        _mla_kv_kernel, grid=(m // tm,),
        in_specs=[row(MLA_KV_LORA), row(LANE)] + [_resident(c.shape) for c in consts],
        out_specs=[row(MLA_HEADS * LANE), row(MLA_WIDTH)],
        out_shape=[jax.ShapeDtypeStruct((m, MLA_HEADS * LANE), BF16), jax.ShapeDtypeStruct((m, MLA_WIDTH), BF16)],
        compiler_params=pltpu.CompilerParams(dimension_semantics=("parallel",), vmem_limit_bytes=VMEM_LIMIT),
        name="mla_kv")(ckv, kslab, *consts)


SUBLANES = 8
FOLD_CHAINS = 4


def _visible_t(q0, k0, tk, tq, s_real):
    q_chunk = (q0 + lax.broadcasted_iota(I32, (tk, tq), 1)) >> CHUNK_SHIFT
    k_idx = k0 + lax.broadcasted_iota(I32, (tk, tq), 0)
    return ((k_idx >> CHUNK_SHIFT) <= q_chunk) & (k_idx < s_real)


def _fold_rows(x, reduce):
    groups = x.shape[0] // SUBLANES
    if groups % FOLD_CHAINS == 0 and groups > FOLD_CHAINS:
        x = reduce(x.reshape(groups // FOLD_CHAINS, FOLD_CHAINS * SUBLANES, x.shape[1]), axis=0)
        groups = FOLD_CHAINS
    return reduce(x.reshape(groups, SUBLANES, x.shape[1]), axis=0)


HEAD_V = 64
ONES_ROWS = 16
HEAD_ROWS = HEAD_V + ONES_ROWS
HEAD_GROUPS = (range(0, 4), range(4, 8))


def _head_values(vt_ref, kb, h):
    v_t = vt_ref[0, kb, h * HEAD_V:(h + 1) * HEAD_V, :]
    return jnp.concatenate([v_t, jnp.ones((ONES_ROWS, v_t.shape[1]), v_t.dtype)], axis=0)


def _softmax_update_t(s, v_t, m, acc):
    m_new = jnp.maximum(m, jnp.max(_fold_rows(s, jnp.max), axis=0, keepdims=True))
    p = jnp.exp2(s - m_new)
    return m_new, jnp.exp2(m - m_new) * acc + _dot(v_t, p.astype(BF16))


def _softmax_init(n_heads, tq):
    return (tuple(jnp.full((1, tq), NEG, F32) for _ in range(n_heads)),
            tuple(jnp.zeros((HEAD_ROWS, tq), F32) for _ in range(n_heads)))


def _write_heads_t(o_ref, accs):
    for h, a in enumerate(accs):
        o_ref[0, h * HEAD_V:(h + 1) * HEAD_V, :] = (a[:HEAD_V] / a[HEAD_V:HEAD_V + 1]).astype(o_ref.dtype)


def _pad_queries(arrays, tq):
    t = arrays[0].shape[1]
    t_pad = pl.cdiv(t, tq) * tq
    return [jnp.pad(a, ((0, 0), (0, t_pad - t), (0, 0))) for a in arrays] if t_pad != t else list(arrays)


def _visible_blocks(q0, tq, tk, s_real):
    n_full = jnp.minimum(((q0 >> CHUNK_SHIFT) + 1) << CHUNK_SHIFT, s_real) // tk
    vis_end = jnp.minimum((((q0 + tq - 1) >> CHUNK_SHIFT) + 1) << CHUNK_SHIFT, s_real)
    return n_full, (vis_end + tk - 1) // tk


def _mla_attn_kernel(q_ref, k_ref, vt_ref, o_ref, *, tq, tk, q_off, s_real):
    q0 = q_off + pl.program_id(1) * tq
    n_full, n_vis = _visible_blocks(q0, tq, tk, s_real)

    def attend_block(kb, carry, masked):
        ms, accs = carry
        k0 = pl.multiple_of(kb * tk, tk)
        if masked:
            cap = jnp.where(_visible_t(q0, k0, tk, tq, s_real), F32_MAX, NEG)
        new = []
        for group in HEAD_GROUPS:
            scores = {h: _nt_dot(k_ref[0, pl.ds(k0, tk), h * LANE:(h + 1) * LANE], q_ref[0, :, h * LANE:(h + 1) * LANE])
                      for h in group}
            if masked:
                scores = {h: jnp.minimum(s, cap) for h, s in scores.items()}
            new += [_softmax_update_t(scores[h], _head_values(vt_ref, kb, h), ms[h], accs[h]) for h in group]
        return tuple(n[0] for n in new), tuple(n[1] for n in new)

    carry = _softmax_init(MLA_HEADS, tq)
    carry = lax.fori_loop(0, n_full, functools.partial(attend_block, masked=False), carry)
    _, accs = lax.fori_loop(n_full, n_vis, functools.partial(attend_block, masked=True), carry)
    _write_heads_t(o_ref, accs)


def _key_blocks_t(v, tk):
    b, s_pad, w = v.shape
    return jnp.swapaxes(v.reshape(b, s_pad // tk, tk, w), 2, 3)


def _mla_attn(qa, ka, va, *, q_off, s_real, tq, tk):
    b, t, _ = qa.shape
    s_pad = ka.shape[1]
    (qa,) = _pad_queries([qa], tq)
    t_pad = qa.shape[1]
    kern = functools.partial(_mla_attn_kernel, tq=tq, tk=tk, q_off=q_off, s_real=s_real)
    o_t = pl.pallas_call(
        kern, grid=(b, t_pad // tq),
        in_specs=[pl.BlockSpec((1, tq, MLA_HEADS * LANE), lambda bi, qi: (bi, qi, 0)),
                  pl.BlockSpec((1, s_pad, MLA_HEADS * LANE), lambda bi, qi: (bi, 0, 0), pipeline_mode=pl.Buffered(1)),
                  pl.BlockSpec((1, s_pad // tk, MLA_WIDTH, tk), lambda bi, qi: (bi, 0, 0, 0),
                               pipeline_mode=pl.Buffered(1))],
        out_specs=pl.BlockSpec((1, MLA_WIDTH, tq), lambda bi, qi: (bi, 0, qi)),
        out_shape=jax.ShapeDtypeStruct((b, MLA_WIDTH, t_pad), BF16),
        compiler_params=pltpu.CompilerParams(dimension_semantics=("parallel", "arbitrary"),
                                             vmem_limit_bytes=VMEM_LIMIT),
        name="mla_attn")(qa, ka, _key_blocks_t(va, tk))
    return o_t[:, :, :t]


FALSE_POSITION_PROBES = 24
MAX_PROBES = 64
COUNT_ROWS = 128


def _sortable(bits):
    return bits ^ ((bits >> 31) & INT_MAX)


def _dsa_kernel(qb_ref, qi_ref, wi_ref, kb_ref, vt_ref, ik2_ref, tri_ref, o_ref,
                key_ref, qm_ref, qim_ref, *, tq, tk, q_off, s_real, topk):
    q0 = q_off + pl.program_id(1) * tq
    n_full, n_vis = _visible_blocks(q0, tq, tk, s_real)
    lane = _lane_iota((tq, LANE))
    lo = lane < LANE // 2

    for h in range(DSA_HEADS):
        pair = qb_ref[0, :, (h // 2) * LANE:(h // 2 + 1) * LANE]
        qm_ref[h] = jnp.where(lo if h % 2 == 0 else ~lo, pair, jnp.zeros_like(pair))
    for h in range(IDX_HEADS):
        pair = qi_ref[0, :, (h // 2) * LANE:(h // 2 + 1) * LANE]
        qim_ref[h] = jnp.where(lo if h % 2 == 0 else ~lo, pair, jnp.zeros_like(pair))
    w_rows = [wi_ref[0, h:h + 1, :] for h in range(IDX_HEADS)]

    def score_block(kb, carry, masked):
        q_max, q_min = carry
        k0 = pl.multiple_of(kb * tk, tk)
        ik = ik2_ref[0, pl.ds(k0, tk), :]
        score = jnp.zeros((tk, tq), F32)
        for h in range(IDX_HEADS):
            score = score + jnp.maximum(_nt_dot(ik, qim_ref[h]), 0.0) * w_rows[h]
        score = jnp.where(score == 0.0, 0.0, score)
        below = above = score
        if masked:
            vis = _visible_t(q0, k0, tk, tq, s_real)
            below, above = jnp.where(vis, score, -jnp.inf), jnp.where(vis, score, jnp.inf)
        key_ref[kb] = _sortable(pltpu.bitcast(below, I32))
        return (jnp.maximum(q_max, _fold_rows(below, jnp.max)), jnp.minimum(q_min, _fold_rows(above, jnp.min)))

    carry = (jnp.full((SUBLANES, tq), -jnp.inf, F32), jnp.full((SUBLANES, tq), jnp.inf, F32))
    carry = lax.fori_loop(0, n_full, functools.partial(score_block, masked=False), carry)
    q_max, q_min = lax.fori_loop(n_full, n_vis, functools.partial(score_block, masked=True), carry)
    row_max = jnp.max(q_max, axis=0, keepdims=True)
    row_min = jnp.min(q_min, axis=0, keepdims=True)

    def count_ge(t):
        def body(kb, acc):
            for r in range(0, tk, COUNT_ROWS):
                acc = acc + _fold_rows(jnp.where(key_ref[kb, r:r + COUNT_ROWS, :] >= t, 1.0, 0.0), jnp.sum)
            return acc
        return jnp.sum(lax.fori_loop(0, n_vis, body, jnp.zeros((SUBLANES, tq), F32)), axis=0, keepdims=True)

    kf = float(topk)
    q_pos = q0 + lax.broadcasted_iota(I32, (1, tq), 1)
    n_row = jnp.minimum(((q_pos >> CHUNK_SHIFT) + 1) << CHUNK_SHIFT, s_real).astype(F32)
    few = n_row < kf
    lo0 = _sortable(lax.bitcast_convert_type(row_min, I32))
    hi0 = _sortable(lax.bitcast_convert_type(row_max, I32)) + 1

    def finished(lo_k, hi_k, c_lo):
        return few | (c_lo == kf) | (hi_k == lo_k + 1)

    def probe_step(carry):
        it, _, lo_k, hi_k, c_lo, c_hi, g_lo, g_hi, last = carry
        v_lo = lax.bitcast_convert_type(_sortable(lo_k), F32)
        v_hi = lax.bitcast_convert_type(_sortable(hi_k), F32)
        a = jnp.log(c_lo * (1.0 / kf)) * g_lo
        b = jnp.log(kf / jnp.maximum(c_hi, 0.5)) * g_hi
        p = _sortable(lax.bitcast_convert_type(v_lo + (v_hi - v_lo) * (a / (a + b)), I32))
        p = jnp.where(it >= FALSE_POSITION_PROBES, (lo_k >> 1) + (hi_k >> 1) + (lo_k & hi_k & 1), p)
        p = jnp.where((it == 0) & (lo_k < 0) & (hi_k > 0), 0, p)
        p = jnp.where(lo_k == 0, 1, p)
        p = jnp.minimum(jnp.maximum(p, lo_k + 1), hi_k - 1)
        c = count_ge(p)
        open_ = ~finished(lo_k, hi_k, c_lo)
        up = open_ & (c >= kf)
        down = open_ & (c < kf)
        lo_k, c_lo = jnp.where(up, p, lo_k), jnp.where(up, c, c_lo)
        hi_k, c_hi = jnp.where(down, p, hi_k), jnp.where(down, c, c_hi)
        g_lo = jnp.where(down, jnp.where(last < 0.0, 0.5 * g_lo, 1.0), jnp.where(up, 1.0, g_lo))
        g_hi = jnp.where(up, jnp.where(last > 0.0, 0.5 * g_hi, 1.0), jnp.where(down, 1.0, g_hi))
        last = jnp.where(up, 1.0, jnp.where(down, -1.0, last))
        n_open = jnp.max(jnp.where(finished(lo_k, hi_k, c_lo), 0, 1))
        return it + 1, n_open, lo_k, hi_k, c_lo, c_hi, g_lo, g_hi, last

    ones = jnp.ones((1, tq), F32)
    init = (jnp.int32(0), jnp.max(jnp.where(finished(lo0, hi0, n_row), 0, 1)), lo0, hi0, n_row, 0.0 * ones,
            ones, ones, 0.0 * ones)
    final = lax.while_loop(lambda c: (c[1] > 0) & (c[0] < MAX_PROBES), probe_step, init)
    t, c_lo, c_hi = final[2], final[4], final[5]

    t = jnp.where(few, KEY_NEG_INF, t)
    need = jnp.where(few, 0.0, jnp.where(c_lo == kf, kf, kf - c_hi))

    def attend_block(kb, carry):
        tied_before, ms, accs = carry
        k0 = pl.multiple_of(kb * tk, tk)
        blk = key_ref[kb]
        tied = jnp.where(blk == t, 1.0, 0.0)
        tied_rank = (tied_before + _dot(tri_ref[...], tied.astype(BF16))) * tied
        cap = jnp.where((blk >= t) & (tied_rank <= need), F32_MAX, NEG)
        k_pairs = [kb_ref[0, pl.ds(k0, tk), p * LANE:(p + 1) * LANE] for p in range(DSA_HEADS // 2)]
        new = []
        for group in HEAD_GROUPS:
            scores = {h: _nt_dot(k_pairs[h // 2], qm_ref[h]) for h in group}
            new += [_softmax_update_t(jnp.minimum(scores[h], cap), _head_values(vt_ref, kb, h), ms[h], accs[h]) for h in group]
        return (tied_before + jnp.sum(_fold_rows(tied, jnp.sum), axis=0, keepdims=True),
                tuple(n[0] for n in new), tuple(n[1] for n in new))

    _, _, accs = lax.fori_loop(0, n_vis, attend_block, (jnp.zeros((1, tq), F32),) + _softmax_init(DSA_HEADS, tq))
    _write_heads_t(o_ref, accs)


def _dsa_attn(qb, qi, wi, kb, vb, ik2, *, q_off, s_real, tq, tk):
    b, t, _ = qb.shape
    s_pad = kb.shape[1]
    n_kb = s_pad // tk
    topk = min(TOPK_MAX, s_real // 4)
    qb, qi, wi = _pad_queries([qb, qi, wi], tq)
    t_pad = qb.shape[1]
    wi_t = jnp.swapaxes(wi[:, :, :SUBLANES], 1, 2)
    v_t = _key_blocks_t(vb, tk)
    tri = (lax.broadcasted_iota(I32, (tk, tk), 0) >= lax.broadcasted_iota(I32, (tk, tk), 1)).astype(BF16)
    qspec = lambda w: pl.BlockSpec((1, tq, w), lambda bi, i: (bi, i, 0))
    kspec = lambda w: pl.BlockSpec((1, s_pad, w), lambda bi, i: (bi, 0, 0), pipeline_mode=pl.Buffered(1))
    kern = functools.partial(_dsa_kernel, tq=tq, tk=tk, q_off=q_off, s_real=s_real, topk=topk)
    o_t = pl.pallas_call(
        kern, grid=(b, t_pad // tq),
        in_specs=[qspec(DSA_WIDTH), qspec(IDX_HEADS * IDX_HD),
                  pl.BlockSpec((1, SUBLANES, tq), lambda bi, i: (bi, 0, i)),
                  kspec(DSA_WIDTH),
                  pl.BlockSpec((1, n_kb, DSA_WIDTH, tk), lambda bi, i: (bi, 0, 0, 0), pipeline_mode=pl.Buffered(1)),
                  kspec(LANE), _resident((tk, tk))],
        out_specs=pl.BlockSpec((1, DSA_WIDTH, tq), lambda bi, i: (bi, 0, i)),
        out_shape=jax.ShapeDtypeStruct((b, DSA_WIDTH, t_pad), BF16),
        scratch_shapes=[pltpu.VMEM((n_kb, tk, tq), I32),
                        pltpu.VMEM((DSA_HEADS, tq, LANE), BF16), pltpu.VMEM((IDX_HEADS, tq, LANE), BF16)],
        compiler_params=pltpu.CompilerParams(dimension_semantics=("parallel", "arbitrary"),
                                             vmem_limit_bytes=VMEM_LIMIT),
        name="dsa_attn")(qb, qi, wi_t, kb, v_t, ik2, tri)
    return o_t[:, :, :t]


def _pad_cols(w, width):
    return jnp.pad(w, ((0, 0), (0, width - w.shape[1])))


def _layer_weights(p, l):
    w_in = p["w_in"][l]
    off, pieces = 0, []
    for n in (MLA_Q_LORA, MLA_KV_LORA, MLA_ROPE, DSA_WIDTH, DSA_WIDTH, DSA_WIDTH, IDX_HEADS * IDX_HD, IDX_HD, IDX_HEADS):
        pieces.append(w_in[:, off:off + n])
        off += n
    c_q, c_kv, k_r, q_b, k_b, v_b, q_i, k_i, w_i = pieces
    k_r = jnp.pad(k_r, ((0, 0), (MLA_NOPE, LANE - MLA_QK)))
    w_in_p = jnp.concatenate([c_q, c_kv, k_r, q_b, k_b, v_b, q_i, k_i, k_i, _pad_cols(w_i, LANE)], axis=1)
    assert w_in_p.shape[1] == C_END

    d_lora = p["mla_w_uq"].shape[1]
    w_uq = p["mla_w_uq"][l].reshape(d_lora, MLA_HEADS, MLA_QK)
    w_uq = jnp.pad(w_uq, ((0, 0), (0, 0), (0, LANE - MLA_QK))).reshape(d_lora, MLA_HEADS * LANE)
    w_ukv = p["mla_w_ukv"][l].reshape(MLA_KV_LORA, MLA_HEADS, MLA_NOPE + MLA_V)
    w_nope = jnp.pad(w_ukv[:, :, :MLA_NOPE], ((0, 0), (0, 0), (0, LANE - MLA_NOPE))).reshape(MLA_KV_LORA, MLA_HEADS * LANE)
    w_v = w_ukv[:, :, MLA_NOPE:].reshape(MLA_KV_LORA, MLA_WIDTH)
    w_out = p["w_out"][l]

    row = lambda g: g[l][None, :].astype(F32)
    pad96 = lambda g: jnp.pad(g[l].astype(F32), (0, LANE - MLA_QK))[None, :]
    twice = lambda g: jnp.tile(g[l].astype(F32), 2)[None, :]
    lw = {
        "w_in": w_in_p.astype(BF16), "w_uq": w_uq.astype(BF16), "w_ukv_nope": w_nope.astype(BF16),
        "w_ukv_v": w_v.astype(BF16), "w_out_a": w_out[:MLA_WIDTH].astype(BF16), "w_out_b": w_out[MLA_WIDTH:].astype(BF16),
        "mix_norm": row(p["mix_norm"]), "mla_q_norm": row(p["mla_q_norm"]), "mla_kv_norm": row(p["mla_kv_norm"]),
        "mla_q_gain": pad96(p["mla_q_gain"]), "mla_k_gain": pad96(p["mla_k_gain"]),
        "dsa_q_gain": twice(p["dsa_q_gain"]), "dsa_k_gain": twice(p["dsa_k_gain"]),
    }
    for f in ("ffn1", "ffn2"):
        lw[f + "_norm"] = row(p[f + "_norm"])
        for w in ("w_gate", "w_up", "w_down"):
            lw[f + "_" + w] = p[f + "_" + w][l].astype(BF16)
    return lw


def _rope_tables(pos, rows):
    def cs(rot):
        inv = 1.0 / (ROPE_THETA ** (jnp.arange(0, rot, 2, dtype=F32) / rot))
        ang = pos.astype(F32)[:, None] * inv[None, :]
        return jnp.cos(ang), jnp.sin(ang)

    t = pos.shape[0]
    cos_a, sin_a = cs(MLA_ROPE)
    ones = lambda w: jnp.ones((t, w), F32)
    zeros = lambda w: jnp.zeros((t, w), F32)
    ca = jnp.concatenate([ones(MLA_NOPE), cos_a, cos_a, ones(LANE - MLA_QK)], axis=1)
    sa = jnp.concatenate([zeros(MLA_NOPE), -sin_a, sin_a, zeros(LANE - MLA_QK)], axis=1)
    cos_b, sin_b = cs(DSA_ROT)
    cb = jnp.tile(jnp.concatenate([cos_b, cos_b, ones(DSA_HD - DSA_ROT)], axis=1), (1, 2))
    sb = jnp.tile(jnp.concatenate([-sin_b, sin_b, zeros(DSA_HD - DSA_ROT)], axis=1), (1, 2))
    reps = max(1, rows // t)
    return tuple(jnp.tile(x, (reps, 1)) for x in (ca, sa, cb, sb))


def _pad_keys(x, s_pad):
    return jnp.pad(x, ((0, 0), (0, s_pad - x.shape[1]), (0, 0)))


def _trunk_layer(x, lw, tables, past, *, b, t, q_off, tq_mla, tk_mla, tq_dsa, tk_dsa):
    h = _ffn(x, lw["ffn1_norm"], lw["ffn1_w_gate"], lw["ffn1_w_up"], lw["ffn1_w_down"])
    (ckv, krope, kslab, kb, vb, ki, qa, qb, kb16, vb16, qi, ik2, wi) = _proj(h, lw, tables, t)
    per_batch = lambda a: a.reshape(b, t, a.shape[-1])
    if past is None:
        ckv_all, kslab_all = ckv, kslab
        kb_all, vb_all, ik2_all = per_batch(kb16), per_batch(vb16), per_batch(ik2)
        s_real = t
    else:
        p_ckv, p_krope, p_kb, p_vb, p_ki = past
        s_real = p_ckv.shape[1] + t
        cat = lambda old, new: jnp.concatenate([old, per_batch(new)], axis=1)
        ckv_all = cat(p_ckv, ckv).reshape(b * s_real, MLA_KV_LORA)
        p_kslab = jnp.pad(p_krope, ((0, 0), (0, 0), (MLA_NOPE, LANE - MLA_QK)))
        kslab_all = cat(p_kslab, kslab).reshape(b * s_real, LANE)
        kb_all = cat(p_kb.reshape(b, -1, DSA_WIDTH).astype(BF16), kb16)
        vb_all = cat(p_vb.reshape(b, -1, DSA_WIDTH).astype(BF16), vb16)
        ik2_all = cat(jnp.tile(p_ki, (1, 1, 2)).astype(BF16), ik2)
    ka, va = _mla_kv(ckv_all, kslab_all, lw)
    s_pad_a = pl.cdiv(s_real, tk_mla) * tk_mla
    ka = _pad_keys(ka.reshape(b, s_real, -1), s_pad_a)
    va = _pad_keys(va.reshape(b, s_real, -1), s_pad_a)
    oa = _mla_attn(per_batch(qa), ka, va, q_off=q_off, s_real=s_real, tq=tq_mla, tk=tk_mla)
    s_pad_b = pl.cdiv(s_real, tk_dsa) * tk_dsa
    ob = _dsa_attn(per_batch(qb), per_batch(qi), per_batch(wi), _pad_keys(kb_all, s_pad_b), _pad_keys(vb_all, s_pad_b),
                   _pad_keys(ik2_all, s_pad_b), q_off=q_off, s_real=s_real, tq=tq_dsa, tk=tk_dsa)
    y = _ffn(h, lw["ffn2_norm"], lw["ffn2_w_gate"], lw["ffn2_w_up"], lw["ffn2_w_down"],
             attn=(oa, ob, lw["w_out_a"], lw["w_out_b"]))
    rows = (per_batch(ckv), per_batch(krope), kb.reshape(b, t, DSA_HEADS, DSA_HD),
            vb.reshape(b, t, DSA_HEADS, DSA_HD), per_batch(ki))
    return y, rows


def kernel(x_prompt, x_sample, cache_mla_ckv, cache_mla_krope, cache_dsa_k, cache_dsa_v, cache_idx_k,
           ffn1_norm, ffn1_w_gate, ffn1_w_up, ffn1_w_down, mix_norm, w_in,
           mla_q_norm, mla_w_uq, mla_kv_norm, mla_w_ukv, mla_q_gain, mla_k_gain,
           dsa_q_gain, dsa_k_gain, w_out, ffn2_norm, ffn2_w_gate, ffn2_w_up, ffn2_w_down):
    params = dict(ffn1_norm=ffn1_norm, ffn1_w_gate=ffn1_w_gate, ffn1_w_up=ffn1_w_up, ffn1_w_down=ffn1_w_down,
                  mix_norm=mix_norm, w_in=w_in, mla_q_norm=mla_q_norm, mla_w_uq=mla_w_uq, mla_kv_norm=mla_kv_norm,
                  mla_w_ukv=mla_w_ukv, mla_q_gain=mla_q_gain, mla_k_gain=mla_k_gain, dsa_q_gain=dsa_q_gain,
                  dsa_k_gain=dsa_k_gain, w_out=w_out, ffn2_norm=ffn2_norm, ffn2_w_gate=ffn2_w_gate,
                  ffn2_w_up=ffn2_w_up, ffn2_w_down=ffn2_w_down)
    depth = w_in.shape[0]
    d_model = x_prompt.shape[-1]
    weights = [_layer_weights(params, l) for l in range(depth)]

    b_p, t_p = x_prompt.shape[:2]
    n_p = b_p * t_p
    tabs_p = _rope_tables(jnp.arange(t_p, dtype=I32), _row_tile(n_p, 512))
    tile_p = dict(tq_mla=min(t_p, 512), tk_mla=min(t_p, 512), tq_dsa=min(t_p, 512), tk_dsa=min(t_p, 512))
    h_p = x_prompt.reshape(n_p, d_model)
    p_rows = []
    for l in range(depth):
        h_p, rows = _trunk_layer(h_p, weights[l], tabs_p, None, b=b_p, t=t_p, q_off=0, **tile_p)
        p_rows.append(rows)

    b_s, t_s = x_sample.shape[:2]
    n_s = b_s * t_s
    past_len = cache_mla_ckv.shape[2]
    tabs_s = _rope_tables(past_len + jnp.arange(t_s, dtype=I32), _row_tile(n_s, 512))
    tile_s = dict(tq_mla=LANE, tk_mla=3 * LANE, tq_dsa=LANE, tk_dsa=3 * LANE)
    h_s = x_sample.reshape(n_s, d_model)
    s_rows = []
    for l in range(depth):
        past = (cache_mla_ckv[l], cache_mla_krope[l], cache_dsa_k[l], cache_dsa_v[l], cache_idx_k[l])
        h_s, rows = _trunk_layer(h_s, weights[l], tabs_s, past, b=b_s, t=t_s, q_off=past_len, **tile_s)
        s_rows.append(rows)

    stack = lambda rows_by_layer, i: jnp.stack([r[i] for r in rows_by_layer])
    return (h_p.reshape(b_p, t_p, d_model), h_s.reshape(b_s, t_s, d_model),
            *[stack(p_rows, i) for i in range(5)], *[stack(s_rows, i) for i in range(5)])
```
